```python
import jax, jax.numpy as jnp
from jax import lax
import numpy as np

D_MODEL = 1024
BATCH = 16
SEQ = 2048
DEPTH = 1

N_HEADS = 8
HEAD_DIM = 64
ATTN_WIDTH = N_HEADS * HEAD_DIM
IDX_HEADS = 8
IDX_DIM = 64
IDX_SCALE = (IDX_HEADS ** -0.5) * (IDX_DIM ** -0.5)
TOPK_MAX = 256
Q_BLOCK = 128
POOL_WINDOWS = (2, 4, 8, 16)
POOL_GROUPS = len(POOL_WINDOWS)
POOL_WIDTH = 512
POOL_GROUP_WIDTH = POOL_WIDTH // POOL_GROUPS
N_BRANCHES = 2
D_FF = -(-8 * D_MODEL // (3 * 256)) * 256
RMS_EPS = 1e-6
NEG_BIG = -1e30

IN_COLS = (ATTN_WIDTH, HEAD_DIM, HEAD_DIM, IDX_HEADS * IDX_DIM, IDX_DIM, IDX_HEADS,
           POOL_WIDTH, N_BRANCHES * D_MODEL)
D_IN = int(sum(IN_COLS))
IN_SPLITS = [int(c) for c in np.cumsum(IN_COLS)[:-1]]

kernel_name = "hybrid_dsa_pool_gated_block"


def rms_norm(x, gain):
    xf = x.astype(jnp.float32)
    y = xf * lax.rsqrt(jnp.mean(xf * xf, axis=-1, keepdims=True) + RMS_EPS)
    return (y * gain.astype(jnp.float32)).astype(x.dtype)


def alibi_slopes(n_heads):
    return 2.0 ** (-8.0 * jnp.arange(1, n_heads + 1, dtype=jnp.float32) / n_heads)


def dsa_attention(q, k, v, q_idx, k_idx, w_idx):
    B, S = q.shape[0], q.shape[1]
    n_keep = min(TOPK_MAX, S // 4)
    nb = S // Q_BLOCK
    slopes = alibi_slopes(N_HEADS)
    key_pos = jnp.arange(S, dtype=jnp.int32)
    scale = HEAD_DIM ** -0.5

    def to_blocks(a):
        return a.reshape((B, nb, Q_BLOCK) + a.shape[2:]).swapaxes(0, 1)

    def one_block(args):
        qb, qib, wb, tb = args
        s = jnp.einsum('bqhd,bsd->bqhs', qib, k_idx)
        s = jnp.einsum('bqhs,bqh->bqs', jax.nn.relu(s), wb).astype(jnp.float32)
        causal = key_pos[None, :] <= tb[:, None]
        s = jnp.where(causal[None], s, -jnp.inf)
        _, idx = lax.top_k(s, n_keep)
        k_sel = jax.vmap(lambda kb, ib: kb[ib])(k, idx)
        v_sel = jax.vmap(lambda vb, ib: vb[ib])(v, idx)
        logits = jnp.einsum('bqhd,bqkd->bhqk', qb, k_sel).astype(jnp.float32) * scale
        dist = tb[None, :, None] - idx
        logits = logits - slopes[None, :, None, None] * dist[:, None].astype(jnp.float32)
        logits = jnp.where((dist >= 0)[:, None], logits, NEG_BIG)
        p = jax.nn.softmax(logits, axis=-1).astype(v.dtype)
        return jnp.einsum('bhqk,bqkd->bqhd', p, v_sel)

    tpos = jnp.arange(S, dtype=jnp.int32).reshape(nb, Q_BLOCK)
    out = lax.map(one_block, (to_blocks(q), to_blocks(q_idx), to_blocks(w_idx), tpos))
    return out.swapaxes(0, 1).reshape(B, S, ATTN_WIDTH)


def multiscale_pool(u, w_group, scale):
    B, S, P = u.shape
    uf = u.astype(jnp.float32)
    c0 = jnp.concatenate([jnp.zeros((B, 1, P), jnp.float32), jnp.cumsum(uf, axis=1)], axis=1)
    t = jnp.arange(S)
    outs = []
    for g, w in enumerate(POOL_WINDOWS):
        sl = slice(g * POOL_GROUP_WIDTH, (g + 1) * POOL_GROUP_WIDTH)
        cg = c0[..., sl]
        start = jnp.concatenate([jnp.zeros((B, w - 1, POOL_GROUP_WIDTH), jnp.float32),
                                 cg[:, :S - w + 1]], axis=1)
        count = jnp.minimum(t + 1, w).astype(jnp.float32)[None, :, None]
        mean = (cg[:, 1:] - start) / count
        outs.append((mean - uf[..., sl]).astype(u.dtype) @ w_group[g])
    return jnp.concatenate(outs, axis=-1) * scale


def setup_inputs(seed: int = 0) -> dict:
    key = jax.random.key(seed)
    ks = jax.random.split(key, 16)
    f32 = jnp.float32

    def w(k, shape, fan_in):
        return jax.random.normal(k, shape, f32) * (fan_in ** -0.5)

    def gain(k, shape):
        return 1.0 + 0.02 * jax.random.normal(k, shape, f32)

    L = DEPTH
    return {
        "x": jax.random.normal(ks[0], (BATCH, SEQ, D_MODEL), f32),
        "norm_mix": gain(ks[1], (L, D_MODEL)),
        "w_in": w(ks[2], (L, D_MODEL, D_IN), D_MODEL),
        "q_norm": gain(ks[3], (L, HEAD_DIM)),
        "k_norm": gain(ks[4], (L, HEAD_DIM)),
        "w_pool_group": w(ks[5], (L, POOL_GROUPS, POOL_GROUP_WIDTH, POOL_GROUP_WIDTH), POOL_GROUP_WIDTH),
        "pool_scale": gain(ks[6], (L, POOL_WIDTH)),
        "w_branch_attn": w(ks[7], (L, ATTN_WIDTH, D_MODEL), ATTN_WIDTH),
        "w_branch_pool": w(ks[8], (L, POOL_WIDTH, D_MODEL), POOL_WIDTH),
        "w_out": w(ks[9], (L, D_MODEL, D_MODEL), D_MODEL),
        "norm_ffn": gain(ks[10], (L, D_MODEL)),
        "w_ffn_gate": w(ks[11], (L, D_MODEL, D_FF), D_MODEL),
        "w_ffn_up": w(ks[12], (L, D_MODEL, D_FF), D_MODEL),
        "w_ffn_down": w(ks[13], (L, D_FF, D_MODEL), D_FF),
    }


def reference(x, norm_mix, w_in, q_norm, k_norm, w_pool_group, pool_scale, w_branch_attn,
              w_branch_pool, w_out, norm_ffn, w_ffn_gate, w_ffn_up, w_ffn_down):
    B, S = x.shape[0], x.shape[1]
    for l in range(DEPTH):
        h = rms_norm(x, norm_mix[l])
        proj = h @ w_in[l]
        q, k, v, qi, ki, wi, u, gates = jnp.split(proj, IN_SPLITS, axis=-1)
        q = rms_norm(q.reshape(B, S, N_HEADS, HEAD_DIM), q_norm[l])
        k = rms_norm(k, k_norm[l])
        qi = qi.reshape(B, S, IDX_HEADS, IDX_DIM)
        wi = wi * IDX_SCALE
        a = dsa_attention(q, k, v, qi, ki, wi)
        p = multiscale_pool(u, w_pool_group[l], pool_scale[l])
        g_attn, g_pool = jnp.split(jax.nn.sigmoid(gates), N_BRANCHES, axis=-1)
        merged = g_attn * (a @ w_branch_attn[l]) + g_pool * (p @ w_branch_pool[l])
        x = x + merged @ w_out[l]
        h2 = rms_norm(x, norm_ffn[l])
        x = x + (jax.nn.silu(h2 @ w_ffn_gate[l]) * (h2 @ w_ffn_up[l])) @ w_ffn_down[l]
    return x
```

```python
import functools

import jax
import jax.numpy as jnp
from jax import lax
from jax.experimental import pallas as pl
from jax.experimental.pallas import tpu as pltpu

D_MODEL = 1024
N_HEADS = 8
HEAD_DIM = 64
ATTN_WIDTH = N_HEADS * HEAD_DIM
IDX_HEADS = 8
IDX_DIM = 64
IDX_SCALE = (IDX_HEADS ** -0.5) * (IDX_DIM ** -0.5)
TOPK = 256
POOL_WINDOWS = (2, 4, 8, 16)
POOL_WIDTH = 512
POOL_GROUP_WIDTH = 128
D_FF = 2816
RMS_EPS = 1e-6
NEG_BIG = -1e30

LANES = 128
KEY_CHUNK = 128
AUG = 128
VT_ROWS = 80
HALO = 16
INT_MIN = -2147483648
VMEM_LIMIT = 56 * 1024 * 1024

T_Q0, T_V0, T_QI0, T_WI0, T_ROWS = 0, 512, 576, 1088, 1104
N_KK0, N_U0, N_G0, N_COLS = 0, 128, 640, 2688


def _rms(x, eps=RMS_EPS):
    return x * lax.rsqrt(jnp.mean(x * x, axis=-1, keepdims=True) + eps)


def _proj_kernel(x_ref, gmix_ref, wt_ref, wn_ref, gq_ref, gk_ref,
                 qt_ref, qit_ref, wit_ref, vt_ref, kaug_ref, kiaug_ref, u_ref, g_ref,
                 *, tm, seq):
    i = pl.program_id(0)
    x = x_ref[...]
    h = (_rms(x) * gmix_ref[...]).astype(jnp.bfloat16)
    y_t = lax.dot_general(wt_ref[...], h, (((1,), (1,)), ((), ())),
                          preferred_element_type=jnp.float32)
    y_n = jnp.dot(h, wn_ref[...], preferred_element_type=jnp.float32)

    gq = gq_ref[...]
    for hd in range(N_HEADS):
        blk = y_t[T_Q0 + hd * HEAD_DIM:T_Q0 + (hd + 1) * HEAD_DIM, :]
        ss = jnp.mean(blk * blk, axis=0, keepdims=True)
        qn = blk * lax.rsqrt(ss + RMS_EPS) * gq * (HEAD_DIM ** -0.5)
        qt_ref[0, hd * HEAD_DIM:(hd + 1) * HEAD_DIM, :] = qn.astype(qt_ref.dtype)

    qit_ref[0] = y_t[T_QI0:T_QI0 + IDX_HEADS * IDX_DIM, :].astype(qit_ref.dtype)
    wit_ref[0] = y_t[T_WI0:T_WI0 + IDX_HEADS, :] * IDX_SCALE

    v_t = y_t[T_V0:T_V0 + HEAD_DIM, :]
    row = lax.broadcasted_iota(jnp.int32, (VT_ROWS - HEAD_DIM, tm), 0)
    tail = jnp.where(row == 0, 1.0, 0.0)
    vt_full = jnp.concatenate([v_t, tail], axis=0).astype(vt_ref.dtype)
    for cc in range(tm // KEY_CHUNK):
        vt_ref[0, cc] = vt_full[:, cc * KEY_CHUNK:(cc + 1) * KEY_CHUNK]

    kk = y_n[:, N_KK0:N_KK0 + LANES]
    lane = lax.broadcasted_iota(jnp.int32, (tm, LANES), 1)
    is_k = lane < HEAD_DIM
    ss = jnp.sum(jnp.where(is_k, kk * kk, 0.0), axis=-1, keepdims=True) * (1.0 / HEAD_DIM)
    kn = kk * lax.rsqrt(ss + RMS_EPS) * gk_ref[...]
    pos = (i * tm) % seq + lax.broadcasted_iota(jnp.int32, (tm, LANES), 0)
    pos_hi = ((pos >> 8) << 8).astype(jnp.float32)
    pos_lo = (pos & 255).astype(jnp.float32)
    feats = jnp.where(lane == HEAD_DIM, pos_hi, jnp.where(lane == HEAD_DIM + 1, pos_lo, 0.0))
    kaug_ref[0] = jnp.where(is_k, kn, feats).astype(kaug_ref.dtype)
    kiaug_ref[0] = jnp.where(is_k, 0.0, kk).astype(kiaug_ref.dtype)

    u_ref[...] = y_n[:, N_U0:N_U0 + POOL_WIDTH]
    g_ref[...] = y_n[:, N_G0:N_G0 + 2 * D_MODEL].astype(g_ref.dtype)


def _proj(x2, gmix, wt, wn, gq, gk, *, batch, seq, tm):
    n = batch * seq
    tiles_per_seq = seq // tm
    const = lambda i: (0, 0)
    bidx = lambda i: (i // tiles_per_seq, 0, i % tiles_per_seq)
    out_shape = (
        jax.ShapeDtypeStruct((batch, ATTN_WIDTH, seq), jnp.bfloat16),
        jax.ShapeDtypeStruct((batch, IDX_HEADS * IDX_DIM, seq), jnp.bfloat16),
        jax.ShapeDtypeStruct((batch, IDX_HEADS, seq), jnp.float32),
        jax.ShapeDtypeStruct((batch, seq // KEY_CHUNK, VT_ROWS, KEY_CHUNK), jnp.bfloat16),
        jax.ShapeDtypeStruct((batch, seq, AUG), jnp.bfloat16),
        jax.ShapeDtypeStruct((batch, seq, AUG), jnp.bfloat16),
        jax.ShapeDtypeStruct((n, POOL_WIDTH), jnp.float32),
        jax.ShapeDtypeStruct((n, 2 * D_MODEL), jnp.bfloat16),
    )
    out_specs = (
        pl.BlockSpec((1, ATTN_WIDTH, tm), bidx),
        pl.BlockSpec((1, IDX_HEADS * IDX_DIM, tm), bidx),
        pl.BlockSpec((1, IDX_HEADS, tm), bidx),
        pl.BlockSpec((1, tm // KEY_CHUNK, VT_ROWS, KEY_CHUNK),
                     lambda i: (i // tiles_per_seq, i % tiles_per_seq, 0, 0)),
        pl.BlockSpec((1, tm, AUG), lambda i: (i // tiles_per_seq, i % tiles_per_seq, 0)),
        pl.BlockSpec((1, tm, AUG), lambda i: (i // tiles_per_seq, i % tiles_per_seq, 0)),
        pl.BlockSpec((tm, POOL_WIDTH), lambda i: (i, 0)),
        pl.BlockSpec((tm, 2 * D_MODEL), lambda i: (i, 0)),
    )
    in_specs = [
        pl.BlockSpec((tm, D_MODEL), lambda i: (i, 0)),
        pl.BlockSpec((1, D_MODEL), const),
        pl.BlockSpec((T_ROWS, D_MODEL), const),
        pl.BlockSpec((D_MODEL, N_COLS), const),
        pl.BlockSpec((HEAD_DIM, 1), const),
        pl.BlockSpec((1, LANES), const),
    ]
    return pl.pallas_call(
        functools.partial(_proj_kernel, tm=tm, seq=seq),
        grid=(n // tm,),
        in_specs=in_specs,
        out_specs=out_specs,
        out_shape=out_shape,
        compiler_params=pltpu.CompilerParams(
            dimension_semantics=("arbitrary",), vmem_limit_bytes=VMEM_LIMIT),
        name="proj",
    )(x2, gmix, wt, wn, gq, gk)


def _sortable_key(score):
    b = pltpu.bitcast(score, jnp.int32)
    neg = b >> 31
    return (b ^ (neg & 0x7FFFFFFF)) - neg


def _attn_kernel(qt_ref, qit_ref, wit_ref, vt_ref, kaug_ref, kiaug_ref, qaug_ref,
                 at_ref, rhs_idx, rhs_att, keys, acc_t):
    j = pl.program_id(1)
    nch = j + 1
    nq = LANES
    wide = N_HEADS * nq

    zeros64 = jnp.zeros((HEAD_DIM, nq), rhs_idx.dtype)
    for hd in range(N_HEADS):
        cols = slice(hd * nq, (hd + 1) * nq)
        rows = slice(hd * HEAD_DIM, (hd + 1) * HEAD_DIM)
        rhs_idx[0:HEAD_DIM, cols] = zeros64
        rhs_idx[HEAD_DIM:AUG, cols] = qit_ref[0, rows, :]
        rhs_att[0:HEAD_DIM, cols] = qt_ref[0, rows, :]
    rhs_att[HEAD_DIM:AUG, :] = qaug_ref[...]

    w_all = wit_ref[0]
    t_pos = j * nq + lax.broadcasted_iota(jnp.int32, (KEY_CHUNK, nq), 1)
    s_loc = lax.broadcasted_iota(jnp.int32, (KEY_CHUNK, nq), 0)

    def score_body(c, carry):
        off = pl.multiple_of(c * KEY_CHUNK, KEY_CHUNK)
        sc = jnp.dot(kiaug_ref[0, pl.ds(off, KEY_CHUNK), :], rhs_idx[...],
                     preferred_element_type=jnp.float32)
        tot = jnp.zeros((KEY_CHUNK, nq), jnp.float32)
        for hd in range(IDX_HEADS):
            tot = tot + jnp.maximum(sc[:, hd * nq:(hd + 1) * nq], 0.0) * w_all[hd:hd + 1, :]
        key = _sortable_key(tot)
        key = jnp.where(off + s_loc <= t_pos, key, INT_MIN)
        keys[pl.ds(off, KEY_CHUNK), :] = key
        return carry

    lax.fori_loop(0, nch, score_body, 0)

    def count_ge(cand):
        def body(c, acc):
            off = pl.multiple_of(c * KEY_CHUNK, KEY_CHUNK)
            return acc + jnp.where(keys[pl.ds(off, KEY_CHUNK), :] >= cand, 1, 0)
        acc = lax.fori_loop(0, nch, body, jnp.zeros((KEY_CHUNK, nq), jnp.int32))
        return jnp.sum(acc, axis=0, keepdims=True)

    def bit_body(it, thr):
        cand = thr + jnp.left_shift(jnp.int32(1), 31 - it)
        return jnp.where(count_ge(cand) >= TOPK, cand, thr)

    thr = lax.fori_loop(0, 32, bit_body, jnp.full((1, nq), INT_MIN, jnp.int32))
    thr = jnp.maximum(thr, INT_MIN + 1)
    need = TOPK - count_ge(thr + 1)

    r_i = lax.broadcasted_iota(jnp.int32, (KEY_CHUNK, KEY_CHUNK), 0)
    c_i = lax.broadcasted_iota(jnp.int32, (KEY_CHUNK, KEY_CHUNK), 1)
    tri = jnp.where(c_i <= r_i, 1.0, 0.0).astype(jnp.bfloat16)

    acc_t[...] = jnp.zeros_like(acc_t)

    def attn_body(c, carry):
        m_old, tie_seen = carry
        off = pl.multiple_of(c * KEY_CHUNK, KEY_CHUNK)
        key = keys[pl.ds(off, KEY_CHUNK), :]
        eq = key == thr
        eq_f = jnp.where(eq, 1.0, 0.0)
        incl = jnp.dot(tri, eq_f.astype(jnp.bfloat16), preferred_element_type=jnp.float32)
        rank = incl - eq_f + tie_seen
        sel = (key > thr) | (eq & (rank < need.astype(jnp.float32)))
        bias = jnp.where(sel, 0.0, NEG_BIG)
        tie_seen = tie_seen + incl[KEY_CHUNK - 1:KEY_CHUNK, :]

        logits = jnp.dot(kaug_ref[0, pl.ds(off, KEY_CHUNK), :], rhs_att[...],
                         preferred_element_type=jnp.float32)
        lm = logits + jnp.concatenate([bias] * N_HEADS, axis=1)
        m_new = jnp.maximum(m_old, jnp.max(lm, axis=0, keepdims=True))
        alpha = jnp.exp(m_old - m_new)
        p = jnp.exp(lm - m_new).astype(jnp.bfloat16)
        pv = jnp.dot(vt_ref[0, c], p, preferred_element_type=jnp.float32)
        acc_t[...] = acc_t[...] * alpha + pv
        return m_new, tie_seen

    lax.fori_loop(0, nch, attn_body,
                  (jnp.full((1, wide), NEG_BIG, jnp.float32), jnp.zeros((1, nq), jnp.float32)))

    res = acc_t[0:HEAD_DIM, :] / acc_t[HEAD_DIM:HEAD_DIM + 1, :]
    for hd in range(N_HEADS):
        at_ref[0, hd * HEAD_DIM:(hd + 1) * HEAD_DIM, :] = (
            res[:, hd * nq:(hd + 1) * nq].astype(at_ref.dtype))


def _attn(qt, qit, wit, vt, kaug, kiaug, qaug, *, batch, seq):
    nq = LANES
    qblk = lambda b, j: (b, 0, j)
    whole = lambda b, j: (b, 0, 0)
    return pl.pallas_call(
        _attn_kernel,
        grid=(batch, seq // nq),
        in_specs=[
            pl.BlockSpec((1, ATTN_WIDTH, nq), qblk),
            pl.BlockSpec((1, IDX_HEADS * IDX_DIM, nq), qblk),
            pl.BlockSpec((1, IDX_HEADS, nq), qblk),
            pl.BlockSpec((1, seq // KEY_CHUNK, VT_ROWS, KEY_CHUNK), lambda b, j: (b, 0, 0, 0)),
            pl.BlockSpec((1, seq, AUG), whole),
            pl.BlockSpec((1, seq, AUG), whole),
            pl.BlockSpec((AUG - HEAD_DIM, N_HEADS * nq), lambda b, j: (0, 0)),
        ],
        out_specs=pl.BlockSpec((1, ATTN_WIDTH, nq), qblk),
        out_shape=jax.ShapeDtypeStruct((batch, ATTN_WIDTH, seq), jnp.bfloat16),
        scratch_shapes=[
            pltpu.VMEM((AUG, N_HEADS * nq), jnp.bfloat16),
            pltpu.VMEM((AUG, N_HEADS * nq), jnp.bfloat16),
            pltpu.VMEM((seq, nq), jnp.int32),
            pltpu.VMEM((VT_ROWS, N_HEADS * nq), jnp.float32),
        ],
        compiler_params=pltpu.CompilerParams(
            dimension_semantics=("arbitrary", "arbitrary"), vmem_limit_bytes=VMEM_LIMIT),
        name="attn",
    )(qt, qit, wit, vt, kaug, kiaug, qaug)


def _mix_kernel(x_ref, at_ref, u_ref, halo_ref, g_ref, wpg_ref, pscale_ref, wba_ref, wbp_ref,
                wout_ref, gffn_ref, wg_ref, wu_ref, wd_ref, o_ref, ubuf, *, tm, seq):
    i = pl.program_id(0)
    tiles_per_seq = seq // tm
    first = (i % tiles_per_seq) == 0
    halo = halo_ref[...]
    ubuf[0:HALO, :] = jnp.where(first, jnp.zeros_like(halo), halo)
    ubuf[HALO:HALO + tm, :] = u_ref[...]

    t_pos = (i * tm) % seq + lax.broadcasted_iota(jnp.int32, (tm, POOL_GROUP_WIDTH), 0)
    pooled = []
    for g, w in enumerate(POOL_WINDOWS):
        lanes = slice(g * POOL_GROUP_WIDTH, (g + 1) * POOL_GROUP_WIDTH)
        levels = w.bit_length() - 1
        cur = ubuf[:, lanes]
        step = 1
        for _ in range(levels):
            cur = cur + pltpu.roll(cur, step, axis=0)
            step *= 2
        wsum = cur[HALO:, :]
        cnt = jnp.minimum(t_pos + 1, w).astype(jnp.float32)
        d = (wsum / cnt - ubuf[HALO:HALO + tm, lanes]).astype(jnp.bfloat16)
        pooled.append(jnp.dot(d, wpg_ref[g], preferred_element_type=jnp.float32))
    p = (jnp.concatenate(pooled, axis=-1) * pscale_ref[...]).astype(jnp.bfloat16)

    a_proj = lax.dot_general(at_ref[0], wba_ref[...], (((0,), (0,)), ((), ())),
                             preferred_element_type=jnp.float32)
    p_proj = jnp.dot(p, wbp_ref[...], preferred_element_type=jnp.float32)
    gates = g_ref[...].astype(jnp.float32)
    merged = (jax.nn.sigmoid(gates[:, :D_MODEL]) * a_proj
              + jax.nn.sigmoid(gates[:, D_MODEL:]) * p_proj)
    x1 = x_ref[...] + jnp.dot(merged.astype(jnp.bfloat16), wout_ref[...],
                              preferred_element_type=jnp.float32)

    h2 = (_rms(x1) * gffn_ref[...]).astype(jnp.bfloat16)
    gate = jnp.dot(h2, wg_ref[...], preferred_element_type=jnp.float32)
    up = jnp.dot(h2, wu_ref[...], preferred_element_type=jnp.float32)
    act = (jax.nn.silu(gate) * up).astype(jnp.bfloat16)
    o_ref[...] = x1 + jnp.dot(act, wd_ref[...], preferred_element_type=jnp.float32)


def _mix(x2, at, u, gates, wpg, pscale, wba, wbp, wout, gffn, wg, wu, wd, *, batch, seq, tm):
    n = batch * seq
    tiles_per_seq = seq // tm
    row = lambda i: (i, 0)
    const2 = lambda i: (0, 0)
    resident = functools.partial(pl.BlockSpec, pipeline_mode=pl.Buffered(1))
    return pl.pallas_call(
        functools.partial(_mix_kernel, tm=tm, seq=seq),
        grid=(n // tm,),
        in_specs=[
            pl.BlockSpec((tm, D_MODEL), row),
            pl.BlockSpec((1, ATTN_WIDTH, tm), lambda i: (i // tiles_per_seq, 0, i % tiles_per_seq)),
            pl.BlockSpec((tm, POOL_WIDTH), row),
            pl.BlockSpec((HALO, POOL_WIDTH), lambda i: (jnp.maximum(i * (tm // HALO) - 1, 0), 0)),
            pl.BlockSpec((tm, 2 * D_MODEL), row),
            resident((len(POOL_WINDOWS), POOL_GROUP_WIDTH, POOL_GROUP_WIDTH), lambda i: (0, 0, 0)),
            resident((1, POOL_WIDTH), const2),
            resident((ATTN_WIDTH, D_MODEL), const2),
            resident((POOL_WIDTH, D_MODEL), const2),
            resident((D_MODEL, D_MODEL), const2),
            resident((1, D_MODEL), const2),
            resident((D_MODEL, D_FF), const2),
            resident((D_MODEL, D_FF), const2),
            resident((D_FF, D_MODEL), const2),
        ],
        out_specs=pl.BlockSpec((tm, D_MODEL), row),
        out_shape=jax.ShapeDtypeStruct((n, D_MODEL), jnp.float32),
        scratch_shapes=[pltpu.VMEM((tm + HALO, POOL_WIDTH), jnp.float32)],
        compiler_params=pltpu.CompilerParams(
            dimension_semantics=("arbitrary",), vmem_limit_bytes=VMEM_LIMIT),
        name="mix",
    )(x2, at, u, u, gates, wpg, pscale, wba, wbp, wout, gffn, wg, wu, wd)


def _alibi_rows():
    slopes = 2.0 ** (-8.0 * jnp.arange(1, N_HEADS + 1, dtype=jnp.float32) / N_HEADS)
    cols = jnp.repeat(slopes, LANES)[None, :]
    rows = jnp.arange(AUG - HEAD_DIM)[:, None]
    return jnp.where(rows < 2, cols, 0.0).astype(jnp.bfloat16)


def _layer(x2, norm_mix, w_in, q_norm, k_norm, w_pool_group, pool_scale, w_branch_attn,
           w_branch_pool, w_out, norm_ffn, w_ffn_gate, w_ffn_up, w_ffn_down, *, batch, seq):
    bf = jnp.bfloat16
    c = [0, 512, 576, 640, 1152, 1216, 1224, 1736, 3784]
    wq, wk, wv, wqi, wki, wwi, wu_in, wgates = (w_in[:, c[n]:c[n + 1]] for n in range(8))
    pad = jnp.zeros((D_MODEL, T_ROWS - T_WI0 - IDX_HEADS), w_in.dtype)
    wt = jnp.concatenate([wq, wv, wqi, wwi, pad], axis=1).T.astype(bf)
    wn = jnp.concatenate([wk, wki, wu_in, wgates], axis=1).astype(bf)
    gk = jnp.concatenate([k_norm, jnp.zeros((LANES - HEAD_DIM,), k_norm.dtype)])[None, :]

    qt, qit, wit, vt, kaug, kiaug, u, gates = _proj(
        x2, norm_mix[None, :], wt, wn, q_norm[:, None], gk, batch=batch, seq=seq, tm=512)
    at = _attn(qt, qit, wit, vt, kaug, kiaug, _alibi_rows(), batch=batch, seq=seq)
    return _mix(x2, at, u, gates, w_pool_group.astype(bf), pool_scale[None, :],
                w_branch_attn.astype(bf), w_branch_pool.astype(bf), w_out.astype(bf),
                norm_ffn[None, :], w_ffn_gate.astype(bf), w_ffn_up.astype(bf),
                w_ffn_down.astype(bf), batch=batch, seq=seq, tm=256)


def kernel(x, norm_mix, w_in, q_norm, k_norm, w_pool_group, pool_scale, w_branch_attn,
           w_branch_pool, w_out, norm_ffn, w_ffn_gate, w_ffn_up, w_ffn_down):
    batch, seq, d = x.shape
    x2 = x.reshape(batch * seq, d)
    for l in range(norm_mix.shape[0]):
        x2 = _layer(x2, norm_mix[l], w_in[l], q_norm[l], k_norm[l], w_pool_group[l],
                    pool_scale[l], w_branch_attn[l], w_branch_pool[l], w_out[l], norm_ffn[l],
                    w_ffn_gate[l], w_ffn_up[l], w_ffn_down[l], batch=batch, seq=seq)
    return x2.reshape(batch, seq, d)
```

```python
import functools

import jax
import jax.numpy as jnp
from jax import lax
from jax.experimental import pallas as pl
from jax.experimental.pallas import tpu as pltpu

D_MODEL = 1024
N_HEADS = 8
HEAD_DIM = 64
ATTN_WIDTH = N_HEADS * HEAD_DIM
IDX_HEADS = 8
IDX_DIM = 64
IDX_SCALE = (IDX_HEADS ** -0.5) * (IDX_DIM ** -0.5)
TOPK = 256
POOL_WINDOWS = (2, 4, 8, 16)
POOL_WIDTH = 512
POOL_GROUP_WIDTH = 128
D_FF = 2816
RMS_EPS = 1e-6
NEG_BIG = -1e30

LANES = 128
Q_BLOCK = 256
KEY_CHUNK = 256
AUG = 128
VT_ROWS = 80
HALO = 16
LOG2E = 1.4426950408889634
SLOPE_PARTS = 3
BISECT_STEPS = 20
VMEM_LIMIT = 56 * 1024 * 1024

T_Q0, T_V0, T_QI0, T_WI0, T_ROWS = 0, 512, 576, 1088, 1104
N_KK0, N_U0, N_G0, N_COLS = 0, 128, 640, 2688


def _rms(x, eps=RMS_EPS):
    return x * lax.rsqrt(jnp.mean(x * x, axis=-1, keepdims=True) + eps)


def _proj_kernel(x_ref, gmix_ref, wt_ref, wn_ref, gq_ref, gk_ref,
                 qt_ref, qit_ref, wit_ref, vt_ref, kaug_ref, kiaug_ref, u_ref, g_ref,
                 *, tm, seq):
    i = pl.program_id(0)
    x = x_ref[...]
    h = (_rms(x) * gmix_ref[...]).astype(jnp.bfloat16)
    y_t = lax.dot_general(wt_ref[...], h, (((1,), (1,)), ((), ())),
                          preferred_element_type=jnp.float32)
    y_n = jnp.dot(h, wn_ref[...], preferred_element_type=jnp.float32)

    gq = gq_ref[...]
    for hd in range(N_HEADS):
        blk = y_t[T_Q0 + hd * HEAD_DIM:T_Q0 + (hd + 1) * HEAD_DIM, :]
        ss = jnp.mean(blk * blk, axis=0, keepdims=True)
        qn = blk * lax.rsqrt(ss + RMS_EPS) * gq * (HEAD_DIM ** -0.5 * LOG2E)
        qt_ref[0, hd * HEAD_DIM:(hd + 1) * HEAD_DIM, :] = qn.astype(qt_ref.dtype)

    qit_ref[0] = y_t[T_QI0:T_QI0 + IDX_HEADS * IDX_DIM, :].astype(qit_ref.dtype)
    wit_ref[0] = y_t[T_WI0:T_WI0 + IDX_HEADS, :] * IDX_SCALE

    v_t = y_t[T_V0:T_V0 + HEAD_DIM, :]
    row = lax.broadcasted_iota(jnp.int32, (VT_ROWS - HEAD_DIM, tm), 0)
    tail = jnp.where(row == 0, 1.0, 0.0)
    vt_full = jnp.concatenate([v_t, tail], axis=0).astype(vt_ref.dtype)
    for cc in range(tm // KEY_CHUNK):
        vt_ref[0, cc] = vt_full[:, cc * KEY_CHUNK:(cc + 1) * KEY_CHUNK]

    kk = y_n[:, N_KK0:N_KK0 + LANES]
    lane = lax.broadcasted_iota(jnp.int32, (tm, LANES), 1)
    is_k = lane < HEAD_DIM
    ss = jnp.sum(jnp.where(is_k, kk * kk, 0.0), axis=-1, keepdims=True) * (1.0 / HEAD_DIM)
    kn = kk * lax.rsqrt(ss + RMS_EPS) * gk_ref[...]
    pos = (i * tm) % seq + lax.broadcasted_iota(jnp.int32, (tm, LANES), 0)
    pos_hi = ((pos >> 8) << 8).astype(jnp.float32)
    pos_lo = (pos & 255).astype(jnp.float32)
    feats = jnp.where(lane < HEAD_DIM + SLOPE_PARTS, pos_hi,
                      jnp.where(lane < HEAD_DIM + 2 * SLOPE_PARTS, pos_lo, 0.0))
    kaug_ref[0] = jnp.where(is_k, kn, feats).astype(kaug_ref.dtype)
    kiaug_ref[0] = jnp.where(is_k, 0.0, kk).astype(kiaug_ref.dtype)

    u_ref[...] = y_n[:, N_U0:N_U0 + POOL_WIDTH]
    g_ref[...] = y_n[:, N_G0:N_G0 + 2 * D_MODEL].astype(g_ref.dtype)


def _proj(x2, gmix, wt, wn, gq, gk, *, batch, seq, tm):
    n = batch * seq
    tiles_per_seq = seq // tm
    const = lambda i: (0, 0)
    bidx = lambda i: (i // tiles_per_seq, 0, i % tiles_per_seq)
    out_shape = (
        jax.ShapeDtypeStruct((batch, ATTN_WIDTH, seq), jnp.bfloat16),
        jax.ShapeDtypeStruct((batch, IDX_HEADS * IDX_DIM, seq), jnp.bfloat16),
        jax.ShapeDtypeStruct((batch, IDX_HEADS, seq), jnp.float32),
        jax.ShapeDtypeStruct((batch, seq // KEY_CHUNK, VT_ROWS, KEY_CHUNK), jnp.bfloat16),
        jax.ShapeDtypeStruct((batch, seq, AUG), jnp.bfloat16),
        jax.ShapeDtypeStruct((batch, seq, AUG), jnp.bfloat16),
        jax.ShapeDtypeStruct((n, POOL_WIDTH), jnp.float32),
        jax.ShapeDtypeStruct((n, 2 * D_MODEL), jnp.bfloat16),
    )
    out_specs = (
        pl.BlockSpec((1, ATTN_WIDTH, tm), bidx),
        pl.BlockSpec((1, IDX_HEADS * IDX_DIM, tm), bidx),
        pl.BlockSpec((1, IDX_HEADS, tm), bidx),
        pl.BlockSpec((1, tm // KEY_CHUNK, VT_ROWS, KEY_CHUNK),
                     lambda i: (i // tiles_per_seq, i % tiles_per_seq, 0, 0)),
        pl.BlockSpec((1, tm, AUG), lambda i: (i // tiles_per_seq, i % tiles_per_seq, 0)),
        pl.BlockSpec((1, tm, AUG), lambda i: (i // tiles_per_seq, i % tiles_per_seq, 0)),
        pl.BlockSpec((tm, POOL_WIDTH), lambda i: (i, 0)),
        pl.BlockSpec((tm, 2 * D_MODEL), lambda i: (i, 0)),
    )
    in_specs = [
        pl.BlockSpec((tm, D_MODEL), lambda i: (i, 0)),
        pl.BlockSpec((1, D_MODEL), const),
        pl.BlockSpec((T_ROWS, D_MODEL), const),
        pl.BlockSpec((D_MODEL, N_COLS), const),
        pl.BlockSpec((HEAD_DIM, 1), const),
        pl.BlockSpec((1, LANES), const),
    ]
    return pl.pallas_call(
        functools.partial(_proj_kernel, tm=tm, seq=seq),
        grid=(n // tm,),
        in_specs=in_specs,
        out_specs=out_specs,
        out_shape=out_shape,
        compiler_params=pltpu.CompilerParams(
            dimension_semantics=("arbitrary",), vmem_limit_bytes=VMEM_LIMIT),
        name="proj",
    )(x2, gmix, wt, wn, gq, gk)


def _attn_kernel(qt_ref, qit_ref, wit_ref, vt_ref, kaug_ref, kiaug_ref, qaug_ref,
                 at_ref, rhs_idx, rhs_att, sc_ref, acc_t):
    j = pl.program_id(1)
    nq = Q_BLOCK
    kc = KEY_CHUNK
    nch = j + 1
    wide = N_HEADS * nq
    inf = jnp.float32(jnp.inf)

    zeros64 = jnp.zeros((HEAD_DIM, nq), rhs_idx.dtype)
    for hd in range(N_HEADS):
        cols = slice(hd * nq, (hd + 1) * nq)
        rows = slice(hd * HEAD_DIM, (hd + 1) * HEAD_DIM)
        rhs_idx[0:HEAD_DIM, cols] = zeros64
        rhs_idx[HEAD_DIM:AUG, cols] = qit_ref[0, rows, :]
        rhs_att[0:HEAD_DIM, cols] = qt_ref[0, rows, :]
    rhs_att[HEAD_DIM:AUG, :] = qaug_ref[...]

    w_all = wit_ref[0]
    t_row = j * nq + lax.broadcasted_iota(jnp.int32, (1, nq), 1)
    t_pos = j * nq + lax.broadcasted_iota(jnp.int32, (kc, nq), 1)
    s_loc = lax.broadcasted_iota(jnp.int32, (kc, nq), 0)

    def pairwise(step, n, carry):
        carry = lax.fori_loop(0, n // 2, lambda i, cr: step(2 * i + 1, step(2 * i, cr)), carry)
        return lax.cond(n % 2 == 1, lambda cr: step(n - 1, cr), lambda cr: cr, carry)

    def fold(x, op):
        return op(x.reshape(kc // 8, 8, nq), axis=0)

    def score_step(c, lo):
        off = pl.multiple_of(c * kc, kc)
        sc = jnp.dot(kiaug_ref[0, pl.ds(off, kc), :], rhs_idx[...],
                     preferred_element_type=jnp.float32)
        tot = jnp.zeros((kc, nq), jnp.float32)
        for hd in range(IDX_HEADS):
            tot = tot + jnp.maximum(sc[:, hd * nq:(hd + 1) * nq], 0.0) * w_all[hd:hd + 1, :]
        causal = off + s_loc <= t_pos
        sc_ref[pl.ds(off, kc), :] = jnp.where(causal, tot, -inf)
        return jnp.minimum(lo, fold(jnp.where(causal, tot, inf), jnp.min))

    lo = pairwise(score_step, nch, jnp.full((8, nq), inf, jnp.float32))
    lo = jnp.min(lo, axis=0, keepdims=True)

    def count_ge(v):
        def body(c, acc):
            off = pl.multiple_of(c * kc, kc)
            return acc + fold(jnp.where(sc_ref[pl.ds(off, kc), :] >= v, 1.0, 0.0), jnp.sum)
        acc = lax.fori_loop(0, nch, body, jnp.zeros((8, nq), jnp.float32))
        return jnp.sum(acc, axis=0, keepdims=True)

    def max_below(v):
        def body(c, acc):
            off = pl.multiple_of(c * kc, kc)
            s = sc_ref[pl.ds(off, kc), :]
            return jnp.maximum(acc, fold(jnp.where(s < v, s, -inf), jnp.max))
        acc = lax.fori_loop(0, nch, body, jnp.full((8, nq), -inf, jnp.float32))
        return jnp.max(acc, axis=0, keepdims=True)

    def n_open(done):
        return jnp.sum(1 - done)

    def snap(state):
        lo, hi, thr, cge, done = state
        b = max_below(hi)
        c = count_ge(b)
        ok = c >= TOPK
        newly = ok & (done == 0)
        return (lo, jnp.where(ok, hi, b), jnp.where(newly, b, thr),
                jnp.where(newly, c, cge), jnp.where(ok, 1, done))

    def bisect(state):
        lo, hi, thr, cge, done = state
        mid = 0.5 * lo + 0.5 * hi
        ge = count_ge(mid) >= TOPK
        return (jnp.where(ge, mid, lo), jnp.where(ge, hi, mid), thr, cge, done)

    short = jnp.where((t_row + 1) <= TOPK, 1, 0)
    state = (lo, jnp.full((1, nq), inf, jnp.float32), lo, jnp.zeros((1, nq), jnp.float32), short)
    n0 = n_open(short)
    state = lax.cond(n0 > 0, snap, lambda s: s, state)
    state = lax.fori_loop(0, jnp.where(n0 > 0, BISECT_STEPS, 0), lambda _, s: bisect(s), state)

    def finish_body(carry):
        s = snap(carry[0])
        return bisect(s), n_open(s[4])

    state, _ = lax.while_loop(lambda carry: carry[1] > 0, finish_body, (state, n0))
    thr, cge = state[2], state[3]

    @pl.when(jnp.max(cge) > TOPK)
    def _():
        def gt_body(c, acc):
            off = pl.multiple_of(c * kc, kc)
            return acc + fold(jnp.where(sc_ref[pl.ds(off, kc), :] > thr, 1.0, 0.0), jnp.sum)
        cnt_gt = jnp.sum(lax.fori_loop(0, nch, gt_body, jnp.zeros((8, nq), jnp.float32)),
                         axis=0, keepdims=True)
        need = TOPK - cnt_gt
        r_i = lax.broadcasted_iota(jnp.int32, (kc, kc), 0)
        c_i = lax.broadcasted_iota(jnp.int32, (kc, kc), 1)
        tri = jnp.where(c_i <= r_i, 1.0, 0.0).astype(jnp.bfloat16)

        def tie_body(c, seen):
            off = pl.multiple_of(c * kc, kc)
            s = sc_ref[pl.ds(off, kc), :]
            eq = s == thr
            eq_f = jnp.where(eq, 1.0, 0.0)
            incl = jnp.dot(tri, eq_f.astype(jnp.bfloat16), preferred_element_type=jnp.float32)
            rank = incl - eq_f + seen
            sc_ref[pl.ds(off, kc), :] = jnp.where(eq & (rank >= need), -inf, s)
            return seen + incl[kc - 1:kc, :]

        lax.fori_loop(0, nch, tie_body, jnp.zeros((1, nq), jnp.float32))

    acc_t[...] = jnp.zeros_like(acc_t)

    def attn_step(c, m_old):
        off = pl.multiple_of(c * kc, kc)
        sel = sc_ref[pl.ds(off, kc), :] >= thr
        logits = jnp.dot(kaug_ref[0, pl.ds(off, kc), :], rhs_att[...],
                         preferred_element_type=jnp.float32)
        lm = jnp.concatenate(
            [jnp.where(sel, logits[:, hd * nq:(hd + 1) * nq], NEG_BIG) for hd in range(N_HEADS)],
            axis=1)
        m_new = jnp.maximum(m_old, jnp.max(lm, axis=0, keepdims=True))
        alpha = jnp.exp2(m_old - m_new)
        p = jnp.exp2(lm - m_new).astype(jnp.bfloat16)
        pv = jnp.dot(vt_ref[0, c], p, preferred_element_type=jnp.float32)
        acc_t[...] = acc_t[...] * alpha + pv
        return m_new

    pairwise(attn_step, nch, jnp.full((1, wide), NEG_BIG, jnp.float32))

    inv_l = 1.0 / acc_t[HEAD_DIM:HEAD_DIM + 1, :]
    for hd in range(N_HEADS):
        at_ref[0, hd * HEAD_DIM:(hd + 1) * HEAD_DIM, :] = (
            acc_t[0:HEAD_DIM, hd * nq:(hd + 1) * nq] * inv_l[:, hd * nq:(hd + 1) * nq]
        ).astype(at_ref.dtype)


def _attn(qt, qit, wit, vt, kaug, kiaug, qaug, *, batch, seq):
    nq = Q_BLOCK
    assert KEY_CHUNK == Q_BLOCK and seq % nq == 0
    qblk = lambda b, j: (b, 0, j)
    whole = lambda b, j: (b, 0, 0)
    return pl.pallas_call(
        _attn_kernel,
        grid=(batch, seq // nq),
        in_specs=[
            pl.BlockSpec((1, ATTN_WIDTH, nq), qblk),
            pl.BlockSpec((1, IDX_HEADS * IDX_DIM, nq), qblk),
            pl.BlockSpec((1, IDX_HEADS, nq), qblk),
            pl.BlockSpec((1, seq // KEY_CHUNK, VT_ROWS, KEY_CHUNK), lambda b, j: (b, 0, 0, 0)),
            pl.BlockSpec((1, seq, AUG), whole),
            pl.BlockSpec((1, seq, AUG), whole),
            pl.BlockSpec((AUG - HEAD_DIM, N_HEADS * nq), lambda b, j: (0, 0)),
        ],
        out_specs=pl.BlockSpec((1, ATTN_WIDTH, nq), qblk),
        out_shape=jax.ShapeDtypeStruct((batch, ATTN_WIDTH, seq), jnp.bfloat16),
        scratch_shapes=[
            pltpu.VMEM((AUG, N_HEADS * nq), jnp.bfloat16),
            pltpu.VMEM((AUG, N_HEADS * nq), jnp.bfloat16),
            pltpu.VMEM((seq, nq), jnp.float32),
            pltpu.VMEM((VT_ROWS, N_HEADS * nq), jnp.float32),
        ],
        compiler_params=pltpu.CompilerParams(
            dimension_semantics=("arbitrary", "arbitrary"), vmem_limit_bytes=VMEM_LIMIT),
        name="attn",
    )(qt, qit, wit, vt, kaug, kiaug, qaug)


def _mix_kernel(x_ref, at_ref, u_ref, halo_ref, g_ref, wpg_ref, pscale_ref, wba_ref, wbp_ref,
                wout_ref, gffn_ref, wg_ref, wu_ref, wd_ref, o_ref, ubuf, *, tm, seq):
    i = pl.program_id(0)
    tiles_per_seq = seq // tm
    first = (i % tiles_per_seq) == 0
    halo = halo_ref[...]
    ubuf[0:HALO, :] = jnp.where(first, jnp.zeros_like(halo), halo)
    ubuf[HALO:HALO + tm, :] = u_ref[...]

    t_pos = (i * tm) % seq + lax.broadcasted_iota(jnp.int32, (tm, POOL_GROUP_WIDTH), 0)
    pooled = []
    for g, w in enumerate(POOL_WINDOWS):
        lanes = slice(g * POOL_GROUP_WIDTH, (g + 1) * POOL_GROUP_WIDTH)
        levels = w.bit_length() - 1
        cur = ubuf[:, lanes]
        step = 1
        for _ in range(levels):
            cur = cur + pltpu.roll(cur, step, axis=0)
            step *= 2
        wsum = cur[HALO:, :]
        cnt = jnp.minimum(t_pos + 1, w).astype(jnp.float32)
        d = (wsum / cnt - ubuf[HALO:HALO + tm, lanes]).astype(jnp.bfloat16)
        pooled.append(jnp.dot(d, wpg_ref[g], preferred_element_type=jnp.float32))
    p = (jnp.concatenate(pooled, axis=-1) * pscale_ref[...]).astype(jnp.bfloat16)

    a_proj = lax.dot_general(at_ref[0], wba_ref[...], (((0,), (0,)), ((), ())),
                             preferred_element_type=jnp.float32)
    p_proj = jnp.dot(p, wbp_ref[...], preferred_element_type=jnp.float32)
    gates = g_ref[...].astype(jnp.float32)
    merged = (jax.nn.sigmoid(gates[:, :D_MODEL]) * a_proj
              + jax.nn.sigmoid(gates[:, D_MODEL:]) * p_proj)
    x1 = x_ref[...] + jnp.dot(merged.astype(jnp.bfloat16), wout_ref[...],
                              preferred_element_type=jnp.float32)

    h2 = (_rms(x1) * gffn_ref[...]).astype(jnp.bfloat16)
    gate = jnp.dot(h2, wg_ref[...], preferred_element_type=jnp.float32)
    up = jnp.dot(h2, wu_ref[...], preferred_element_type=jnp.float32)
    act = (jax.nn.silu(gate) * up).astype(jnp.bfloat16)
    o_ref[...] = x1 + jnp.dot(act, wd_ref[...], preferred_element_type=jnp.float32)


def _mix(x2, at, u, gates, wpg, pscale, wba, wbp, wout, gffn, wg, wu, wd, *, batch, seq, tm):
    n = batch * seq
    tiles_per_seq = seq // tm
    row = lambda i: (i, 0)
    const2 = lambda i: (0, 0)
    resident = functools.partial(pl.BlockSpec, pipeline_mode=pl.Buffered(1))
    return pl.pallas_call(
        functools.partial(_mix_kernel, tm=tm, seq=seq),
        grid=(n // tm,),
        in_specs=[
            pl.BlockSpec((tm, D_MODEL), row),
            pl.BlockSpec((1, ATTN_WIDTH, tm), lambda i: (i // tiles_per_seq, 0, i % tiles_per_seq)),
            pl.BlockSpec((tm, POOL_WIDTH), row),
            pl.BlockSpec((HALO, POOL_WIDTH), lambda i: (jnp.maximum(i * (tm // HALO) - 1, 0), 0)),
            pl.BlockSpec((tm, 2 * D_MODEL), row),
            resident((len(POOL_WINDOWS), POOL_GROUP_WIDTH, POOL_GROUP_WIDTH), lambda i: (0, 0, 0)),
            resident((1, POOL_WIDTH), const2),
            resident((ATTN_WIDTH, D_MODEL), const2),
            resident((POOL_WIDTH, D_MODEL), const2),
            resident((D_MODEL, D_MODEL), const2),
            resident((1, D_MODEL), const2),
            resident((D_MODEL, D_FF), const2),
            resident((D_MODEL, D_FF), const2),
            resident((D_FF, D_MODEL), const2),
        ],
        out_specs=pl.BlockSpec((tm, D_MODEL), row),
        out_shape=jax.ShapeDtypeStruct((n, D_MODEL), jnp.float32),
        scratch_shapes=[pltpu.VMEM((tm + HALO, POOL_WIDTH), jnp.float32)],
        compiler_params=pltpu.CompilerParams(
            dimension_semantics=("arbitrary",), vmem_limit_bytes=VMEM_LIMIT),
        name="mix",
    )(x2, at, u, u, gates, wpg, pscale, wba, wbp, wout, gffn, wg, wu, wd)


def _alibi_rows():
    slopes = 2.0 ** (-8.0 * jnp.arange(1, N_HEADS + 1, dtype=jnp.float32) / N_HEADS) * LOG2E
    parts, rest = [], slopes
    for _ in range(SLOPE_PARTS):
        term = rest.astype(jnp.bfloat16)
        parts.append(term)
        rest = rest - term.astype(jnp.float32)
    block = jnp.stack(parts * 2, axis=0)
    block = jnp.repeat(block, Q_BLOCK, axis=1)
    pad = jnp.zeros((AUG - HEAD_DIM - 2 * SLOPE_PARTS, N_HEADS * Q_BLOCK), jnp.bfloat16)
    return jnp.concatenate([block, pad], axis=0)


def _layer(x2, norm_mix, w_in, q_norm, k_norm, w_pool_group, pool_scale, w_branch_attn,
           w_branch_pool, w_out, norm_ffn, w_ffn_gate, w_ffn_up, w_ffn_down, *, batch, seq):
    bf = jnp.bfloat16
    c = [0, 512, 576, 640, 1152, 1216, 1224, 1736, 3784]
    wq, wk, wv, wqi, wki, wwi, wu_in, wgates = (w_in[:, c[n]:c[n + 1]] for n in range(8))
    pad = jnp.zeros((D_MODEL, T_ROWS - T_WI0 - IDX_HEADS), w_in.dtype)
    wt = jnp.concatenate([wq, wv, wqi, wwi, pad], axis=1).T.astype(bf)
    wn = jnp.concatenate([wk, wki, wu_in, wgates], axis=1).astype(bf)
    gk = jnp.concatenate([k_norm, jnp.zeros((LANES - HEAD_DIM,), k_norm.dtype)])[None, :]

    qt, qit, wit, vt, kaug, kiaug, u, gates = _proj(
        x2, norm_mix[None, :], wt, wn, q_norm[:, None], gk, batch=batch, seq=seq, tm=512)
    at = _attn(qt, qit, wit, vt, kaug, kiaug, _alibi_rows(), batch=batch, seq=seq)
    return _mix(x2, at, u, gates, w_pool_group.astype(bf), pool_scale[None, :],
                w_branch_attn.astype(bf), w_branch_pool.astype(bf), w_out.astype(bf),
                norm_ffn[None, :], w_ffn_gate.astype(bf), w_ffn_up.astype(bf),
                w_ffn_down.astype(bf), batch=batch, seq=seq, tm=256)


def kernel(x, norm_mix, w_in, q_norm, k_norm, w_pool_group, pool_scale, w_branch_attn,
           w_branch_pool, w_out, norm_ffn, w_ffn_gate, w_ffn_up, w_ffn_down):
    batch, seq, d = x.shape
    x2 = x.reshape(batch * seq, d)
    for l in range(norm_mix.shape[0]):
        x2 = _layer(x2, norm_mix[l], w_in[l], q_norm[l], k_norm[l], w_pool_group[l],
                    pool_scale[l], w_branch_attn[l], w_branch_pool[l], w_out[l], norm_ffn[l],
                    w_ffn_gate[l], w_ffn_up[l], w_ffn_down[l], batch=batch, seq=seq)
    return x2.reshape(batch, seq, d)
```

```python
import functools

import jax
import jax.numpy as jnp
from jax import lax
from jax.experimental import pallas as pl
from jax.experimental.pallas import tpu as pltpu

D_MODEL = 1024
N_HEADS = 8
HEAD_DIM = 64
ATTN_WIDTH = N_HEADS * HEAD_DIM
IDX_HEADS = 8
IDX_DIM = 64
IDX_SCALE = (IDX_HEADS ** -0.5) * (IDX_DIM ** -0.5)
TOPK = 256
POOL_WINDOWS = (2, 4, 8, 16)
POOL_WIDTH = 512
POOL_GROUP_WIDTH = 128
D_FF = 2816
RMS_EPS = 1e-6
NEG_BIG = -1e30

LANES = 128
Q_BLOCK = 256
KEY_CHUNK = 256
AUG = 128
VT_ROWS = 80
HALO = 16
LOG2E = 1.4426950408889634
SLOPE_PARTS = 3
PACK_ROWS = 16
BISECT16_STEPS = 11
BISECT_STEPS = 10
TINY = 1e-30
VMEM_LIMIT = 56 * 1024 * 1024

T_Q0, T_V0, T_QI0, T_WI0, T_ROWS = 0, 512, 576, 1088, 1104
N_KK0, N_U0, N_G0, N_COLS = 0, 128, 640, 2688


def _rms(x, eps=RMS_EPS):
    return x * lax.rsqrt(jnp.mean(x * x, axis=-1, keepdims=True) + eps)


def _proj_kernel(x_ref, gmix_ref, wt_ref, wn_ref, gq_ref, gk_ref,
                 qt_ref, qit_ref, wit_ref, vt_ref, kaug_ref, kiaug_ref, u_ref, g_ref,
                 *, tm, seq):
    i = pl.program_id(0)
    x = x_ref[...]
    h = (_rms(x) * gmix_ref[...]).astype(jnp.bfloat16)
    y_t = lax.dot_general(wt_ref[...], h, (((1,), (1,)), ((), ())),
                          preferred_element_type=jnp.float32)
    y_n = jnp.dot(h, wn_ref[...], preferred_element_type=jnp.float32)

    gq = gq_ref[...]
    for hd in range(N_HEADS):
        blk = y_t[T_Q0 + hd * HEAD_DIM:T_Q0 + (hd + 1) * HEAD_DIM, :]
        ss = jnp.mean(blk * blk, axis=0, keepdims=True)
        qn = blk * lax.rsqrt(ss + RMS_EPS) * gq * (HEAD_DIM ** -0.5 * LOG2E)
        qt_ref[0, hd * HEAD_DIM:(hd + 1) * HEAD_DIM, :] = qn.astype(qt_ref.dtype)

    qit_ref[0] = y_t[T_QI0:T_QI0 + IDX_HEADS * IDX_DIM, :].astype(qit_ref.dtype)
    wit_ref[0] = y_t[T_WI0:T_WI0 + IDX_HEADS, :] * IDX_SCALE

    v_t = y_t[T_V0:T_V0 + HEAD_DIM, :]
    row = lax.broadcasted_iota(jnp.int32, (VT_ROWS - HEAD_DIM, tm), 0)
    tail = jnp.where(row == 0, 1.0, 0.0)
    vt_full = jnp.concatenate([v_t, tail], axis=0).astype(vt_ref.dtype)
    for cc in range(tm // KEY_CHUNK):
        vt_ref[0, cc] = vt_full[:, cc * KEY_CHUNK:(cc + 1) * KEY_CHUNK]

    kk = y_n[:, N_KK0:N_KK0 + LANES]
    lane = lax.broadcasted_iota(jnp.int32, (tm, LANES), 1)
    is_k = lane < HEAD_DIM
    ss = jnp.sum(jnp.where(is_k, kk * kk, 0.0), axis=-1, keepdims=True) * (1.0 / HEAD_DIM)
    kn = kk * lax.rsqrt(ss + RMS_EPS) * gk_ref[...]
    pos = (i * tm) % seq + lax.broadcasted_iota(jnp.int32, (tm, LANES), 0)
    pos_hi = ((pos >> 8) << 8).astype(jnp.float32)
    pos_lo = (pos & 255).astype(jnp.float32)
    feats = jnp.where(lane < HEAD_DIM + SLOPE_PARTS, pos_hi,
                      jnp.where(lane < HEAD_DIM + 2 * SLOPE_PARTS, pos_lo, 0.0))
    kaug_ref[0] = jnp.where(is_k, kn, feats).astype(kaug_ref.dtype)
    kiaug_ref[0] = jnp.where(is_k, 0.0, kk).astype(kiaug_ref.dtype)

    u_ref[...] = y_n[:, N_U0:N_U0 + POOL_WIDTH]
    g_ref[...] = y_n[:, N_G0:N_G0 + 2 * D_MODEL].astype(g_ref.dtype)


def _proj(x2, gmix, wt, wn, gq, gk, *, batch, seq, tm):
    n = batch * seq
    tiles_per_seq = seq // tm
    const = lambda i: (0, 0)
    bidx = lambda i: (i // tiles_per_seq, 0, i % tiles_per_seq)
    out_shape = (
        jax.ShapeDtypeStruct((batch, ATTN_WIDTH, seq), jnp.bfloat16),
        jax.ShapeDtypeStruct((batch, IDX_HEADS * IDX_DIM, seq), jnp.bfloat16),
        jax.ShapeDtypeStruct((batch, IDX_HEADS, seq), jnp.float32),
        jax.ShapeDtypeStruct((batch, seq // KEY_CHUNK, VT_ROWS, KEY_CHUNK), jnp.bfloat16),
        jax.ShapeDtypeStruct((batch, seq, AUG), jnp.bfloat16),
        jax.ShapeDtypeStruct((batch, seq, AUG), jnp.bfloat16),
        jax.ShapeDtypeStruct((n, POOL_WIDTH), jnp.float32),
        jax.ShapeDtypeStruct((n, 2 * D_MODEL), jnp.bfloat16),
    )
    out_specs = (
        pl.BlockSpec((1, ATTN_WIDTH, tm), bidx),
        pl.BlockSpec((1, IDX_HEADS * IDX_DIM, tm), bidx),
        pl.BlockSpec((1, IDX_HEADS, tm), bidx),
        pl.BlockSpec((1, tm // KEY_CHUNK, VT_ROWS, KEY_CHUNK),
                     lambda i: (i // tiles_per_seq, i % tiles_per_seq, 0, 0)),
        pl.BlockSpec((1, tm, AUG), lambda i: (i // tiles_per_seq, i % tiles_per_seq, 0)),
        pl.BlockSpec((1, tm, AUG), lambda i: (i // tiles_per_seq, i % tiles_per_seq, 0)),
        pl.BlockSpec((tm, POOL_WIDTH), lambda i: (i, 0)),
        pl.BlockSpec((tm, 2 * D_MODEL), lambda i: (i, 0)),
    )
    in_specs = [
        pl.BlockSpec((tm, D_MODEL), lambda i: (i, 0)),
        pl.BlockSpec((1, D_MODEL), const),
        pl.BlockSpec((T_ROWS, D_MODEL), const),
        pl.BlockSpec((D_MODEL, N_COLS), const),
        pl.BlockSpec((HEAD_DIM, 1), const),
        pl.BlockSpec((1, LANES), const),
    ]
    return pl.pallas_call(
        functools.partial(_proj_kernel, tm=tm, seq=seq),
        grid=(n // tm,),
        in_specs=in_specs,
        out_specs=out_specs,
        out_shape=out_shape,
        compiler_params=pltpu.CompilerParams(
            dimension_semantics=("arbitrary",), vmem_limit_bytes=VMEM_LIMIT),
        name="proj",
    )(x2, gmix, wt, wn, gq, gk)


def _attn_kernel(qt_ref, qit_ref, wit_ref, vt_ref, kaug_ref, kiaug_ref, qaug_ref,
                 at_ref, rhs_idx, rhs_att, sc_ref, sc16_ref, acc_t, buf_a, buf_b):
    j = pl.program_id(1)
    nq = Q_BLOCK
    kc = KEY_CHUNK
    nch = j + 1
    wide = N_HEADS * nq
    inf = jnp.float32(jnp.inf)

    zeros64 = jnp.zeros((HEAD_DIM, nq), rhs_idx.dtype)
    for hd in range(N_HEADS):
        cols = slice(hd * nq, (hd + 1) * nq)
        rows = slice(hd * HEAD_DIM, (hd + 1) * HEAD_DIM)
        rhs_idx[0:HEAD_DIM, cols] = zeros64
        rhs_idx[HEAD_DIM:AUG, cols] = qit_ref[0, rows, :]
        rhs_att[0:HEAD_DIM, cols] = qt_ref[0, rows, :]
    rhs_att[HEAD_DIM:AUG, :] = qaug_ref[...]

    w_all = wit_ref[0]
    t_row = j * nq + lax.broadcasted_iota(jnp.int32, (1, nq), 1)
    t_pos = j * nq + lax.broadcasted_iota(jnp.int32, (kc, nq), 1)
    s_loc = lax.broadcasted_iota(jnp.int32, (kc, nq), 0)

    def pairwise(step, n, carry):
        carry = lax.fori_loop(0, n // 2, lambda i, cr: step(2 * i + 1, step(2 * i, cr)), carry)
        return lax.cond(n % 2 == 1, lambda cr: step(n - 1, cr), lambda cr: cr, carry)

    def fold(x, op, rows=8):
        parts = [x[r * rows:(r + 1) * rows, :] for r in range(kc // rows)]
        while len(parts) > 1:
            parts = [op(parts[i], parts[i + 1]) for i in range(0, len(parts), 2)]
        return parts[0]

    bufs = (buf_a, buf_b)

    def pipelined(produce, consume, carry):
        def both(c, par, carry, aux):
            nxt = produce(c + 1, bufs[1 - par])
            return consume(c, bufs[par], carry, aux), nxt

        def pair_body(i, st):
            return both(2 * i + 1, 1, *both(2 * i, 0, *st))

        st = lax.fori_loop(0, (nch - 1) // 2, pair_body, (carry, produce(0, bufs[0])))

        def even_tail(st):
            return consume(nch - 1, bufs[1], *both(nch - 2, 0, *st))

        return lax.cond(nch % 2 == 0, even_tail, lambda st: consume(nch - 1, bufs[0], *st), st)

    def score_mm(c, buf):
        off = pl.multiple_of(c * kc, kc)
        buf[...] = jnp.dot(kiaug_ref[0, pl.ds(off, kc), :], rhs_idx[...],
                           preferred_element_type=jnp.float32)
        return jnp.int32(0)

    def score_reduce(c, buf, lo_hi, _):
        off = pl.multiple_of(c * kc, kc)
        tot = jnp.zeros((kc, nq), jnp.float32)
        for hd in range(IDX_HEADS):
            tot = tot + jnp.maximum(buf[:, hd * nq:(hd + 1) * nq], 0.0) * w_all[hd:hd + 1, :]
        causal = off + s_loc <= t_pos
        masked = jnp.where(causal, tot, -inf)
        sc_ref[pl.ds(off, kc), :] = masked
        sc16_ref[pl.ds(off, kc), :] = masked.astype(jnp.bfloat16)
        lo, hi = lo_hi
        return (jnp.minimum(lo, fold(jnp.where(causal, tot, inf), jnp.minimum)),
                jnp.maximum(hi, fold(masked, jnp.maximum)))

    lo, hi = pipelined(score_mm, score_reduce, (jnp.full((8, nq), inf, jnp.float32),
                                                jnp.full((8, nq), -inf, jnp.float32)))
    lo = jnp.min(lo, axis=0, keepdims=True)
    hi = jnp.max(hi, axis=0, keepdims=True)

    def count_ge(v):
        def step(c, acc):
            off = pl.multiple_of(c * kc, kc)
            return acc + fold(jnp.where(sc_ref[pl.ds(off, kc), :] >= v, 1.0, 0.0), jnp.add)
        acc = pairwise(step, nch, jnp.zeros((8, nq), jnp.float32))
        return jnp.sum(acc, axis=0, keepdims=True)

    def count_ge16(v):
        v16 = v.astype(jnp.bfloat16)
        one, zero = jnp.ones((), jnp.bfloat16), jnp.zeros((), jnp.bfloat16)
        def step(c, acc):
            off = pl.multiple_of(c * kc, kc)
            hit = jnp.where(sc16_ref[pl.ds(off, kc), :] >= v16, one, zero)
            return acc + fold(hit, jnp.add, PACK_ROWS)
        acc = pairwise(step, nch, jnp.zeros((PACK_ROWS, nq), jnp.bfloat16))
        return jnp.sum(acc.astype(jnp.float32), axis=0, keepdims=True)

    def max_below(v):
        def step(c, acc):
            off = pl.multiple_of(c * kc, kc)
            s = sc_ref[pl.ds(off, kc), :]
            return jnp.maximum(acc, fold(jnp.where(s < v, s, -inf), jnp.maximum))
        acc = pairwise(step, nch, jnp.full((8, nq), -inf, jnp.float32))
        return jnp.max(acc, axis=0, keepdims=True)

    def n_open(done):
        return jnp.sum(1 - done)

    def snap(state):
        lo, hi, thr, cge, done = state
        b = max_below(hi)
        c = count_ge(b)
        ok = c >= TOPK
        newly = ok & (done == 0)
        return (lo, jnp.where(ok, hi, b), jnp.where(newly, b, thr),
                jnp.where(newly, c, cge), jnp.where(ok, 1, done))

    def bisect(state):
        lo, hi, thr, cge, done = state
        mid = 0.5 * lo + 0.5 * hi
        ge = count_ge(mid) >= TOPK
        return (jnp.where(ge, mid, lo), jnp.where(ge, hi, mid), thr, cge, done)

    def bisect16(_, lo_hi):
        lo, hi = lo_hi
        mid = (0.5 * lo + 0.5 * hi).astype(jnp.bfloat16).astype(jnp.float32)
        ge = count_ge16(mid) >= TOPK
        return jnp.where(ge, mid, lo), jnp.where(ge, hi, mid)

    short = jnp.where((t_row + 1) <= TOPK, 1, 0)
    n0 = n_open(short)
    run = n0 > 0
    lo16 = lo.astype(jnp.bfloat16).astype(jnp.float32)
    hi16 = hi.astype(jnp.bfloat16).astype(jnp.float32)
    hi16 = (hi16 + jnp.abs(hi16) * 2.0 ** -6 + TINY).astype(jnp.bfloat16).astype(jnp.float32)
    lo16, hi16 = lax.fori_loop(0, jnp.where(run, BISECT16_STEPS, 0), bisect16, (lo16, hi16))
    lo32 = jnp.maximum(lo, lo16 - jnp.abs(lo16) * 2.0 ** -7 - TINY)
    state = (lo32, hi16, lo, jnp.zeros((1, nq), jnp.float32), short)
    state = lax.fori_loop(0, jnp.where(run, BISECT_STEPS, 0), lambda _, s: bisect(s), state)
    state = lax.cond(run, snap, lambda s: s, state)

    def finish_body(carry):
        s = snap(bisect(carry[0]))
        return s, n_open(s[4])

    state, _ = lax.while_loop(lambda carry: carry[1] > 0, finish_body, (state, n_open(state[4])))
    thr, cge = state[2], state[3]

    @pl.when(jnp.max(cge) > TOPK)
    def _():
        def gt_body(c, acc):
            off = pl.multiple_of(c * kc, kc)
            return acc + fold(jnp.where(sc_ref[pl.ds(off, kc), :] > thr, 1.0, 0.0), jnp.add)
        cnt_gt = jnp.sum(lax.fori_loop(0, nch, gt_body, jnp.zeros((8, nq), jnp.float32)),
                         axis=0, keepdims=True)
        need = TOPK - cnt_gt
        r_i = lax.broadcasted_iota(jnp.int32, (kc, kc), 0)
        c_i = lax.broadcasted_iota(jnp.int32, (kc, kc), 1)
        tri = jnp.where(c_i <= r_i, 1.0, 0.0).astype(jnp.bfloat16)

        def tie_body(c, seen):
            off = pl.multiple_of(c * kc, kc)
            s = sc_ref[pl.ds(off, kc), :]
            eq = s == thr
            eq_f = jnp.where(eq, 1.0, 0.0)
            incl = jnp.dot(tri, eq_f.astype(jnp.bfloat16), preferred_element_type=jnp.float32)
            rank = incl - eq_f + seen
            sc_ref[pl.ds(off, kc), :] = jnp.where(eq & (rank >= need), -inf, s)
            return seen + incl[kc - 1:kc, :]

        lax.fori_loop(0, nch, tie_body, jnp.zeros((1, nq), jnp.float32))

    acc_t[...] = jnp.zeros_like(acc_t)

    def masked_logits(c, buf):
        off = pl.multiple_of(c * kc, kc)
        bias = jnp.where(sc_ref[pl.ds(off, kc), :] >= thr, 0.0, NEG_BIG)
        logits = jnp.dot(kaug_ref[0, pl.ds(off, kc), :], rhs_att[...],
                         preferred_element_type=jnp.float32)
        cms = []
        for hd in range(N_HEADS):
            lm = logits[:, hd * nq:(hd + 1) * nq] + bias
            buf[:, hd * nq:(hd + 1) * nq] = lm
            cms.append(jnp.max(fold(lm, jnp.maximum), axis=0, keepdims=True))
        return jnp.concatenate(cms, axis=1)

    def softmax_pv(c, buf, m_old, cmax):
        m_new = jnp.maximum(m_old, cmax)
        alpha = jnp.exp2(m_old - m_new)
        p = jnp.exp2((buf[...] - m_new).astype(jnp.bfloat16))
        pv = jnp.dot(vt_ref[0, c], p, preferred_element_type=jnp.float32)
        acc_t[...] = acc_t[...] * alpha + pv
        return m_new

    pipelined(masked_logits, softmax_pv, jnp.full((1, wide), NEG_BIG, jnp.float32))

    inv_l = 1.0 / acc_t[HEAD_DIM:HEAD_DIM + 1, :]
    for hd in range(N_HEADS):
        at_ref[0, hd * HEAD_DIM:(hd + 1) * HEAD_DIM, :] = (
            acc_t[0:HEAD_DIM, hd * nq:(hd + 1) * nq] * inv_l[:, hd * nq:(hd + 1) * nq]
        ).astype(at_ref.dtype)


def _attn(qt, qit, wit, vt, kaug, kiaug, qaug, *, batch, seq):
    nq = Q_BLOCK
    assert KEY_CHUNK == Q_BLOCK and seq % nq == 0
    qblk = lambda b, j: (b, 0, j)
    whole = lambda b, j: (b, 0, 0)
    return pl.pallas_call(
        _attn_kernel,
        grid=(batch, seq // nq),
        in_specs=[
            pl.BlockSpec((1, ATTN_WIDTH, nq), qblk),
            pl.BlockSpec((1, IDX_HEADS * IDX_DIM, nq), qblk),
            pl.BlockSpec((1, IDX_HEADS, nq), qblk),
            pl.BlockSpec((1, seq // KEY_CHUNK, VT_ROWS, KEY_CHUNK), lambda b, j: (b, 0, 0, 0)),
            pl.BlockSpec((1, seq, AUG), whole),
            pl.BlockSpec((1, seq, AUG), whole),
            pl.BlockSpec((AUG - HEAD_DIM, N_HEADS * nq), lambda b, j: (0, 0)),
        ],
        out_specs=pl.BlockSpec((1, ATTN_WIDTH, nq), qblk),
        out_shape=jax.ShapeDtypeStruct((batch, ATTN_WIDTH, seq), jnp.bfloat16),
        scratch_shapes=[
            pltpu.VMEM((AUG, N_HEADS * nq), jnp.bfloat16),
            pltpu.VMEM((AUG, N_HEADS * nq), jnp.bfloat16),
            pltpu.VMEM((seq, nq), jnp.float32),
            pltpu.VMEM((seq, nq), jnp.bfloat16),
            pltpu.VMEM((VT_ROWS, N_HEADS * nq), jnp.float32),
            pltpu.VMEM((KEY_CHUNK, N_HEADS * nq), jnp.float32),
            pltpu.VMEM((KEY_CHUNK, N_HEADS * nq), jnp.float32),
        ],
        compiler_params=pltpu.CompilerParams(
            dimension_semantics=("arbitrary", "arbitrary"), vmem_limit_bytes=VMEM_LIMIT),
        name="attn",
    )(qt, qit, wit, vt, kaug, kiaug, qaug)


def _mix_kernel(x_ref, at_ref, u_ref, halo_ref, g_ref, wpg_ref, pscale_ref, wba_ref, wbp_ref,
                wout_ref, gffn_ref, wg_ref, wu_ref, wd_ref, o_ref, ubuf, *, tm, seq):
    i = pl.program_id(0)
    tiles_per_seq = seq // tm
    first = (i % tiles_per_seq) == 0
    halo = halo_ref[...]
    ubuf[0:HALO, :] = jnp.where(first, jnp.zeros_like(halo), halo)
    ubuf[HALO:HALO + tm, :] = u_ref[...]

    t_pos = (i * tm) % seq + lax.broadcasted_iota(jnp.int32, (tm, POOL_GROUP_WIDTH), 0)
    pooled = []
    for g, w in enumerate(POOL_WINDOWS):
        lanes = slice(g * POOL_GROUP_WIDTH, (g + 1) * POOL_GROUP_WIDTH)
        levels = w.bit_length() - 1
        cur = ubuf[:, lanes]
        step = 1
        for _ in range(levels):
            cur = cur + pltpu.roll(cur, step, axis=0)
            step *= 2
        wsum = cur[HALO:, :]
        cnt = jnp.minimum(t_pos + 1, w).astype(jnp.float32)
        d = (wsum / cnt - ubuf[HALO:HALO + tm, lanes]).astype(jnp.bfloat16)
        pooled.append(jnp.dot(d, wpg_ref[g], preferred_element_type=jnp.float32))
    p = (jnp.concatenate(pooled, axis=-1) * pscale_ref[...]).astype(jnp.bfloat16)

    a_proj = lax.dot_general(at_ref[0], wba_ref[...], (((0,), (0,)), ((), ())),
                             preferred_element_type=jnp.float32)
    p_proj = jnp.dot(p, wbp_ref[...], preferred_element_type=jnp.float32)
    gates = g_ref[...].astype(jnp.float32)
    merged = (jax.nn.sigmoid(gates[:, :D_MODEL]) * a_proj
              + jax.nn.sigmoid(gates[:, D_MODEL:]) * p_proj)
    x1 = x_ref[...] + jnp.dot(merged.astype(jnp.bfloat16), wout_ref[...],
                              preferred_element_type=jnp.float32)

    h2 = (_rms(x1) * gffn_ref[...]).astype(jnp.bfloat16)
    gate = jnp.dot(h2, wg_ref[...], preferred_element_type=jnp.float32)
    up = jnp.dot(h2, wu_ref[...], preferred_element_type=jnp.float32)
    act = (jax.nn.silu(gate) * up).astype(jnp.bfloat16)
    o_ref[...] = x1 + jnp.dot(act, wd_ref[...], preferred_element_type=jnp.float32)


def _mix(x2, at, u, gates, wpg, pscale, wba, wbp, wout, gffn, wg, wu, wd, *, batch, seq, tm):
    n = batch * seq
    tiles_per_seq = seq // tm
    row = lambda i: (i, 0)
    const2 = lambda i: (0, 0)
    resident = functools.partial(pl.BlockSpec, pipeline_mode=pl.Buffered(1))
    return pl.pallas_call(
        functools.partial(_mix_kernel, tm=tm, seq=seq),
        grid=(n // tm,),
        in_specs=[
            pl.BlockSpec((tm, D_MODEL), row),
            pl.BlockSpec((1, ATTN_WIDTH, tm), lambda i: (i // tiles_per_seq, 0, i % tiles_per_seq)),
            pl.BlockSpec((tm, POOL_WIDTH), row),
            pl.BlockSpec((HALO, POOL_WIDTH), lambda i: (jnp.maximum(i * (tm // HALO) - 1, 0), 0)),
            pl.BlockSpec((tm, 2 * D_MODEL), row),
            resident((len(POOL_WINDOWS), POOL_GROUP_WIDTH, POOL_GROUP_WIDTH), lambda i: (0, 0, 0)),
            resident((1, POOL_WIDTH), const2),
            resident((ATTN_WIDTH, D_MODEL), const2),
            resident((POOL_WIDTH, D_MODEL), const2),
            resident((D_MODEL, D_MODEL), const2),
            resident((1, D_MODEL), const2),
            resident((D_MODEL, D_FF), const2),
            resident((D_MODEL, D_FF), const2),
            resident((D_FF, D_MODEL), const2),
        ],
        out_specs=pl.BlockSpec((tm, D_MODEL), row),
        out_shape=jax.ShapeDtypeStruct((n, D_MODEL), jnp.float32),
        scratch_shapes=[pltpu.VMEM((tm + HALO, POOL_WIDTH), jnp.float32)],
        compiler_params=pltpu.CompilerParams(
            dimension_semantics=("arbitrary",), vmem_limit_bytes=VMEM_LIMIT),
        name="mix",
    )(x2, at, u, u, gates, wpg, pscale, wba, wbp, wout, gffn, wg, wu, wd)


def _alibi_rows():
    slopes = 2.0 ** (-8.0 * jnp.arange(1, N_HEADS + 1, dtype=jnp.float32) / N_HEADS) * LOG2E
    parts, rest = [], slopes
    for _ in range(SLOPE_PARTS):
        term = rest.astype(jnp.bfloat16)
        parts.append(term)
        rest = rest - term.astype(jnp.float32)
    block = jnp.stack(parts * 2, axis=0)
    block = jnp.repeat(block, Q_BLOCK, axis=1)
    pad = jnp.zeros((AUG - HEAD_DIM - 2 * SLOPE_PARTS, N_HEADS * Q_BLOCK), jnp.bfloat16)
    return jnp.concatenate([block, pad], axis=0)


def _layer(x2, norm_mix, w_in, q_norm, k_norm, w_pool_group, pool_scale, w_branch_attn,
           w_branch_pool, w_out, norm_ffn, w_ffn_gate, w_ffn_up, w_ffn_down, *, batch, seq):
    bf = jnp.bfloat16
    c = [0, 512, 576, 640, 1152, 1216, 1224, 1736, 3784]
    wq, wk, wv, wqi, wki, wwi, wu_in, wgates = (w_in[:, c[n]:c[n + 1]] for n in range(8))
    pad = jnp.zeros((D_MODEL, T_ROWS - T_WI0 - IDX_HEADS), w_in.dtype)
    wt = jnp.concatenate([wq, wv, wqi, wwi, pad], axis=1).T.astype(bf)
    wn = jnp.concatenate([wk, wki, wu_in, wgates], axis=1).astype(bf)
    gk = jnp.concatenate([k_norm, jnp.zeros((LANES - HEAD_DIM,), k_norm.dtype)])[None, :]

    qt, qit, wit, vt, kaug, kiaug, u, gates = _proj(
        x2, norm_mix[None, :], wt, wn, q_norm[:, None], gk, batch=batch, seq=seq, tm=512)
    at = _attn(qt, qit, wit, vt, kaug, kiaug, _alibi_rows(), batch=batch, seq=seq)
    return _mix(x2, at, u, gates, w_pool_group.astype(bf), pool_scale[None, :],
                w_branch_attn.astype(bf), w_branch_pool.astype(bf), w_out.astype(bf),
                norm_ffn[None, :], w_ffn_gate.astype(bf), w_ffn_up.astype(bf),
                w_ffn_down.astype(bf), batch=batch, seq=seq, tm=256)


def kernel(x, norm_mix, w_in, q_norm, k_norm, w_pool_group, pool_scale, w_branch_attn,
           w_branch_pool, w_out, norm_ffn, w_ffn_gate, w_ffn_up, w_ffn_down):
    batch, seq, d = x.shape
    x2 = x.reshape(batch * seq, d)
    for l in range(norm_mix.shape[0]):
        x2 = _layer(x2, norm_mix[l], w_in[l], q_norm[l], k_norm[l], w_pool_group[l],
                    pool_scale[l], w_branch_attn[l], w_branch_pool[l], w_out[l], norm_ffn[l],
                    w_ffn_gate[l], w_ffn_up[l], w_ffn_down[l], batch=batch, seq=seq)
    return x2.reshape(batch, seq, d)
```

```python
import functools

import jax
import jax.numpy as jnp
from jax import lax
from jax.experimental import pallas as pl
from jax.experimental.pallas import tpu as pltpu

D_MODEL = 1024
N_HEADS = 8
HEAD_DIM = 64
ATTN_WIDTH = N_HEADS * HEAD_DIM
IDX_HEADS = 8
IDX_DIM = 64
IDX_SCALE = (IDX_HEADS ** -0.5) * (IDX_DIM ** -0.5)
TOPK = 256
POOL_WINDOWS = (2, 4, 8, 16)
POOL_WIDTH = 512
POOL_GROUP_WIDTH = 128
D_FF = 2816
RMS_EPS = 1e-6
NEG_BIG = -1e30

LANES = 128
Q_BLOCK = 256
KEY_CHUNK = 256
AUG = 128
VT_ROWS = 80
HALO = 16
LOG2E = 1.4426950408889634
SLOPE_PARTS = 3
PACK_ROWS = 16
BISECT16_STEPS = 11
BISECT_STEPS = 12
TINY = 1e-30
VMEM_LIMIT = 56 * 1024 * 1024

T_Q0, T_V0, T_QI0, T_WI0, T_ROWS = 0, 512, 576, 1088, 1104
N_KK0, N_U0, N_G0, N_COLS = 0, 128, 640, 2688


def _rms(x, eps=RMS_EPS):
    return x * lax.rsqrt(jnp.mean(x * x, axis=-1, keepdims=True) + eps)


def _proj_kernel(x_ref, gmix_ref, wt_ref, wn_ref, gq_ref, gk_ref,
                 qt_ref, qit_ref, wit_ref, vt_ref, kaug_ref, kiaug_ref, u_ref, g_ref,
                 *, tm, seq):
    i = pl.program_id(0)
    x = x_ref[...]
    h = (_rms(x) * gmix_ref[...]).astype(jnp.bfloat16)
    y_t = lax.dot_general(wt_ref[...], h, (((1,), (1,)), ((), ())),
                          preferred_element_type=jnp.float32)
    y_n = jnp.dot(h, wn_ref[...], preferred_element_type=jnp.float32)

    gq = gq_ref[...]
    for hd in range(N_HEADS):
        blk = y_t[T_Q0 + hd * HEAD_DIM:T_Q0 + (hd + 1) * HEAD_DIM, :]
        ss = jnp.mean(blk * blk, axis=0, keepdims=True)
        qn = blk * lax.rsqrt(ss + RMS_EPS) * gq * (HEAD_DIM ** -0.5 * LOG2E)
        qt_ref[0, hd * HEAD_DIM:(hd + 1) * HEAD_DIM, :] = qn.astype(qt_ref.dtype)

    qit_ref[0] = y_t[T_QI0:T_QI0 + IDX_HEADS * IDX_DIM, :].astype(qit_ref.dtype)
    wit_ref[0] = y_t[T_WI0:T_WI0 + IDX_HEADS, :] * IDX_SCALE

    v_t = y_t[T_V0:T_V0 + HEAD_DIM, :]
    row = lax.broadcasted_iota(jnp.int32, (VT_ROWS - HEAD_DIM, tm), 0)
    tail = jnp.where(row == 0, 1.0, 0.0)
    vt_full = jnp.concatenate([v_t, tail], axis=0).astype(vt_ref.dtype)
    for cc in range(tm // KEY_CHUNK):
        vt_ref[0, cc] = vt_full[:, cc * KEY_CHUNK:(cc + 1) * KEY_CHUNK]

    kk = y_n[:, N_KK0:N_KK0 + LANES]
    lane = lax.broadcasted_iota(jnp.int32, (tm, LANES), 1)
    is_k = lane < HEAD_DIM
    ss = jnp.sum(jnp.where(is_k, kk * kk, 0.0), axis=-1, keepdims=True) * (1.0 / HEAD_DIM)
    kn = kk * lax.rsqrt(ss + RMS_EPS) * gk_ref[...]
    pos = (i * tm) % seq + lax.broadcasted_iota(jnp.int32, (tm, LANES), 0)
    pos_hi = ((pos >> 8) << 8).astype(jnp.float32)
    pos_lo = (pos & 255).astype(jnp.float32)
    feats = jnp.where(lane < HEAD_DIM + SLOPE_PARTS, pos_hi,
                      jnp.where(lane < HEAD_DIM + 2 * SLOPE_PARTS, pos_lo, 0.0))
    kaug_ref[0] = jnp.where(is_k, kn, feats).astype(kaug_ref.dtype)
    kiaug_ref[0] = jnp.where(is_k, 0.0, kk).astype(kiaug_ref.dtype)

    u_ref[...] = y_n[:, N_U0:N_U0 + POOL_WIDTH]
    g_ref[...] = y_n[:, N_G0:N_G0 + 2 * D_MODEL].astype(g_ref.dtype)


def _proj(x2, gmix, wt, wn, gq, gk, *, batch, seq, tm):
    n = batch * seq
    tiles_per_seq = seq // tm
    const = lambda i: (0, 0)
    bidx = lambda i: (i // tiles_per_seq, 0, i % tiles_per_seq)
    out_shape = (
        jax.ShapeDtypeStruct((batch, ATTN_WIDTH, seq), jnp.bfloat16),
        jax.ShapeDtypeStruct((batch, IDX_HEADS * IDX_DIM, seq), jnp.bfloat16),
        jax.ShapeDtypeStruct((batch, IDX_HEADS, seq), jnp.float32),
        jax.ShapeDtypeStruct((batch, seq // KEY_CHUNK, VT_ROWS, KEY_CHUNK), jnp.bfloat16),
        jax.ShapeDtypeStruct((batch, seq, AUG), jnp.bfloat16),
        jax.ShapeDtypeStruct((batch, seq, AUG), jnp.bfloat16),
        jax.ShapeDtypeStruct((n, POOL_WIDTH), jnp.float32),
        jax.ShapeDtypeStruct((n, 2 * D_MODEL), jnp.bfloat16),
    )
    out_specs = (
        pl.BlockSpec((1, ATTN_WIDTH, tm), bidx),
        pl.BlockSpec((1, IDX_HEADS * IDX_DIM, tm), bidx),
        pl.BlockSpec((1, IDX_HEADS, tm), bidx),
        pl.BlockSpec((1, tm // KEY_CHUNK, VT_ROWS, KEY_CHUNK),
                     lambda i: (i // tiles_per_seq, i % tiles_per_seq, 0, 0)),
        pl.BlockSpec((1, tm, AUG), lambda i: (i // tiles_per_seq, i % tiles_per_seq, 0)),
        pl.BlockSpec((1, tm, AUG), lambda i: (i // tiles_per_seq, i % tiles_per_seq, 0)),
        pl.BlockSpec((tm, POOL_WIDTH), lambda i: (i, 0)),
        pl.BlockSpec((tm, 2 * D_MODEL), lambda i: (i, 0)),
    )
    in_specs = [
        pl.BlockSpec((tm, D_MODEL), lambda i: (i, 0)),
        pl.BlockSpec((1, D_MODEL), const),
        pl.BlockSpec((T_ROWS, D_MODEL), const),
        pl.BlockSpec((D_MODEL, N_COLS), const),
        pl.BlockSpec((HEAD_DIM, 1), const),
        pl.BlockSpec((1, LANES), const),
    ]
    return pl.pallas_call(
        functools.partial(_proj_kernel, tm=tm, seq=seq),
        grid=(n // tm,),
        in_specs=in_specs,
        out_specs=out_specs,
        out_shape=out_shape,
        compiler_params=pltpu.CompilerParams(
            dimension_semantics=("arbitrary",), vmem_limit_bytes=VMEM_LIMIT),
        name="proj",
    )(x2, gmix, wt, wn, gq, gk)


def _attn_kernel(qt_ref, qit_ref, wit_ref, vt_ref, kaug_ref, kiaug_ref, qaug_ref,
                 at_ref, rhs_idx, rhs_att, sc_ref, sc16_ref, acc_t, buf_a, buf_b):
    j = pl.program_id(1)
    nq = Q_BLOCK
    kc = KEY_CHUNK
    nch = j + 1
    wide = N_HEADS * nq
    inf = jnp.float32(jnp.inf)

    zeros64 = jnp.zeros((HEAD_DIM, nq), rhs_idx.dtype)
    for hd in range(N_HEADS):
        cols = slice(hd * nq, (hd + 1) * nq)
        rows = slice(hd * HEAD_DIM, (hd + 1) * HEAD_DIM)
        rhs_idx[0:HEAD_DIM, cols] = zeros64
        rhs_idx[HEAD_DIM:AUG, cols] = qit_ref[0, rows, :]
        rhs_att[0:HEAD_DIM, cols] = qt_ref[0, rows, :]
    rhs_att[HEAD_DIM:AUG, :] = qaug_ref[...]

    w_all = wit_ref[0]
    t_row = j * nq + lax.broadcasted_iota(jnp.int32, (1, nq), 1)
    t_pos = j * nq + lax.broadcasted_iota(jnp.int32, (kc, nq), 1)
    s_loc = lax.broadcasted_iota(jnp.int32, (kc, nq), 0)

    def pairwise(step, n, carry):
        carry = lax.fori_loop(0, n // 2, lambda i, cr: step(2 * i + 1, step(2 * i, cr)), carry)
        return lax.cond(n % 2 == 1, lambda cr: step(n - 1, cr), lambda cr: cr, carry)

    def fold(x, op, rows=8):
        parts = [x[r * rows:(r + 1) * rows, :] for r in range(kc // rows)]
        while len(parts) > 1:
            parts = [op(parts[i], parts[i + 1]) for i in range(0, len(parts), 2)]
        return parts[0]

    bufs = (buf_a, buf_b)

    def cols(hd):
        return slice(hd * nq, (hd + 1) * nq)

    def pipelined(produce, consume, finish, carry, prepare=lambda c: None):
        heads = range(N_HEADS)

        def drain(c, par, carry, aux):
            part = None
            for h in heads:
                part = consume(c, bufs[par], h, carry, aux, part)
            return finish(c, carry, part)

        def both(c, par, carry, aux):
            part, nxt, ctx = None, [], prepare(c + 1)
            for h in heads:
                nxt.append(produce(c + 1, bufs[1 - par], h, ctx))
                part = consume(c, bufs[par], h, carry, aux, part)
            return finish(c, carry, part), tuple(nxt)

        def pair_body(i, st):
            return both(2 * i + 1, 1, *both(2 * i, 0, *st))

        ctx0 = prepare(0)
        first = tuple(produce(0, bufs[0], h, ctx0) for h in heads)
        st = lax.fori_loop(0, (nch - 1) // 2, pair_body, (carry, first))

        def even_tail(st):
            return drain(nch - 1, 1, *both(nch - 2, 0, *st))

        return lax.cond(nch % 2 == 0, even_tail, lambda st: drain(nch - 1, 0, *st), st)

    def score_mm(c, buf, h, _):
        off = pl.multiple_of(c * kc, kc)
        buf[:, cols(h)] = jnp.dot(kiaug_ref[0, pl.ds(off, kc), :], rhs_idx[:, cols(h)],
                                  preferred_element_type=jnp.float32)
        return jnp.int32(0)

    def score_head(c, buf, h, lo_hi, _, tot):
        term = jnp.maximum(buf[:, cols(h)], 0.0) * w_all[h:h + 1, :]
        return term if tot is None else tot + term

    def score_store(c, lo_hi, tot):
        off = pl.multiple_of(c * kc, kc)
        causal = off + s_loc <= t_pos
        masked = jnp.where(causal, tot, -inf)
        sc_ref[pl.ds(off, kc), :] = masked
        sc16_ref[pl.ds(off, kc), :] = masked.astype(jnp.bfloat16)
        lo, hi = lo_hi
        return (jnp.minimum(lo, fold(jnp.where(causal, tot, inf), jnp.minimum)),
                jnp.maximum(hi, fold(masked, jnp.maximum)))

    lo, hi = pipelined(score_mm, score_head, score_store,
                       (jnp.full((8, nq), inf, jnp.float32), jnp.full((8, nq), -inf, jnp.float32)))
    lo = jnp.min(lo, axis=0, keepdims=True)
    hi = jnp.max(hi, axis=0, keepdims=True)

    def count_ge(v):
        def step(c, acc):
            off = pl.multiple_of(c * kc, kc)
            return acc + fold(jnp.where(sc_ref[pl.ds(off, kc), :] >= v, 1.0, 0.0), jnp.add)
        acc = pairwise(step, nch, jnp.zeros((8, nq), jnp.float32))
        return jnp.sum(acc, axis=0, keepdims=True)

    def count_ge16(v):
        v16 = v.astype(jnp.bfloat16)
        one, zero = jnp.ones((), jnp.bfloat16), jnp.zeros((), jnp.bfloat16)
        def step(c, acc):
            off = pl.multiple_of(c * kc, kc)
            hit = jnp.where(sc16_ref[pl.ds(off, kc), :] >= v16, one, zero)
            return acc + fold(hit, jnp.add, PACK_ROWS)
        acc = pairwise(step, nch, jnp.zeros((PACK_ROWS, nq), jnp.bfloat16))
        return jnp.sum(acc.astype(jnp.float32), axis=0, keepdims=True)

    def max_below(v):
        def step(c, acc):
            off = pl.multiple_of(c * kc, kc)
            s = sc_ref[pl.ds(off, kc), :]
            return jnp.maximum(acc, fold(jnp.where(s < v, s, -inf), jnp.maximum))
        acc = pairwise(step, nch, jnp.full((8, nq), -inf, jnp.float32))
        return jnp.max(acc, axis=0, keepdims=True)

    def n_open(done):
        return jnp.sum(1 - done)

    def snap(state):
        lo, hi, thr, cge, done = state
        b = max_below(hi)
        c = count_ge(b)
        ok = c >= TOPK
        newly = ok & (done == 0)
        return (lo, jnp.where(ok, hi, b), jnp.where(newly, b, thr),
                jnp.where(newly, c, cge), jnp.where(ok, 1, done))

    def bisect(state):
        lo, hi, thr, cge, done = state
        mid = 0.5 * lo + 0.5 * hi
        ge = count_ge(mid) >= TOPK
        return (jnp.where(ge, mid, lo), jnp.where(ge, hi, mid), thr, cge, done)

    def bisect16(_, lo_hi):
        lo, hi = lo_hi
        mid = (0.5 * lo + 0.5 * hi).astype(jnp.bfloat16).astype(jnp.float32)
        ge = count_ge16(mid) >= TOPK
        return jnp.where(ge, mid, lo), jnp.where(ge, hi, mid)

    short = jnp.where((t_row + 1) <= TOPK, 1, 0)
    n0 = n_open(short)
    run = n0 > 0
    lo16 = lo.astype(jnp.bfloat16).astype(jnp.float32)
    hi16 = hi.astype(jnp.bfloat16).astype(jnp.float32)
    hi16 = (hi16 + jnp.abs(hi16) * 2.0 ** -6 + TINY).astype(jnp.bfloat16).astype(jnp.float32)
    lo16, hi16 = lax.fori_loop(0, jnp.where(run, BISECT16_STEPS, 0), bisect16, (lo16, hi16))
    lo32 = jnp.maximum(lo, lo16 - jnp.abs(lo16) * 2.0 ** -7 - TINY)
    state = (lo32, hi16, lo, jnp.zeros((1, nq), jnp.float32), short)
    state = lax.fori_loop(0, jnp.where(run, BISECT_STEPS, 0), lambda _, s: bisect(s), state)
    state = lax.cond(run, snap, lambda s: s, state)

    def finish_body(carry):
        s = snap(bisect(bisect(carry[0])))
        return s, n_open(s[4])

    state, _ = lax.while_loop(lambda carry: carry[1] > 0, finish_body, (state, n_open(state[4])))
    thr, cge = state[2], state[3]

    @pl.when(jnp.max(cge) > TOPK)
    def _():
        def gt_body(c, acc):
            off = pl.multiple_of(c * kc, kc)
            return acc + fold(jnp.where(sc_ref[pl.ds(off, kc), :] > thr, 1.0, 0.0), jnp.add)
        cnt_gt = jnp.sum(lax.fori_loop(0, nch, gt_body, jnp.zeros((8, nq), jnp.float32)),
                         axis=0, keepdims=True)
        need = TOPK - cnt_gt
        r_i = lax.broadcasted_iota(jnp.int32, (kc, kc), 0)
        c_i = lax.broadcasted_iota(jnp.int32, (kc, kc), 1)
        tri = jnp.where(c_i <= r_i, 1.0, 0.0).astype(jnp.bfloat16)

        def tie_body(c, seen):
            off = pl.multiple_of(c * kc, kc)
            s = sc_ref[pl.ds(off, kc), :]
            eq = s == thr
            eq_f = jnp.where(eq, 1.0, 0.0)
            incl = jnp.dot(tri, eq_f.astype(jnp.bfloat16), preferred_element_type=jnp.float32)
            rank = incl - eq_f + seen
            sc_ref[pl.ds(off, kc), :] = jnp.where(eq & (rank >= need), -inf, s)
            return seen + incl[kc - 1:kc, :]

        lax.fori_loop(0, nch, tie_body, jnp.zeros((1, nq), jnp.float32))

    acc_t[...] = jnp.zeros_like(acc_t)

    def select_bias(c):
        off = pl.multiple_of(c * kc, kc)
        return jnp.where(sc_ref[pl.ds(off, kc), :] >= thr, 0.0, NEG_BIG)

    def masked_logits(c, buf, h, bias):
        off = pl.multiple_of(c * kc, kc)
        lm = jnp.dot(kaug_ref[0, pl.ds(off, kc), :], rhs_att[:, cols(h)],
                     preferred_element_type=jnp.float32) + bias
        buf[:, cols(h)] = lm
        return jnp.max(fold(lm, jnp.maximum), axis=0, keepdims=True)

    def softmax_pv(c, buf, h, m_old, cmax, m_list):
        m_new = jnp.maximum(m_old[h], cmax[h])
        alpha = jnp.exp2(m_old[h] - m_new)
        p = jnp.exp2((buf[:, cols(h)] - m_new).astype(jnp.bfloat16))
        pv = jnp.dot(vt_ref[0, c], p, preferred_element_type=jnp.float32)
        acc_t[:, cols(h)] = acc_t[:, cols(h)] * alpha + pv
        return (m_list or ()) + (m_new,)

    pipelined(masked_logits, softmax_pv, lambda c, m_old, m_list: m_list,
              tuple(jnp.full((1, nq), NEG_BIG, jnp.float32) for _ in range(N_HEADS)),
              prepare=select_bias)

    inv_l = 1.0 / acc_t[HEAD_DIM:HEAD_DIM + 1, :]
    for hd in range(N_HEADS):
        at_ref[0, hd * HEAD_DIM:(hd + 1) * HEAD_DIM, :] = (
            acc_t[0:HEAD_DIM, hd * nq:(hd + 1) * nq] * inv_l[:, hd * nq:(hd + 1) * nq]
        ).astype(at_ref.dtype)


def _attn(qt, qit, wit, vt, kaug, kiaug, qaug, *, batch, seq):
    nq = Q_BLOCK
    assert KEY_CHUNK == Q_BLOCK and seq % nq == 0
    qblk = lambda b, j: (b, 0, j)
    whole = lambda b, j: (b, 0, 0)
    return pl.pallas_call(
        _attn_kernel,
        grid=(batch, seq // nq),
        in_specs=[
            pl.BlockSpec((1, ATTN_WIDTH, nq), qblk),
            pl.BlockSpec((1, IDX_HEADS * IDX_DIM, nq), qblk),
            pl.BlockSpec((1, IDX_HEADS, nq), qblk),
            pl.BlockSpec((1, seq // KEY_CHUNK, VT_ROWS, KEY_CHUNK), lambda b, j: (b, 0, 0, 0)),
            pl.BlockSpec((1, seq, AUG), whole),
            pl.BlockSpec((1, seq, AUG), whole),
            pl.BlockSpec((AUG - HEAD_DIM, N_HEADS * nq), lambda b, j: (0, 0)),
        ],
        out_specs=pl.BlockSpec((1, ATTN_WIDTH, nq), qblk),
        out_shape=jax.ShapeDtypeStruct((batch, ATTN_WIDTH, seq), jnp.bfloat16),
        scratch_shapes=[
            pltpu.VMEM((AUG, N_HEADS * nq), jnp.bfloat16),
            pltpu.VMEM((AUG, N_HEADS * nq), jnp.bfloat16),
            pltpu.VMEM((seq, nq), jnp.float32),
            pltpu.VMEM((seq, nq), jnp.bfloat16),
            pltpu.VMEM((VT_ROWS, N_HEADS * nq), jnp.float32),
            pltpu.VMEM((KEY_CHUNK, N_HEADS * nq), jnp.float32),
            pltpu.VMEM((KEY_CHUNK, N_HEADS * nq), jnp.float32),
        ],
        compiler_params=pltpu.CompilerParams(
            dimension_semantics=("arbitrary", "arbitrary"), vmem_limit_bytes=VMEM_LIMIT),
        name="attn",
    )(qt, qit, wit, vt, kaug, kiaug, qaug)


def _mix_kernel(x_ref, at_ref, u_ref, halo_ref, g_ref, wpg_ref, pscale_ref, wba_ref, wbp_ref,
                wout_ref, gffn_ref, wg_ref, wu_ref, wd_ref, o_ref, ubuf, *, tm, seq):
    i = pl.program_id(0)
    tiles_per_seq = seq // tm
    first = (i % tiles_per_seq) == 0
    halo = halo_ref[...]
    ubuf[0:HALO, :] = jnp.where(first, jnp.zeros_like(halo), halo)
    ubuf[HALO:HALO + tm, :] = u_ref[...]

    t_pos = (i * tm) % seq + lax.broadcasted_iota(jnp.int32, (tm, POOL_GROUP_WIDTH), 0)
    pooled = []
    for g, w in enumerate(POOL_WINDOWS):
        lanes = slice(g * POOL_GROUP_WIDTH, (g + 1) * POOL_GROUP_WIDTH)
        levels = w.bit_length() - 1
        cur = ubuf[:, lanes]
        step = 1
        for _ in range(levels):
            cur = cur + pltpu.roll(cur, step, axis=0)
            step *= 2
        wsum = cur[HALO:, :]
        cnt = jnp.minimum(t_pos + 1, w).astype(jnp.float32)
        d = (wsum / cnt - ubuf[HALO:HALO + tm, lanes]).astype(jnp.bfloat16)
        pooled.append(jnp.dot(d, wpg_ref[g], preferred_element_type=jnp.float32))
    p = (jnp.concatenate(pooled, axis=-1) * pscale_ref[...]).astype(jnp.bfloat16)

    a_proj = lax.dot_general(at_ref[0], wba_ref[...], (((0,), (0,)), ((), ())),
                             preferred_element_type=jnp.float32)
    p_proj = jnp.dot(p, wbp_ref[...], preferred_element_type=jnp.float32)
    gates = g_ref[...].astype(jnp.float32)
    merged = (jax.nn.sigmoid(gates[:, :D_MODEL]) * a_proj
              + jax.nn.sigmoid(gates[:, D_MODEL:]) * p_proj)
    x1 = x_ref[...] + jnp.dot(merged.astype(jnp.bfloat16), wout_ref[...],
                              preferred_element_type=jnp.float32)

    h2 = (_rms(x1) * gffn_ref[...]).astype(jnp.bfloat16)
    gate = jnp.dot(h2, wg_ref[...], preferred_element_type=jnp.float32)
    up = jnp.dot(h2, wu_ref[...], preferred_element_type=jnp.float32)
    act = (jax.nn.silu(gate) * up).astype(jnp.bfloat16)
    o_ref[...] = x1 + jnp.dot(act, wd_ref[...], preferred_element_type=jnp.float32)


def _mix(x2, at, u, gates, wpg, pscale, wba, wbp, wout, gffn, wg, wu, wd, *, batch, seq, tm):
    n = batch * seq
    tiles_per_seq = seq // tm
    row = lambda i: (i, 0)
    const2 = lambda i: (0, 0)
    resident = functools.partial(pl.BlockSpec, pipeline_mode=pl.Buffered(1))
    return pl.pallas_call(
        functools.partial(_mix_kernel, tm=tm, seq=seq),
        grid=(n // tm,),
        in_specs=[
            pl.BlockSpec((tm, D_MODEL), row),
            pl.BlockSpec((1, ATTN_WIDTH, tm), lambda i: (i // tiles_per_seq, 0, i % tiles_per_seq)),
            pl.BlockSpec((tm, POOL_WIDTH), row),
            pl.BlockSpec((HALO, POOL_WIDTH), lambda i: (jnp.maximum(i * (tm // HALO) - 1, 0), 0)),
            pl.BlockSpec((tm, 2 * D_MODEL), row),
            resident((len(POOL_WINDOWS), POOL_GROUP_WIDTH, POOL_GROUP_WIDTH), lambda i: (0, 0, 0)),
            resident((1, POOL_WIDTH), const2),
            resident((ATTN_WIDTH, D_MODEL), const2),
            resident((POOL_WIDTH, D_MODEL), const2),
            resident((D_MODEL, D_MODEL), const2),
            resident((1, D_MODEL), const2),
            resident((D_MODEL, D_FF), const2),
            resident((D_MODEL, D_FF), const2),
            resident((D_FF, D_MODEL), const2),
        ],
        out_specs=pl.BlockSpec((tm, D_MODEL), row),
        out_shape=jax.ShapeDtypeStruct((n, D_MODEL), jnp.float32),
        scratch_shapes=[pltpu.VMEM((tm + HALO, POOL_WIDTH), jnp.float32)],
        compiler_params=pltpu.CompilerParams(
            dimension_semantics=("arbitrary",), vmem_limit_bytes=VMEM_LIMIT),
        name="mix",
    )(x2, at, u, u, gates, wpg, pscale, wba, wbp, wout, gffn, wg, wu, wd)


def _alibi_rows():
    slopes = 2.0 ** (-8.0 * jnp.arange(1, N_HEADS + 1, dtype=jnp.float32) / N_HEADS) * LOG2E
    parts, rest = [], slopes
    for _ in range(SLOPE_PARTS):
        term = rest.astype(jnp.bfloat16)
        parts.append(term)
        rest = rest - term.astype(jnp.float32)
    block = jnp.stack(parts * 2, axis=0)
    block = jnp.repeat(block, Q_BLOCK, axis=1)
    pad = jnp.zeros((AUG - HEAD_DIM - 2 * SLOPE_PARTS, N_HEADS * Q_BLOCK), jnp.bfloat16)
    return jnp.concatenate([block, pad], axis=0)


def _layer(x2, norm_mix, w_in, q_norm, k_norm, w_pool_group, pool_scale, w_branch_attn,
           w_branch_pool, w_out, norm_ffn, w_ffn_gate, w_ffn_up, w_ffn_down, *, batch, seq):
    bf = jnp.bfloat16
    c = [0, 512, 576, 640, 1152, 1216, 1224, 1736, 3784]
    wq, wk, wv, wqi, wki, wwi, wu_in, wgates = (w_in[:, c[n]:c[n + 1]] for n in range(8))
    pad = jnp.zeros((D_MODEL, T_ROWS - T_WI0 - IDX_HEADS), w_in.dtype)
    wt = jnp.concatenate([wq, wv, wqi, wwi, pad], axis=1).T.astype(bf)
    wn = jnp.concatenate([wk, wki, wu_in, wgates], axis=1).astype(bf)
    gk = jnp.concatenate([k_norm, jnp.zeros((LANES - HEAD_DIM,), k_norm.dtype)])[None, :]

    qt, qit, wit, vt, kaug, kiaug, u, gates = _proj(
        x2, norm_mix[None, :], wt, wn, q_norm[:, None], gk, batch=batch, seq=seq, tm=512)
    at = _attn(qt, qit, wit, vt, kaug, kiaug, _alibi_rows(), batch=batch, seq=seq)
    return _mix(x2, at, u, gates, w_pool_group.astype(bf), pool_scale[None, :],
                w_branch_attn.astype(bf), w_branch_pool.astype(bf), w_out.astype(bf),
                norm_ffn[None, :], w_ffn_gate.astype(bf), w_ffn_up.astype(bf),
                w_ffn_down.astype(bf), batch=batch, seq=seq, tm=256)


def kernel(x, norm_mix, w_in, q_norm, k_norm, w_pool_group, pool_scale, w_branch_attn,
           w_branch_pool, w_out, norm_ffn, w_ffn_gate, w_ffn_up, w_ffn_down):
    batch, seq, d = x.shape
    x2 = x.reshape(batch * seq, d)
    for l in range(norm_mix.shape[0]):
        x2 = _layer(x2, norm_mix[l], w_in[l], q_norm[l], k_norm[l], w_pool_group[l],
                    pool_scale[l], w_branch_attn[l], w_branch_pool[l], w_out[l], norm_ffn[l],
                    w_ffn_gate[l], w_ffn_up[l], w_ffn_down[l], batch=batch, seq=seq)
    return x2.reshape(batch, seq, d)
```

```python
import functools

import jax
import jax.numpy as jnp
from jax import lax
from jax.experimental import pallas as pl
from jax.experimental.pallas import tpu as pltpu

D_MODEL = 1024
N_HEADS = 8
HEAD_DIM = 64
ATTN_WIDTH = N_HEADS * HEAD_DIM
IDX_HEADS = 8
IDX_DIM = 64
IDX_SCALE = (IDX_HEADS ** -0.5) * (IDX_DIM ** -0.5)
TOPK = 256
POOL_WINDOWS = (2, 4, 8, 16)
POOL_WIDTH = 512
POOL_GROUP_WIDTH = 128
D_FF = 2816
RMS_EPS = 1e-6
NEG_BIG = -1e30

LANES = 128
Q_BLOCK = 256
GROUP = 2
KEY_CHUNK = 256
AUG = 128
VT_ROWS = 80
HALO = 16
LOG2E = 1.4426950408889634
SLOPE_PARTS = 3
PACK_ROWS = 16
BISECT16_STEPS = 10
BISECT_STEPS = 6
TINY = 1e-30
VMEM_LIMIT = 56 * 1024 * 1024

T_Q0, T_V0, T_QI0, T_WI0, T_ROWS = 0, 512, 576, 1088, 1104
N_KK0, N_U0, N_G0, N_COLS = 0, 128, 640, 2688


def _rms(x, eps=RMS_EPS):
    return x * lax.rsqrt(jnp.mean(x * x, axis=-1, keepdims=True) + eps)


def _proj_kernel(x_ref, gmix_ref, wt_ref, wn_ref, gq_ref, gk_ref,
                 qt_ref, qit_ref, wit_ref, vt_ref, kaug_ref, kiaug_ref, u_ref, g_ref,
                 *, tm, seq):
    i = pl.program_id(0)
    x = x_ref[...]
    h = (_rms(x) * gmix_ref[...]).astype(jnp.bfloat16)
    y_t = lax.dot_general(wt_ref[...], h, (((1,), (1,)), ((), ())),
                          preferred_element_type=jnp.float32)
    y_n = jnp.dot(h, wn_ref[...], preferred_element_type=jnp.float32)

    gq = gq_ref[...]
    for hd in range(N_HEADS):
        blk = y_t[T_Q0 + hd * HEAD_DIM:T_Q0 + (hd + 1) * HEAD_DIM, :]
        ss = jnp.mean(blk * blk, axis=0, keepdims=True)
        qn = blk * lax.rsqrt(ss + RMS_EPS) * gq * (HEAD_DIM ** -0.5 * LOG2E)
        qt_ref[0, hd * HEAD_DIM:(hd + 1) * HEAD_DIM, :] = qn.astype(qt_ref.dtype)

    qit_ref[0] = y_t[T_QI0:T_QI0 + IDX_HEADS * IDX_DIM, :].astype(qit_ref.dtype)
    wit_ref[0] = y_t[T_WI0:T_WI0 + IDX_HEADS, :] * IDX_SCALE

    v_t = y_t[T_V0:T_V0 + HEAD_DIM, :]
    row = lax.broadcasted_iota(jnp.int32, (VT_ROWS - HEAD_DIM, tm), 0)
    tail = jnp.where(row == 0, 1.0, 0.0)
    vt_full = jnp.concatenate([v_t, tail], axis=0).astype(vt_ref.dtype)
    for cc in range(tm // KEY_CHUNK):
        vt_ref[0, cc] = vt_full[:, cc * KEY_CHUNK:(cc + 1) * KEY_CHUNK]

    kk = y_n[:, N_KK0:N_KK0 + LANES]
    lane = lax.broadcasted_iota(jnp.int32, (tm, LANES), 1)
    is_k = lane < HEAD_DIM
    ss = jnp.sum(jnp.where(is_k, kk * kk, 0.0), axis=-1, keepdims=True) * (1.0 / HEAD_DIM)
    kn = kk * lax.rsqrt(ss + RMS_EPS) * gk_ref[...]
    pos = (i * tm) % seq + lax.broadcasted_iota(jnp.int32, (tm, LANES), 0)
    pos_hi = ((pos >> 8) << 8).astype(jnp.float32)
    pos_lo = (pos & 255).astype(jnp.float32)
    feats = jnp.where(lane < HEAD_DIM + SLOPE_PARTS, pos_hi,
                      jnp.where(lane < HEAD_DIM + 2 * SLOPE_PARTS, pos_lo, 0.0))
    kaug_ref[0] = jnp.where(is_k, kn, feats).astype(kaug_ref.dtype)
    kiaug_ref[0] = jnp.where(is_k, 0.0, kk).astype(kiaug_ref.dtype)

    u_ref[...] = y_n[:, N_U0:N_U0 + POOL_WIDTH]
    g_ref[...] = y_n[:, N_G0:N_G0 + 2 * D_MODEL].astype(g_ref.dtype)


def _proj(x2, gmix, wt, wn, gq, gk, *, batch, seq, tm):
    n = batch * seq
    tiles_per_seq = seq // tm
    const = lambda i: (0, 0)
    bidx = lambda i: (i // tiles_per_seq, 0, i % tiles_per_seq)
    out_shape = (
        jax.ShapeDtypeStruct((batch, ATTN_WIDTH, seq), jnp.bfloat16),
        jax.ShapeDtypeStruct((batch, IDX_HEADS * IDX_DIM, seq), jnp.bfloat16),
        jax.ShapeDtypeStruct((batch, IDX_HEADS, seq), jnp.float32),
        jax.ShapeDtypeStruct((batch, seq // KEY_CHUNK, VT_ROWS, KEY_CHUNK), jnp.bfloat16),
        jax.ShapeDtypeStruct((batch, seq, AUG), jnp.bfloat16),
        jax.ShapeDtypeStruct((batch, seq, AUG), jnp.bfloat16),
        jax.ShapeDtypeStruct((n, POOL_WIDTH), jnp.float32),
        jax.ShapeDtypeStruct((n, 2 * D_MODEL), jnp.bfloat16),
    )
    out_specs = (
        pl.BlockSpec((1, ATTN_WIDTH, tm), bidx),
        pl.BlockSpec((1, IDX_HEADS * IDX_DIM, tm), bidx),
        pl.BlockSpec((1, IDX_HEADS, tm), bidx),
        pl.BlockSpec((1, tm // KEY_CHUNK, VT_ROWS, KEY_CHUNK),
                     lambda i: (i // tiles_per_seq, i % tiles_per_seq, 0, 0)),
        pl.BlockSpec((1, tm, AUG), lambda i: (i // tiles_per_seq, i % tiles_per_seq, 0)),
        pl.BlockSpec((1, tm, AUG), lambda i: (i // tiles_per_seq, i % tiles_per_seq, 0)),
        pl.BlockSpec((tm, POOL_WIDTH), lambda i: (i, 0)),
        pl.BlockSpec((tm, 2 * D_MODEL), lambda i: (i, 0)),
    )
    in_specs = [
        pl.BlockSpec((tm, D_MODEL), lambda i: (i, 0)),
        pl.BlockSpec((1, D_MODEL), const),
        pl.BlockSpec((T_ROWS, D_MODEL), const),
        pl.BlockSpec((D_MODEL, N_COLS), const),
        pl.BlockSpec((HEAD_DIM, 1), const),
        pl.BlockSpec((1, LANES), const),
    ]
    return pl.pallas_call(
        functools.partial(_proj_kernel, tm=tm, seq=seq),
        grid=(n // tm,),
        in_specs=in_specs,
        out_specs=out_specs,
        out_shape=out_shape,
        compiler_params=pltpu.CompilerParams(
            dimension_semantics=("arbitrary",), vmem_limit_bytes=VMEM_LIMIT),
        name="proj",
    )(x2, gmix, wt, wn, gq, gk)


def _attn_kernel(qt_ref, qit_ref, wit_ref, vt_ref, kaug_ref, kiaug_ref, qaug_ref,
                 at_ref, rhs_idx, rhs_att, sc_ref, sc16_ref, acc_t, buf_a, buf_b):
    j = pl.program_id(1)
    nq = Q_BLOCK
    kc = KEY_CHUNK
    nch = j + 1
    lanes = GROUP * nq
    tiles = GROUP * N_HEADS
    inf = jnp.float32(jnp.inf)

    def cols(t):
        return slice(t * nq, (t + 1) * nq)

    zeros64 = jnp.zeros((HEAD_DIM, nq), rhs_idx.dtype)
    for t in range(tiles):
        g, hd = divmod(t, N_HEADS)
        rows = slice(hd * HEAD_DIM, (hd + 1) * HEAD_DIM)
        rhs_idx[0:HEAD_DIM, cols(t)] = zeros64
        rhs_idx[HEAD_DIM:AUG, cols(t)] = qit_ref[g, rows, :]
        rhs_att[0:HEAD_DIM, cols(t)] = qt_ref[g, rows, :]
        rhs_att[HEAD_DIM:AUG, cols(t)] = qaug_ref[:, cols(hd)]

    t_row = j * nq + lax.broadcasted_iota(jnp.int32, (1, lanes), 1) % nq
    t_pos = j * nq + lax.broadcasted_iota(jnp.int32, (kc, nq), 1)
    s_loc = lax.broadcasted_iota(jnp.int32, (kc, nq), 0)

    def pairwise(step, n, carry):
        carry = lax.fori_loop(0, n // 2, lambda i, cr: step(2 * i + 1, step(2 * i, cr)), carry)
        return lax.cond(n % 2 == 1, lambda cr: step(n - 1, cr), lambda cr: cr, carry)

    def fold(x, op, rows=8):
        parts = [x[r * rows:(r + 1) * rows, :] for r in range(kc // rows)]
        while len(parts) > 1:
            parts = [op(parts[i], parts[i + 1]) for i in range(0, len(parts), 2)]
        return parts[0]

    bufs = (buf_a, buf_b)

    def pipelined(produce, consume, finish, carry, prepare=lambda c: None):
        def drain(c, par, carry, aux):
            part = None
            for t in range(tiles):
                part = consume(c, bufs[par], t, carry, aux, part)
            return finish(c, carry, part)

        def both(c, par, carry, aux):
            part, nxt, ctx = None, [], prepare(c + 1)
            for t in range(tiles):
                nxt.append(produce(c + 1, bufs[1 - par], t, ctx))
                part = consume(c, bufs[par], t, carry, aux, part)
            return finish(c, carry, part), tuple(nxt)

        def pair_body(i, st):
            return both(2 * i + 1, 1, *both(2 * i, 0, *st))

        ctx0 = prepare(0)
        first = tuple(produce(0, bufs[0], t, ctx0) for t in range(tiles))
        st = lax.fori_loop(0, (nch - 1) // 2, pair_body, (carry, first))

        def even_tail(st):
            return drain(nch - 1, 1, *both(nch - 2, 0, *st))

        return lax.cond(nch % 2 == 0, even_tail, lambda st: drain(nch - 1, 0, *st), st)

    def score_mm(c, buf, t, _):
        off = pl.multiple_of(c * kc, kc)
        buf[:, cols(t)] = jnp.dot(kiaug_ref[t // N_HEADS, pl.ds(off, kc), :], rhs_idx[:, cols(t)],
                                  preferred_element_type=jnp.float32)
        return jnp.int32(0)

    def score_head(c, buf, t, lo_hi, _, tots):
        g, hd = divmod(t, N_HEADS)
        term = jnp.maximum(buf[:, cols(t)], 0.0) * wit_ref[g, hd:hd + 1, :]
        tots = list(tots or [None] * GROUP)
        tots[g] = term if tots[g] is None else tots[g] + term
        return tuple(tots)

    def score_store(c, lo_hi, tots):
        off = pl.multiple_of(c * kc, kc)
        causal = off + s_loc <= t_pos
        los, his = [], []
        for g in range(GROUP):
            masked = jnp.where(causal, tots[g], -inf)
            sc_ref[pl.ds(off, kc), cols(g)] = masked
            sc16_ref[pl.ds(off, kc), cols(g)] = masked.astype(jnp.bfloat16)
            los.append(fold(jnp.where(causal, tots[g], inf), jnp.minimum))
            his.append(fold(masked, jnp.maximum))
        lo, hi = lo_hi
        return (jnp.minimum(lo, jnp.concatenate(los, axis=1)),
                jnp.maximum(hi, jnp.concatenate(his, axis=1)))

    lo, hi = pipelined(score_mm, score_head, score_store,
                       (jnp.full((8, lanes), inf, jnp.float32),
                        jnp.full((8, lanes), -inf, jnp.float32)))
    lo = jnp.min(lo, axis=0, keepdims=True)
    hi = jnp.max(hi, axis=0, keepdims=True)

    def by_lane_tile(ref, c, fn):
        off = pl.multiple_of(c * kc, kc)
        return jnp.concatenate(
            [fn(ref[pl.ds(off, kc), l * LANES:(l + 1) * LANES], slice(l * LANES, (l + 1) * LANES))
             for l in range(lanes // LANES)], axis=1)

    def count_ge(v):
        def step(c, acc):
            return acc + by_lane_tile(
                sc_ref, c, lambda s, ln: fold(jnp.where(s >= v[:, ln], 1.0, 0.0), jnp.add))
        acc = pairwise(step, nch, jnp.zeros((8, lanes), jnp.float32))
        return jnp.sum(acc, axis=0, keepdims=True)

    def count_ge16(v):
        v16 = v.astype(jnp.bfloat16)
        one, zero = jnp.ones((), jnp.bfloat16), jnp.zeros((), jnp.bfloat16)
        def step(c, acc):
            return acc + by_lane_tile(
                sc16_ref, c,
                lambda s, ln: fold(jnp.where(s >= v16[:, ln], one, zero), jnp.add, PACK_ROWS))
        acc = pairwise(step, nch, jnp.zeros((PACK_ROWS, lanes), jnp.bfloat16))
        return jnp.sum(acc.astype(jnp.float32), axis=0, keepdims=True)

    def max_below(v):
        def step(c, acc):
            return jnp.maximum(acc, by_lane_tile(
                sc_ref, c, lambda s, ln: fold(jnp.where(s < v[:, ln], s, -inf), jnp.maximum)))
        acc = pairwise(step, nch, jnp.full((8, lanes), -inf, jnp.float32))
        return jnp.max(acc, axis=0, keepdims=True)

    def n_open(done):
        return jnp.sum(1 - done)

    def snap(state):
        lo, hi, thr, cge, done = state
        b = max_below(hi)
        c = count_ge(b)
        ok = c >= TOPK
        newly = ok & (done == 0)
        return (lo, jnp.where(ok, hi, b), jnp.where(newly, b, thr),
                jnp.where(newly, c, cge), jnp.where(ok, 1, done))

    def bisect(state):
        lo, hi, thr, cge, done = state
        mid = 0.5 * lo + 0.5 * hi
        ge = count_ge(mid) >= TOPK
        return (jnp.where(ge, mid, lo), jnp.where(ge, hi, mid), thr, cge, done)

    def bisect16(_, lo_hi):
        lo, hi = lo_hi
        mid = (0.5 * lo + 0.5 * hi).astype(jnp.bfloat16).astype(jnp.float32)
        ge = count_ge16(mid) >= TOPK
        return jnp.where(ge, mid, lo), jnp.where(ge, hi, mid)

    short = jnp.where((t_row + 1) <= TOPK, 1, 0)
    n0 = n_open(short)
    run = n0 > 0
    lo16 = lo.astype(jnp.bfloat16).astype(jnp.float32)
    hi16 = hi.astype(jnp.bfloat16).astype(jnp.float32)
    hi16 = (hi16 + jnp.abs(hi16) * 2.0 ** -6 + TINY).astype(jnp.bfloat16).astype(jnp.float32)
    lo16, hi16 = lax.fori_loop(0, jnp.where(run, BISECT16_STEPS, 0), bisect16, (lo16, hi16))
    lo32 = jnp.maximum(lo, lo16 - jnp.abs(lo16) * 2.0 ** -7 - TINY)
    state = (lo32, hi16, lo, jnp.zeros((1, lanes), jnp.float32), short)
    state = lax.fori_loop(0, jnp.where(run, BISECT_STEPS, 0), lambda _, s: bisect(s), state)
    state = lax.cond(run, snap, lambda s: s, state)

    def finish_body(carry):
        s = snap(bisect(carry[0]))
        return s, n_open(s[4])

    state, _ = lax.while_loop(lambda carry: carry[1] > 0, finish_body, (state, n_open(state[4])))
    thr, cge = state[2], state[3]

    @pl.when(jnp.max(cge) > TOPK)
    def _():
        def gt_body(c, acc):
            off = pl.multiple_of(c * kc, kc)
            return acc + fold(jnp.where(sc_ref[pl.ds(off, kc), :] > thr, 1.0, 0.0), jnp.add)
        cnt_gt = jnp.sum(lax.fori_loop(0, nch, gt_body, jnp.zeros((8, lanes), jnp.float32)),
                         axis=0, keepdims=True)
        need = TOPK - cnt_gt
        r_i = lax.broadcasted_iota(jnp.int32, (kc, kc), 0)
        c_i = lax.broadcasted_iota(jnp.int32, (kc, kc), 1)
        tri = jnp.where(c_i <= r_i, 1.0, 0.0).astype(jnp.bfloat16)

        def tie_body(c, seen):
            off = pl.multiple_of(c * kc, kc)
            s = sc_ref[pl.ds(off, kc), :]
            eq = s == thr
            eq_f = jnp.where(eq, 1.0, 0.0)
            incl = jnp.dot(tri, eq_f.astype(jnp.bfloat16), preferred_element_type=jnp.float32)
            rank = incl - eq_f + seen
            sc_ref[pl.ds(off, kc), :] = jnp.where(eq & (rank >= need), -inf, s)
            return seen + incl[kc - 1:kc, :]

        lax.fori_loop(0, nch, tie_body, jnp.zeros((1, lanes), jnp.float32))

    acc_t[...] = jnp.zeros_like(acc_t)

    def select_bias(c):
        off = pl.multiple_of(c * kc, kc)
        return jnp.where(sc_ref[pl.ds(off, kc), :] >= thr, 0.0, NEG_BIG)

    def masked_logits(c, buf, t, bias):
        g = t // N_HEADS
        off = pl.multiple_of(c * kc, kc)
        lm = jnp.dot(kaug_ref[g, pl.ds(off, kc), :], rhs_att[:, cols(t)],
                     preferred_element_type=jnp.float32) + bias[:, cols(g)]
        buf[:, cols(t)] = lm
        return jnp.max(fold(lm, jnp.maximum), axis=0, keepdims=True)

    def softmax_pv(c, buf, t, m_old, cmax, m_list):
        m_new = jnp.maximum(m_old[t], cmax[t])
        alpha = jnp.exp2(m_old[t] - m_new)
        p = jnp.exp2((buf[:, cols(t)] - m_new).astype(jnp.bfloat16))
        pv = jnp.dot(vt_ref[t // N_HEADS, c], p, preferred_element_type=jnp.float32)
        acc_t[:, cols(t)] = acc_t[:, cols(t)] * alpha + pv
        return (m_list or ()) + (m_new,)

    pipelined(masked_logits, softmax_pv, lambda c, m_old, m_list: m_list,
              tuple(jnp.full((1, nq), NEG_BIG, jnp.float32) for _ in range(tiles)),
              prepare=select_bias)

    for t in range(tiles):
        g, hd = divmod(t, N_HEADS)
        inv_l = 1.0 / acc_t[HEAD_DIM:HEAD_DIM + 1, cols(t)]
        at_ref[g, hd * HEAD_DIM:(hd + 1) * HEAD_DIM, :] = (
            acc_t[0:HEAD_DIM, cols(t)] * inv_l).astype(at_ref.dtype)


def _attn(qt, qit, wit, vt, kaug, kiaug, qaug, *, batch, seq):
    nq = Q_BLOCK
    assert KEY_CHUNK == Q_BLOCK and seq % nq == 0 and batch % GROUP == 0
    wide = GROUP * N_HEADS * nq
    qblk = lambda b, j: (b, 0, j)
    whole = lambda b, j: (b, 0, 0)
    return pl.pallas_call(
        _attn_kernel,
        grid=(batch // GROUP, seq // nq),
        in_specs=[
            pl.BlockSpec((GROUP, ATTN_WIDTH, nq), qblk),
            pl.BlockSpec((GROUP, IDX_HEADS * IDX_DIM, nq), qblk),
            pl.BlockSpec((GROUP, IDX_HEADS, nq), qblk),
            pl.BlockSpec((GROUP, seq // KEY_CHUNK, VT_ROWS, KEY_CHUNK),
                         lambda b, j: (b, 0, 0, 0)),
            pl.BlockSpec((GROUP, seq, AUG), whole),
            pl.BlockSpec((GROUP, seq, AUG), whole),
            pl.BlockSpec((AUG - HEAD_DIM, N_HEADS * nq), lambda b, j: (0, 0)),
        ],
        out_specs=pl.BlockSpec((GROUP, ATTN_WIDTH, nq), qblk),
        out_shape=jax.ShapeDtypeStruct((batch, ATTN_WIDTH, seq), jnp.bfloat16),
        scratch_shapes=[
            pltpu.VMEM((AUG, wide), jnp.bfloat16),
            pltpu.VMEM((AUG, wide), jnp.bfloat16),
            pltpu.VMEM((seq, GROUP * nq), jnp.float32),
            pltpu.VMEM((seq, GROUP * nq), jnp.bfloat16),
            pltpu.VMEM((VT_ROWS, wide), jnp.float32),
            pltpu.VMEM((KEY_CHUNK, wide), jnp.float32),
            pltpu.VMEM((KEY_CHUNK, wide), jnp.float32),
        ],
        compiler_params=pltpu.CompilerParams(
            dimension_semantics=("arbitrary", "arbitrary"), vmem_limit_bytes=VMEM_LIMIT),
        name="attn",
    )(qt, qit, wit, vt, kaug, kiaug, qaug)


def _mix_kernel(x_ref, at_ref, u_ref, halo_ref, g_ref, wpg_ref, pscale_ref, wba_ref, wbp_ref,
                wout_ref, gffn_ref, wg_ref, wu_ref, wd_ref, o_ref, ubuf, *, tm, seq):
    i = pl.program_id(0)
    tiles_per_seq = seq // tm
    first = (i % tiles_per_seq) == 0
    halo = halo_ref[...]
    ubuf[0:HALO, :] = jnp.where(first, jnp.zeros_like(halo), halo)
    ubuf[HALO:HALO + tm, :] = u_ref[...]

    t_pos = (i * tm) % seq + lax.broadcasted_iota(jnp.int32, (tm, POOL_GROUP_WIDTH), 0)
    pooled = []
    for g, w in enumerate(POOL_WINDOWS):
        lanes = slice(g * POOL_GROUP_WIDTH, (g + 1) * POOL_GROUP_WIDTH)
        levels = w.bit_length() - 1
        cur = ubuf[:, lanes]
        step = 1
        for _ in range(levels):
            cur = cur + pltpu.roll(cur, step, axis=0)
            step *= 2
        wsum = cur[HALO:, :]
        cnt = jnp.minimum(t_pos + 1, w).astype(jnp.float32)
        d = (wsum / cnt - ubuf[HALO:HALO + tm, lanes]).astype(jnp.bfloat16)
        pooled.append(jnp.dot(d, wpg_ref[g], preferred_element_type=jnp.float32))
    p = (jnp.concatenate(pooled, axis=-1) * pscale_ref[...]).astype(jnp.bfloat16)

    a_proj = lax.dot_general(at_ref[0], wba_ref[...], (((0,), (0,)), ((), ())),
                             preferred_element_type=jnp.float32)
    p_proj = jnp.dot(p, wbp_ref[...], preferred_element_type=jnp.float32)
    gates = g_ref[...].astype(jnp.float32)
    merged = (jax.nn.sigmoid(gates[:, :D_MODEL]) * a_proj
              + jax.nn.sigmoid(gates[:, D_MODEL:]) * p_proj)
    x1 = x_ref[...] + jnp.dot(merged.astype(jnp.bfloat16), wout_ref[...],
                              preferred_element_type=jnp.float32)

    h2 = (_rms(x1) * gffn_ref[...]).astype(jnp.bfloat16)
    gate = jnp.dot(h2, wg_ref[...], preferred_element_type=jnp.float32)
    up = jnp.dot(h2, wu_ref[...], preferred_element_type=jnp.float32)
    act = (jax.nn.silu(gate) * up).astype(jnp.bfloat16)
    o_ref[...] = x1 + jnp.dot(act, wd_ref[...], preferred_element_type=jnp.float32)


def _mix(x2, at, u, gates, wpg, pscale, wba, wbp, wout, gffn, wg, wu, wd, *, batch, seq, tm):
    n = batch * seq
    tiles_per_seq = seq // tm
    row = lambda i: (i, 0)
    const2 = lambda i: (0, 0)
    resident = functools.partial(pl.BlockSpec, pipeline_mode=pl.Buffered(1))
    return pl.pallas_call(
        functools.partial(_mix_kernel, tm=tm, seq=seq),
        grid=(n // tm,),
        in_specs=[
            pl.BlockSpec((tm, D_MODEL), row),
            pl.BlockSpec((1, ATTN_WIDTH, tm), lambda i: (i // tiles_per_seq, 0, i % tiles_per_seq)),
            pl.BlockSpec((tm, POOL_WIDTH), row),
            pl.BlockSpec((HALO, POOL_WIDTH), lambda i: (jnp.maximum(i * (tm // HALO) - 1, 0), 0)),
            pl.BlockSpec((tm, 2 * D_MODEL), row),
            resident((len(POOL_WINDOWS), POOL_GROUP_WIDTH, POOL_GROUP_WIDTH), lambda i: (0, 0, 0)),
            resident((1, POOL_WIDTH), const2),
            resident((ATTN_WIDTH, D_MODEL), const2),
            resident((POOL_WIDTH, D_MODEL), const2),
            resident((D_MODEL, D_MODEL), const2),
            resident((1, D_MODEL), const2),
            resident((D_MODEL, D_FF), const2),
            resident((D_MODEL, D_FF), const2),
            resident((D_FF, D_MODEL), const2),
        ],
        out_specs=pl.BlockSpec((tm, D_MODEL), row),
        out_shape=jax.ShapeDtypeStruct((n, D_MODEL), jnp.float32),
        scratch_shapes=[pltpu.VMEM((tm + HALO, POOL_WIDTH), jnp.float32)],
        compiler_params=pltpu.CompilerParams(
            dimension_semantics=("arbitrary",), vmem_limit_bytes=VMEM_LIMIT),
        name="mix",
    )(x2, at, u, u, gates, wpg, pscale, wba, wbp, wout, gffn, wg, wu, wd)


def _alibi_rows():
    slopes = 2.0 ** (-8.0 * jnp.arange(1, N_HEADS + 1, dtype=jnp.float32) / N_HEADS) * LOG2E
    parts, rest = [], slopes
    for _ in range(SLOPE_PARTS):
        term = rest.astype(jnp.bfloat16)
        parts.append(term)
        rest = rest - term.astype(jnp.float32)
    block = jnp.stack(parts * 2, axis=0)
    block = jnp.repeat(block, Q_BLOCK, axis=1)
    pad = jnp.zeros((AUG - HEAD_DIM - 2 * SLOPE_PARTS, N_HEADS * Q_BLOCK), jnp.bfloat16)
    return jnp.concatenate([block, pad], axis=0)


def _layer(x2, norm_mix, w_in, q_norm, k_norm, w_pool_group, pool_scale, w_branch_attn,
           w_branch_pool, w_out, norm_ffn, w_ffn_gate, w_ffn_up, w_ffn_down, *, batch, seq):
    bf = jnp.bfloat16
    c = [0, 512, 576, 640, 1152, 1216, 1224, 1736, 3784]
    wq, wk, wv, wqi, wki, wwi, wu_in, wgates = (w_in[:, c[n]:c[n + 1]] for n in range(8))
    pad = jnp.zeros((D_MODEL, T_ROWS - T_WI0 - IDX_HEADS), w_in.dtype)
    wt = jnp.concatenate([wq, wv, wqi, wwi, pad], axis=1).T.astype(bf)
    wn = jnp.concatenate([wk, wki, wu_in, wgates], axis=1).astype(bf)
    gk = jnp.concatenate([k_norm, jnp.zeros((LANES - HEAD_DIM,), k_norm.dtype)])[None, :]

    qt, qit, wit, vt, kaug, kiaug, u, gates = _proj(
        x2, norm_mix[None, :], wt, wn, q_norm[:, None], gk, batch=batch, seq=seq, tm=512)
    at = _attn(qt, qit, wit, vt, kaug, kiaug, _alibi_rows(), batch=batch, seq=seq)
    return _mix(x2, at, u, gates, w_pool_group.astype(bf), pool_scale[None, :],
                w_branch_attn.astype(bf), w_branch_pool.astype(bf), w_out.astype(bf),
                norm_ffn[None, :], w_ffn_gate.astype(bf), w_ffn_up.astype(bf),
                w_ffn_down.astype(bf), batch=batch, seq=seq, tm=256)


def kernel(x, norm_mix, w_in, q_norm, k_norm, w_pool_group, pool_scale, w_branch_attn,
           w_branch_pool, w_out, norm_ffn, w_ffn_gate, w_ffn_up, w_ffn_down):
    batch, seq, d = x.shape
    x2 = x.reshape(batch * seq, d)
    for l in range(norm_mix.shape[0]):
        x2 = _layer(x2, norm_mix[l], w_in[l], q_norm[l], k_norm[l], w_pool_group[l],
                    pool_scale[l], w_branch_attn[l], w_branch_pool[l], w_out[l], norm_ffn[l],
                    w_ffn_gate[l], w_ffn_up[l], w_ffn_down[l], batch=batch, seq=seq)
    return x2.reshape(batch, seq, d)
```

```python
import functools

import jax
import jax.numpy as jnp
from jax import lax
from jax.experimental import pallas as pl
from jax.experimental.pallas import tpu as pltpu

D_MODEL = 1024
N_HEADS = 8
HEAD_DIM = 64
ATTN_WIDTH = N_HEADS * HEAD_DIM
IDX_HEADS = 8
IDX_DIM = 64
IDX_SCALE = (IDX_HEADS ** -0.5) * (IDX_DIM ** -0.5)
TOPK = 256
POOL_WINDOWS = (2, 4, 8, 16)
POOL_WIDTH = 512
POOL_GROUP_WIDTH = 128
D_FF = 2816
RMS_EPS = 1e-6
NEG_BIG = -1e30

LANES = 128
PROJ_ROWS = 1024
MIX_ROWS = 512
Q_BLOCK = 256
GROUP = 2
KEY_CHUNK = 256
AUG = 128
VT_ROWS = 80
HALO = 16
LOG2E = 1.4426950408889634
SLOPE_PARTS = 3
PACK_ROWS = 16
BF16_STEP = 2.0 ** -7
BF16_EXACT_INT = 256
BISECT16_STEPS = 10
BISECT_STEPS = 6
TINY = 1e-30
VMEM_LIMIT = 56 * 1024 * 1024

T_Q0, T_V0, T_QI0, T_WI0, T_ROWS = 0, 512, 576, 1088, 1104
N_KK0, N_U0, N_G0, N_COLS = 0, 128, 640, 2688


def _rms(x, eps=RMS_EPS):
    return x * lax.rsqrt(jnp.mean(x * x, axis=-1, keepdims=True) + eps)


def _proj_kernel(x_ref, gmix_ref, wt_ref, wn_ref, gq_ref, gk_ref,
                 qt_ref, qit_ref, wit_ref, vt_ref, kaug_ref, kiaug_ref, u_ref, g_ref,
                 *, tm, seq):
    i = pl.program_id(0)
    x = x_ref[...]
    h = (_rms(x) * gmix_ref[...]).astype(jnp.bfloat16)
    y_t = lax.dot_general(wt_ref[...], h, (((1,), (1,)), ((), ())),
                          preferred_element_type=jnp.float32)
    y_n = jnp.dot(h, wn_ref[...], preferred_element_type=jnp.float32)

    gq = gq_ref[...]
    for hd in range(N_HEADS):
        blk = y_t[T_Q0 + hd * HEAD_DIM:T_Q0 + (hd + 1) * HEAD_DIM, :]
        ss = jnp.mean(blk * blk, axis=0, keepdims=True)
        qn = blk * lax.rsqrt(ss + RMS_EPS) * gq * (HEAD_DIM ** -0.5 * LOG2E)
        qt_ref[0, hd * HEAD_DIM:(hd + 1) * HEAD_DIM, :] = qn.astype(qt_ref.dtype)

    qit_ref[0] = y_t[T_QI0:T_QI0 + IDX_HEADS * IDX_DIM, :].astype(qit_ref.dtype)
    wit_ref[0] = y_t[T_WI0:T_WI0 + IDX_HEADS, :] * IDX_SCALE

    v_t = y_t[T_V0:T_V0 + HEAD_DIM, :]
    row = lax.broadcasted_iota(jnp.int32, (VT_ROWS - HEAD_DIM, tm), 0)
    tail = jnp.where(row == 0, 1.0, 0.0)
    vt_full = jnp.concatenate([v_t, tail], axis=0).astype(vt_ref.dtype)
    for cc in range(tm // KEY_CHUNK):
        vt_ref[0, cc] = vt_full[:, cc * KEY_CHUNK:(cc + 1) * KEY_CHUNK]

    kk = y_n[:, N_KK0:N_KK0 + LANES]
    lane = lax.broadcasted_iota(jnp.int32, (tm, LANES), 1)
    is_k = lane < HEAD_DIM
    ss = jnp.sum(jnp.where(is_k, kk * kk, 0.0), axis=-1, keepdims=True) * (1.0 / HEAD_DIM)
    kn = kk * lax.rsqrt(ss + RMS_EPS) * gk_ref[...]
    pos = (i * tm) % seq + lax.broadcasted_iota(jnp.int32, (tm, LANES), 0)
    pos_lo = pos & (BF16_EXACT_INT - 1)
    pos_hi = (pos - pos_lo).astype(jnp.float32)
    pos_lo = pos_lo.astype(jnp.float32)
    feats = jnp.where(lane < HEAD_DIM + SLOPE_PARTS, pos_hi,
                      jnp.where(lane < HEAD_DIM + 2 * SLOPE_PARTS, pos_lo, 0.0))
    kaug_ref[0] = jnp.where(is_k, kn, feats).astype(kaug_ref.dtype)
    kiaug_ref[0] = jnp.where(is_k, 0.0, kk).astype(kiaug_ref.dtype)

    u_ref[...] = y_n[:, N_U0:N_U0 + POOL_WIDTH]
    g_ref[...] = y_n[:, N_G0:N_G0 + 2 * D_MODEL].astype(g_ref.dtype)


def _proj(x2, gmix, wt, wn, gq, gk, *, batch, seq, tm):
    n = batch * seq
    tiles_per_seq = seq // tm
    const = lambda i: (0, 0)
    bidx = lambda i: (i // tiles_per_seq, 0, i % tiles_per_seq)
    out_shape = (
        jax.ShapeDtypeStruct((batch, ATTN_WIDTH, seq), jnp.bfloat16),
        jax.ShapeDtypeStruct((batch, IDX_HEADS * IDX_DIM, seq), jnp.bfloat16),
        jax.ShapeDtypeStruct((batch, IDX_HEADS, seq), jnp.float32),
        jax.ShapeDtypeStruct((batch, seq // KEY_CHUNK, VT_ROWS, KEY_CHUNK), jnp.bfloat16),
        jax.ShapeDtypeStruct((batch, seq, AUG), jnp.bfloat16),
        jax.ShapeDtypeStruct((batch, seq, AUG), jnp.bfloat16),
        jax.ShapeDtypeStruct((n, POOL_WIDTH), jnp.float32),
        jax.ShapeDtypeStruct((n, 2 * D_MODEL), jnp.bfloat16),
    )
    out_specs = (
        pl.BlockSpec((1, ATTN_WIDTH, tm), bidx),
        pl.BlockSpec((1, IDX_HEADS * IDX_DIM, tm), bidx),
        pl.BlockSpec((1, IDX_HEADS, tm), bidx),
        pl.BlockSpec((1, tm // KEY_CHUNK, VT_ROWS, KEY_CHUNK),
                     lambda i: (i // tiles_per_seq, i % tiles_per_seq, 0, 0)),
        pl.BlockSpec((1, tm, AUG), lambda i: (i // tiles_per_seq, i % tiles_per_seq, 0)),
        pl.BlockSpec((1, tm, AUG), lambda i: (i // tiles_per_seq, i % tiles_per_seq, 0)),
        pl.BlockSpec((tm, POOL_WIDTH), lambda i: (i, 0)),
        pl.BlockSpec((tm, 2 * D_MODEL), lambda i: (i, 0)),
    )
    in_specs = [
        pl.BlockSpec((tm, D_MODEL), lambda i: (i, 0)),
        pl.BlockSpec((1, D_MODEL), const),
        pl.BlockSpec((T_ROWS, D_MODEL), const),
        pl.BlockSpec((D_MODEL, N_COLS), const),
        pl.BlockSpec((HEAD_DIM, 1), const),
        pl.BlockSpec((1, LANES), const),
    ]
    return pl.pallas_call(
        functools.partial(_proj_kernel, tm=tm, seq=seq),
        grid=(n // tm,),
        in_specs=in_specs,
        out_specs=out_specs,
        out_shape=out_shape,
        compiler_params=pltpu.CompilerParams(
            dimension_semantics=("arbitrary",), vmem_limit_bytes=VMEM_LIMIT),
        name="proj",
    )(x2, gmix, wt, wn, gq, gk)


def _attn_kernel(qt_ref, qit_ref, wit_ref, vt_ref, kaug_ref, kiaug_ref, qaug_ref,
                 at_ref, rhs_idx, rhs_att, sc_ref, sc16_ref, acc_t, buf_a, buf_b):
    j = pl.program_id(1)
    nq = Q_BLOCK
    kc = KEY_CHUNK
    nch = j + 1
    lanes = GROUP * nq
    tiles = GROUP * N_HEADS
    inf = jnp.float32(jnp.inf)

    def cols(t):
        return slice(t * nq, (t + 1) * nq)

    zeros64 = jnp.zeros((HEAD_DIM, nq), rhs_idx.dtype)
    for t in range(tiles):
        g, hd = divmod(t, N_HEADS)
        rows = slice(hd * HEAD_DIM, (hd + 1) * HEAD_DIM)
        rhs_idx[0:HEAD_DIM, cols(t)] = zeros64
        rhs_idx[HEAD_DIM:AUG, cols(t)] = qit_ref[g, rows, :]
        rhs_att[0:HEAD_DIM, cols(t)] = qt_ref[g, rows, :]
        rhs_att[HEAD_DIM:AUG, cols(t)] = qaug_ref[:, cols(hd)]

    t_row = j * nq + lax.broadcasted_iota(jnp.int32, (1, lanes), 1) % nq
    t_pos = j * nq + lax.broadcasted_iota(jnp.int32, (kc, nq), 1)
    s_loc = lax.broadcasted_iota(jnp.int32, (kc, nq), 0)

    def pairwise(step, n, carry):
        carry = lax.fori_loop(0, n // 2, lambda i, cr: step(2 * i + 1, step(2 * i, cr)), carry)
        return lax.cond(n % 2 == 1, lambda cr: step(n - 1, cr), lambda cr: cr, carry)

    def fold(x, op, rows=8):
        parts = [x[r * rows:(r + 1) * rows, :] for r in range(kc // rows)]
        while len(parts) > 1:
            parts = [op(parts[i], parts[i + 1]) for i in range(0, len(parts), 2)]
        return parts[0]

    bufs = (buf_a, buf_b)

    def pipelined(produce, consume, finish, carry, prepare=lambda c: None):
        def drain(c, par, carry, aux):
            part = None
            for t in range(tiles):
                part = consume(c, bufs[par], t, carry, aux, part)
            return finish(c, carry, part)

        def both(c, par, carry, aux):
            part, nxt, ctx = None, [], prepare(c + 1)
            for t in range(tiles):
                nxt.append(produce(c + 1, bufs[1 - par], t, ctx))
                part = consume(c, bufs[par], t, carry, aux, part)
            return finish(c, carry, part), tuple(nxt)

        def pair_body(i, st):
            return both(2 * i + 1, 1, *both(2 * i, 0, *st))

        ctx0 = prepare(0)
        first = tuple(produce(0, bufs[0], t, ctx0) for t in range(tiles))
        st = lax.fori_loop(0, (nch - 1) // 2, pair_body, (carry, first))

        def even_tail(st):
            return drain(nch - 1, 1, *both(nch - 2, 0, *st))

        return lax.cond(nch % 2 == 0, even_tail, lambda st: drain(nch - 1, 0, *st), st)

    def score_mm(c, buf, t, _):
        off = pl.multiple_of(c * kc, kc)
        buf[:, cols(t)] = jnp.dot(kiaug_ref[t // N_HEADS, pl.ds(off, kc), :], rhs_idx[:, cols(t)],
                                  preferred_element_type=jnp.float32)
        return jnp.int32(0)

    def score_head(c, buf, t, lo_hi, _, tots):
        g, hd = divmod(t, N_HEADS)
        term = jnp.maximum(buf[:, cols(t)], 0.0) * wit_ref[g, hd:hd + 1, :]
        tots = list(tots or [None] * GROUP)
        tots[g] = term if tots[g] is None else tots[g] + term
        return tuple(tots)

    def score_store(c, lo_hi, tots):
        off = pl.multiple_of(c * kc, kc)
        causal = off + s_loc <= t_pos
        los, his = [], []
        for g in range(GROUP):
            masked = jnp.where(causal, tots[g], -inf)
            sc_ref[pl.ds(off, kc), cols(g)] = masked
            sc16_ref[pl.ds(off, kc), cols(g)] = masked.astype(jnp.bfloat16)
            los.append(fold(jnp.where(causal, tots[g], inf), jnp.minimum))
            his.append(fold(masked, jnp.maximum))
        lo, hi = lo_hi
        return (jnp.minimum(lo, jnp.concatenate(los, axis=1)),
                jnp.maximum(hi, jnp.concatenate(his, axis=1)))

    lo, hi = pipelined(score_mm, score_head, score_store,
                       (jnp.full((8, lanes), inf, jnp.float32),
                        jnp.full((8, lanes), -inf, jnp.float32)))
    lo = jnp.min(lo, axis=0, keepdims=True)
    hi = jnp.max(hi, axis=0, keepdims=True)

    def by_lane_tile(ref, c, fn):
        off = pl.multiple_of(c * kc, kc)
        return jnp.concatenate(
            [fn(ref[pl.ds(off, kc), l * LANES:(l + 1) * LANES], slice(l * LANES, (l + 1) * LANES))
             for l in range(lanes // LANES)], axis=1)

    def count_ge(v):
        def step(c, acc):
            return acc + by_lane_tile(
                sc_ref, c, lambda s, ln: fold(jnp.where(s >= v[:, ln], 1.0, 0.0), jnp.add))
        acc = pairwise(step, nch, jnp.zeros((8, lanes), jnp.float32))
        return jnp.sum(acc, axis=0, keepdims=True)

    def count_ge16(v):
        v16 = v.astype(jnp.bfloat16)
        one, zero = jnp.ones((), jnp.bfloat16), jnp.zeros((), jnp.bfloat16)
        def step(c, acc):
            return acc + by_lane_tile(
                sc16_ref, c,
                lambda s, ln: fold(jnp.where(s >= v16[:, ln], one, zero), jnp.add, PACK_ROWS))
        acc = pairwise(step, nch, jnp.zeros((PACK_ROWS, lanes), jnp.bfloat16))
        return jnp.sum(acc.astype(jnp.float32), axis=0, keepdims=True)

    def max_below(v):
        def step(c, acc):
            return jnp.maximum(acc, by_lane_tile(
                sc_ref, c, lambda s, ln: fold(jnp.where(s < v[:, ln], s, -inf), jnp.maximum)))
        acc = pairwise(step, nch, jnp.full((8, lanes), -inf, jnp.float32))
        return jnp.max(acc, axis=0, keepdims=True)

    def n_open(done):
        return jnp.sum(1 - done)

    def snap(state):
        lo, hi, thr, cge, done = state
        b = max_below(hi)
        c = count_ge(b)
        ok = c >= TOPK
        newly = ok & (done == 0)
        return (lo, jnp.where(ok, hi, b), jnp.where(newly, b, thr),
                jnp.where(newly, c, cge), jnp.where(ok, 1, done))

    def bisect(state):
        lo, hi, thr, cge, done = state
        mid = 0.5 * lo + 0.5 * hi
        ge = count_ge(mid) >= TOPK
        return (jnp.where(ge, mid, lo), jnp.where(ge, hi, mid), thr, cge, done)

    def bisect16(_, lo_hi):
        lo, hi = lo_hi
        mid = (0.5 * lo + 0.5 * hi).astype(jnp.bfloat16).astype(jnp.float32)
        ge = count_ge16(mid) >= TOPK
        return jnp.where(ge, mid, lo), jnp.where(ge, hi, mid)

    short = jnp.where((t_row + 1) <= TOPK, 1, 0)
    n0 = n_open(short)
    run = n0 > 0
    lo16 = lo.astype(jnp.bfloat16).astype(jnp.float32)
    hi16 = hi.astype(jnp.bfloat16).astype(jnp.float32)
    hi16 = (hi16 + jnp.abs(hi16) * (2 * BF16_STEP) + TINY).astype(jnp.bfloat16).astype(jnp.float32)
    lo16, hi16 = lax.fori_loop(0, jnp.where(run, BISECT16_STEPS, 0), bisect16, (lo16, hi16))
    lo32 = jnp.maximum(lo, lo16 - jnp.abs(lo16) * BF16_STEP - TINY)
    state = (lo32, hi16, lo, jnp.zeros((1, lanes), jnp.float32), short)
    state = lax.fori_loop(0, jnp.where(run, BISECT_STEPS, 0), lambda _, s: bisect(s), state)
    state = lax.cond(run, snap, lambda s: s, state)

    def finish_body(carry):
        s = snap(bisect(carry[0]))
        return s, n_open(s[4])

    state, _ = lax.while_loop(lambda carry: carry[1] > 0, finish_body, (state, n_open(state[4])))
    thr, cge = state[2], state[3]

    @pl.when(jnp.max(cge) > TOPK)
    def _():
        def gt_body(c, acc):
            off = pl.multiple_of(c * kc, kc)
            return acc + fold(jnp.where(sc_ref[pl.ds(off, kc), :] > thr, 1.0, 0.0), jnp.add)
        cnt_gt = jnp.sum(lax.fori_loop(0, nch, gt_body, jnp.zeros((8, lanes), jnp.float32)),
                         axis=0, keepdims=True)
        need = TOPK - cnt_gt
        r_i = lax.broadcasted_iota(jnp.int32, (kc, kc), 0)
        c_i = lax.broadcasted_iota(jnp.int32, (kc, kc), 1)
        tri = jnp.where(c_i <= r_i, 1.0, 0.0).astype(jnp.bfloat16)

        def tie_body(c, seen):
            off = pl.multiple_of(c * kc, kc)
            s = sc_ref[pl.ds(off, kc), :]
            eq = s == thr
            eq_f = jnp.where(eq, 1.0, 0.0)
            incl = jnp.dot(tri, eq_f.astype(jnp.bfloat16), preferred_element_type=jnp.float32)
            rank = incl - eq_f + seen
            sc_ref[pl.ds(off, kc), :] = jnp.where(eq & (rank >= need), -inf, s)
            return seen + incl[kc - 1:kc, :]

        lax.fori_loop(0, nch, tie_body, jnp.zeros((1, lanes), jnp.float32))

    acc_t[...] = jnp.zeros_like(acc_t)

    def select_bias(c):
        off = pl.multiple_of(c * kc, kc)
        return jnp.where(sc_ref[pl.ds(off, kc), :] >= thr, 0.0, NEG_BIG)

    def masked_logits(c, buf, t, bias):
        g = t // N_HEADS
        off = pl.multiple_of(c * kc, kc)
        lm = jnp.dot(kaug_ref[g, pl.ds(off, kc), :], rhs_att[:, cols(t)],
                     preferred_element_type=jnp.float32) + bias[:, cols(g)]
        buf[:, cols(t)] = lm
        return jnp.max(fold(lm, jnp.maximum), axis=0, keepdims=True)

    def softmax_pv(c, buf, t, m_old, cmax, m_list):
        m_new = jnp.maximum(m_old[t], cmax[t])
        alpha = jnp.exp2(m_old[t] - m_new)
        p = jnp.exp2((buf[:, cols(t)] - m_new).astype(jnp.bfloat16))
        pv = jnp.dot(vt_ref[t // N_HEADS, c], p, preferred_element_type=jnp.float32)
        acc_t[:, cols(t)] = acc_t[:, cols(t)] * alpha + pv
        return (m_list or ()) + (m_new,)

    pipelined(masked_logits, softmax_pv, lambda c, m_old, m_list: m_list,
              tuple(jnp.full((1, nq), NEG_BIG, jnp.float32) for _ in range(tiles)),
              prepare=select_bias)

    for t in range(tiles):
        g, hd = divmod(t, N_HEADS)
        inv_l = 1.0 / acc_t[HEAD_DIM:HEAD_DIM + 1, cols(t)]
        at_ref[g, hd * HEAD_DIM:(hd + 1) * HEAD_DIM, :] = (
            acc_t[0:HEAD_DIM, cols(t)] * inv_l).astype(at_ref.dtype)


def _attn(qt, qit, wit, vt, kaug, kiaug, qaug, *, batch, seq):
    nq = Q_BLOCK
    assert KEY_CHUNK == Q_BLOCK and seq % nq == 0 and batch % GROUP == 0
    wide = GROUP * N_HEADS * nq
    qblk = lambda b, j: (b, 0, j)
    whole = lambda b, j: (b, 0, 0)
    return pl.pallas_call(
        _attn_kernel,
        grid=(batch // GROUP, seq // nq),
        in_specs=[
            pl.BlockSpec((GROUP, ATTN_WIDTH, nq), qblk),
            pl.BlockSpec((GROUP, IDX_HEADS * IDX_DIM, nq), qblk),
            pl.BlockSpec((GROUP, IDX_HEADS, nq), qblk),
            pl.BlockSpec((GROUP, seq // KEY_CHUNK, VT_ROWS, KEY_CHUNK),
                         lambda b, j: (b, 0, 0, 0)),
            pl.BlockSpec((GROUP, seq, AUG), whole),
            pl.BlockSpec((GROUP, seq, AUG), whole),
            pl.BlockSpec((AUG - HEAD_DIM, N_HEADS * nq), lambda b, j: (0, 0)),
        ],
        out_specs=pl.BlockSpec((GROUP, ATTN_WIDTH, nq), qblk),
        out_shape=jax.ShapeDtypeStruct((batch, ATTN_WIDTH, seq), jnp.bfloat16),
        scratch_shapes=[
            pltpu.VMEM((AUG, wide), jnp.bfloat16),
            pltpu.VMEM((AUG, wide), jnp.bfloat16),
            pltpu.VMEM((seq, GROUP * nq), jnp.float32),
            pltpu.VMEM((seq, GROUP * nq), jnp.bfloat16),
            pltpu.VMEM((VT_ROWS, wide), jnp.float32),
            pltpu.VMEM((KEY_CHUNK, wide), jnp.float32),
            pltpu.VMEM((KEY_CHUNK, wide), jnp.float32),
        ],
        compiler_params=pltpu.CompilerParams(
            dimension_semantics=("arbitrary", "arbitrary"), vmem_limit_bytes=VMEM_LIMIT),
        name="attn",
    )(qt, qit, wit, vt, kaug, kiaug, qaug)


def _mix_kernel(x_ref, at_ref, u_ref, halo_ref, g_ref, wpg_ref, pscale_ref, wba_ref, wbp_ref,
                wout_ref, gffn_ref, wg_ref, wu_ref, wd_ref, o_ref, ubuf, *, tm, seq):
    i = pl.program_id(0)
    tiles_per_seq = seq // tm
    first = (i % tiles_per_seq) == 0
    halo = halo_ref[...]
    ubuf[0:HALO, :] = jnp.where(first, jnp.zeros_like(halo), halo)
    ubuf[HALO:HALO + tm, :] = u_ref[...]

    t_pos = (i * tm) % seq + lax.broadcasted_iota(jnp.int32, (tm, POOL_GROUP_WIDTH), 0)
    pooled = []
    for g, w in enumerate(POOL_WINDOWS):
        lanes = slice(g * POOL_GROUP_WIDTH, (g + 1) * POOL_GROUP_WIDTH)
        levels = w.bit_length() - 1
        cur = ubuf[:, lanes]
        step = 1
        for _ in range(levels):
            cur = cur + pltpu.roll(cur, step, axis=0)
            step *= 2
        wsum = cur[HALO:, :]
        cnt = jnp.minimum(t_pos + 1, w).astype(jnp.float32)
        d = (wsum / cnt - ubuf[HALO:HALO + tm, lanes]).astype(jnp.bfloat16)
        pooled.append(jnp.dot(d, wpg_ref[g], preferred_element_type=jnp.float32))
    p = (jnp.concatenate(pooled, axis=-1) * pscale_ref[...]).astype(jnp.bfloat16)

    a_proj = lax.dot_general(at_ref[0], wba_ref[...], (((0,), (0,)), ((), ())),
                             preferred_element_type=jnp.float32)
    p_proj = jnp.dot(p, wbp_ref[...], preferred_element_type=jnp.float32)
    gates = g_ref[...].astype(jnp.float32)
    merged = (jax.nn.sigmoid(gates[:, :D_MODEL]) * a_proj
              + jax.nn.sigmoid(gates[:, D_MODEL:]) * p_proj)
    x1 = x_ref[...] + jnp.dot(merged.astype(jnp.bfloat16), wout_ref[...],
                              preferred_element_type=jnp.float32)

    h2 = (_rms(x1) * gffn_ref[...]).astype(jnp.bfloat16)
    gate = jnp.dot(h2, wg_ref[...], preferred_element_type=jnp.float32)
    up = jnp.dot(h2, wu_ref[...], preferred_element_type=jnp.float32)
    act = (jax.nn.silu(gate) * up).astype(jnp.bfloat16)
    o_ref[...] = x1 + jnp.dot(act, wd_ref[...], preferred_element_type=jnp.float32)


def _mix(x2, at, u, gates, wpg, pscale, wba, wbp, wout, gffn, wg, wu, wd, *, batch, seq, tm):
    n = batch * seq
    tiles_per_seq = seq // tm
    row = lambda i: (i, 0)
    const2 = lambda i: (0, 0)
    resident = functools.partial(pl.BlockSpec, pipeline_mode=pl.Buffered(1))
    return pl.pallas_call(
        functools.partial(_mix_kernel, tm=tm, seq=seq),
        grid=(n // tm,),
        in_specs=[
            pl.BlockSpec((tm, D_MODEL), row),
            pl.BlockSpec((1, ATTN_WIDTH, tm), lambda i: (i // tiles_per_seq, 0, i % tiles_per_seq)),
            pl.BlockSpec((tm, POOL_WIDTH), row),
            pl.BlockSpec((HALO, POOL_WIDTH), lambda i: (jnp.maximum(i * (tm // HALO) - 1, 0), 0)),
            pl.BlockSpec((tm, 2 * D_MODEL), row),
            resident((len(POOL_WINDOWS), POOL_GROUP_WIDTH, POOL_GROUP_WIDTH), lambda i: (0, 0, 0)),
            resident((1, POOL_WIDTH), const2),
            resident((ATTN_WIDTH, D_MODEL), const2),
            resident((POOL_WIDTH, D_MODEL), const2),
            resident((D_MODEL, D_MODEL), const2),
            resident((1, D_MODEL), const2),
            resident((D_MODEL, D_FF), const2),
            resident((D_MODEL, D_FF), const2),
            resident((D_FF, D_MODEL), const2),
        ],
        out_specs=pl.BlockSpec((tm, D_MODEL), row),
        out_shape=jax.ShapeDtypeStruct((n, D_MODEL), jnp.float32),
        scratch_shapes=[pltpu.VMEM((tm + HALO, POOL_WIDTH), jnp.float32)],
        compiler_params=pltpu.CompilerParams(
            dimension_semantics=("arbitrary",), vmem_limit_bytes=VMEM_LIMIT),
        name="mix",
    )(x2, at, u, u, gates, wpg, pscale, wba, wbp, wout, gffn, wg, wu, wd)


def _alibi_rows():
    slopes = 2.0 ** (-8.0 * jnp.arange(1, N_HEADS + 1, dtype=jnp.float32) / N_HEADS) * LOG2E
    parts, rest = [], slopes
    for _ in range(SLOPE_PARTS):
        term = rest.astype(jnp.bfloat16)
        parts.append(term)
        rest = rest - term.astype(jnp.float32)
    block = jnp.stack(parts * 2, axis=0)
    block = jnp.repeat(block, Q_BLOCK, axis=1)
    pad = jnp.zeros((AUG - HEAD_DIM - 2 * SLOPE_PARTS, N_HEADS * Q_BLOCK), jnp.bfloat16)
    return jnp.concatenate([block, pad], axis=0)


def _layer(x2, norm_mix, w_in, q_norm, k_norm, w_pool_group, pool_scale, w_branch_attn,
           w_branch_pool, w_out, norm_ffn, w_ffn_gate, w_ffn_up, w_ffn_down, *, batch, seq):
    bf = jnp.bfloat16
    w16 = w_in.astype(bf)
    pad = jnp.zeros((D_MODEL, T_ROWS - T_WI0 - IDX_HEADS), bf)
    wt = jnp.concatenate([w16[:, 0:512], w16[:, 576:1152], w16[:, 1216:1224], pad], axis=1).T
    wn = jnp.concatenate([w16[:, 512:576], w16[:, 1152:1216], w16[:, 1224:3784]], axis=1)
    gk = jnp.concatenate([k_norm, jnp.zeros((LANES - HEAD_DIM,), k_norm.dtype)])[None, :]

    qt, qit, wit, vt, kaug, kiaug, u, gates = _proj(
        x2, norm_mix[None, :], wt, wn, q_norm[:, None], gk, batch=batch, seq=seq, tm=PROJ_ROWS)
    at = _attn(qt, qit, wit, vt, kaug, kiaug, _alibi_rows(), batch=batch, seq=seq)
    return _mix(x2, at, u, gates, w_pool_group.astype(bf), pool_scale[None, :],
                w_branch_attn.astype(bf), w_branch_pool.astype(bf), w_out.astype(bf),
                norm_ffn[None, :], w_ffn_gate.astype(bf), w_ffn_up.astype(bf),
                w_ffn_down.astype(bf), batch=batch, seq=seq, tm=MIX_ROWS)


def kernel(x, norm_mix, w_in, q_norm, k_norm, w_pool_group, pool_scale, w_branch_attn,
           w_branch_pool, w_out, norm_ffn, w_ffn_gate, w_ffn_up, w_ffn_down):
    batch, seq, d = x.shape
    x2 = x.reshape(batch * seq, d)
    for l in range(norm_mix.shape[0]):
        x2 = _layer(x2, norm_mix[l], w_in[l], q_norm[l], k_norm[l], w_pool_group[l],
                    pool_scale[l], w_branch_attn[l], w_branch_pool[l], w_out[l], norm_ffn[l],
                    w_ffn_gate[l], w_ffn_up[l], w_ffn_down[l], batch=batch, seq=seq)
    return x2.reshape(batch, seq, d)
```

```python
import functools

import jax
import jax.numpy as jnp
from jax import lax
from jax.experimental import pallas as pl
from jax.experimental.pallas import tpu as pltpu

D_MODEL = 1024
N_HEADS = 8
HEAD_DIM = 64
ATTN_WIDTH = N_HEADS * HEAD_DIM
IDX_HEADS = 8
IDX_DIM = 64
IDX_SCALE = (IDX_HEADS ** -0.5) * (IDX_DIM ** -0.5)
TOPK = 256
POOL_WINDOWS = (2, 4, 8, 16)
POOL_WIDTH = 512
POOL_GROUP_WIDTH = 128
D_FF = 2816
RMS_EPS = 1e-6
NEG_BIG = -1e30

LANES = 128
PROJ_ROWS = 1024
MIX_ROWS = 512
Q_BLOCK = 256
GROUP = 4
KEY_CHUNK = 256
AUG = 128
VT_ROWS = 80
HALO = 16
LOG2E = 1.4426950408889634
SLOPE_PARTS = 3
PACK_ROWS = 16
BF16_STEP = 2.0 ** -7
BF16_EXACT_INT = 256
BISECT16_STEPS = 10
BISECT_STEPS = 6
TINY = 1e-30
VMEM_LIMIT = 56 * 1024 * 1024

T_Q0, T_V0, T_QI0, T_WI0, T_ROWS = 0, 512, 576, 1088, 1104
N_KK0, N_U0, N_G0, N_COLS = 0, 128, 640, 2688


def _rms(x, eps=RMS_EPS):
    return x * lax.rsqrt(jnp.mean(x * x, axis=-1, keepdims=True) + eps)


def _proj_kernel(x_ref, gmix_ref, wt_ref, wn_ref, gq_ref, gk_ref,
                 qt_ref, qit_ref, wit_ref, vt_ref, kaug_ref, kiaug_ref, u_ref, g_ref,
                 *, tm, seq):
    i = pl.program_id(0)
    x = x_ref[...]
    h = (_rms(x) * gmix_ref[...]).astype(jnp.bfloat16)
    y_t = lax.dot_general(wt_ref[...], h, (((1,), (1,)), ((), ())),
                          preferred_element_type=jnp.float32)
    y_n = jnp.dot(h, wn_ref[...], preferred_element_type=jnp.float32)

    gq = gq_ref[...]
    for hd in range(N_HEADS):
        blk = y_t[T_Q0 + hd * HEAD_DIM:T_Q0 + (hd + 1) * HEAD_DIM, :]
        ss = jnp.mean(blk * blk, axis=0, keepdims=True)
        qn = blk * lax.rsqrt(ss + RMS_EPS) * gq * (HEAD_DIM ** -0.5 * LOG2E)
        qt_ref[0, hd * HEAD_DIM:(hd + 1) * HEAD_DIM, :] = qn.astype(qt_ref.dtype)

    qit_ref[0] = y_t[T_QI0:T_QI0 + IDX_HEADS * IDX_DIM, :].astype(qit_ref.dtype)
    wit_ref[0] = y_t[T_WI0:T_WI0 + IDX_HEADS, :] * IDX_SCALE

    v_t = y_t[T_V0:T_V0 + HEAD_DIM, :]
    row = lax.broadcasted_iota(jnp.int32, (VT_ROWS - HEAD_DIM, tm), 0)
    tail = jnp.where(row == 0, 1.0, 0.0)
    vt_full = jnp.concatenate([v_t, tail], axis=0).astype(vt_ref.dtype)
    for cc in range(tm // KEY_CHUNK):
        vt_ref[0, cc] = vt_full[:, cc * KEY_CHUNK:(cc + 1) * KEY_CHUNK]

    kk = y_n[:, N_KK0:N_KK0 + LANES]
    lane = lax.broadcasted_iota(jnp.int32, (tm, LANES), 1)
    is_k = lane < HEAD_DIM
    ss = jnp.sum(jnp.where(is_k, kk * kk, 0.0), axis=-1, keepdims=True) * (1.0 / HEAD_DIM)
    kn = kk * lax.rsqrt(ss + RMS_EPS) * gk_ref[...]
    pos = (i * tm) % seq + lax.broadcasted_iota(jnp.int32, (tm, LANES), 0)
    pos_lo = pos & (BF16_EXACT_INT - 1)
    pos_hi = (pos - pos_lo).astype(jnp.float32)
    pos_lo = pos_lo.astype(jnp.float32)
    feats = jnp.where(lane < HEAD_DIM + SLOPE_PARTS, pos_hi,
                      jnp.where(lane < HEAD_DIM + 2 * SLOPE_PARTS, pos_lo, 0.0))
    kaug_ref[0] = jnp.where(is_k, kn, feats).astype(kaug_ref.dtype)
    kiaug_ref[0] = jnp.where(is_k, 0.0, kk).astype(kiaug_ref.dtype)

    u_ref[...] = y_n[:, N_U0:N_U0 + POOL_WIDTH]
    g_ref[...] = y_n[:, N_G0:N_G0 + 2 * D_MODEL].astype(g_ref.dtype)


def _proj(x2, gmix, wt, wn, gq, gk, *, batch, seq, tm):
    n = batch * seq
    tiles_per_seq = seq // tm
    const = lambda i: (0, 0)
    bidx = lambda i: (i // tiles_per_seq, 0, i % tiles_per_seq)
    out_shape = (
        jax.ShapeDtypeStruct((batch, ATTN_WIDTH, seq), jnp.bfloat16),
        jax.ShapeDtypeStruct((batch, IDX_HEADS * IDX_DIM, seq), jnp.bfloat16),
        jax.ShapeDtypeStruct((batch, IDX_HEADS, seq), jnp.float32),
        jax.ShapeDtypeStruct((batch, seq // KEY_CHUNK, VT_ROWS, KEY_CHUNK), jnp.bfloat16),
        jax.ShapeDtypeStruct((batch, seq, AUG), jnp.bfloat16),
        jax.ShapeDtypeStruct((batch, seq, AUG), jnp.bfloat16),
        jax.ShapeDtypeStruct((n, POOL_WIDTH), jnp.float32),
        jax.ShapeDtypeStruct((n, 2 * D_MODEL), jnp.bfloat16),
    )
    out_specs = (
        pl.BlockSpec((1, ATTN_WIDTH, tm), bidx),
        pl.BlockSpec((1, IDX_HEADS * IDX_DIM, tm), bidx),
        pl.BlockSpec((1, IDX_HEADS, tm), bidx),
        pl.BlockSpec((1, tm // KEY_CHUNK, VT_ROWS, KEY_CHUNK),
                     lambda i: (i // tiles_per_seq, i % tiles_per_seq, 0, 0)),
        pl.BlockSpec((1, tm, AUG), lambda i: (i // tiles_per_seq, i % tiles_per_seq, 0)),
        pl.BlockSpec((1, tm, AUG), lambda i: (i // tiles_per_seq, i % tiles_per_seq, 0)),
        pl.BlockSpec((tm, POOL_WIDTH), lambda i: (i, 0)),
        pl.BlockSpec((tm, 2 * D_MODEL), lambda i: (i, 0)),
    )
    in_specs = [
        pl.BlockSpec((tm, D_MODEL), lambda i: (i, 0)),
        pl.BlockSpec((1, D_MODEL), const),
        pl.BlockSpec((T_ROWS, D_MODEL), const),
        pl.BlockSpec((D_MODEL, N_COLS), const),
        pl.BlockSpec((HEAD_DIM, 1), const),
        pl.BlockSpec((1, LANES), const),
    ]
    return pl.pallas_call(
        functools.partial(_proj_kernel, tm=tm, seq=seq),
        grid=(n // tm,),
        in_specs=in_specs,
        out_specs=out_specs,
        out_shape=out_shape,
        compiler_params=pltpu.CompilerParams(
            dimension_semantics=("arbitrary",), vmem_limit_bytes=VMEM_LIMIT),
        name="proj",
    )(x2, gmix, wt, wn, gq, gk)


def _attn_kernel(qt_ref, qit_ref, wit_ref, vt_ref, kaug_ref, kiaug_ref, qaug_ref,
                 at_ref, rhs_idx, rhs_att, sc_ref, sc16_ref, acc_t, buf_a, buf_b):
    j = pl.program_id(1)
    nq = Q_BLOCK
    kc = KEY_CHUNK
    nch = j + 1
    lanes = GROUP * nq
    tiles = GROUP * N_HEADS
    inf = jnp.float32(jnp.inf)

    def cols(t):
        return slice(t * nq, (t + 1) * nq)

    zeros64 = jnp.zeros((HEAD_DIM, nq), rhs_idx.dtype)
    for t in range(tiles):
        g, hd = divmod(t, N_HEADS)
        rows = slice(hd * HEAD_DIM, (hd + 1) * HEAD_DIM)
        rhs_idx[0:HEAD_DIM, cols(t)] = zeros64
        rhs_idx[HEAD_DIM:AUG, cols(t)] = qit_ref[g, rows, :]
        rhs_att[0:HEAD_DIM, cols(t)] = qt_ref[g, rows, :]
        rhs_att[HEAD_DIM:AUG, cols(t)] = qaug_ref[:, cols(hd)]

    t_row = j * nq + lax.broadcasted_iota(jnp.int32, (1, lanes), 1) % nq
    t_pos = j * nq + lax.broadcasted_iota(jnp.int32, (kc, nq), 1)
    s_loc = lax.broadcasted_iota(jnp.int32, (kc, nq), 0)

    def pairwise(step, n, carry):
        carry = lax.fori_loop(0, n // 2, lambda i, cr: step(2 * i + 1, step(2 * i, cr)), carry)
        return lax.cond(n % 2 == 1, lambda cr: step(n - 1, cr), lambda cr: cr, carry)

    def fold(x, op, rows=8):
        parts = [x[r * rows:(r + 1) * rows, :] for r in range(kc // rows)]
        while len(parts) > 1:
            parts = [op(parts[i], parts[i + 1]) for i in range(0, len(parts), 2)]
        return parts[0]

    bufs = (buf_a, buf_b)

    def pipelined(produce, consume, finish, carry, prepare=lambda c: None):
        def drain(c, par, carry, aux):
            part = None
            for t in range(tiles):
                part = consume(c, bufs[par], t, carry, aux, part)
            return finish(c, carry, part)

        def both(c, par, carry, aux):
            part, nxt, ctx = None, [], prepare(c + 1)
            for t in range(tiles):
                nxt.append(produce(c + 1, bufs[1 - par], t, ctx))
                part = consume(c, bufs[par], t, carry, aux, part)
            return finish(c, carry, part), tuple(nxt)

        def pair_body(i, st):
            return both(2 * i + 1, 1, *both(2 * i, 0, *st))

        ctx0 = prepare(0)
        first = tuple(produce(0, bufs[0], t, ctx0) for t in range(tiles))
        st = lax.fori_loop(0, (nch - 1) // 2, pair_body, (carry, first))

        def even_tail(st):
            return drain(nch - 1, 1, *both(nch - 2, 0, *st))

        return lax.cond(nch % 2 == 0, even_tail, lambda st: drain(nch - 1, 0, *st), st)

    def score_mm(c, buf, t, _):
        off = pl.multiple_of(c * kc, kc)
        buf[:, cols(t)] = jnp.dot(kiaug_ref[t // N_HEADS, pl.ds(off, kc), :], rhs_idx[:, cols(t)],
                                  preferred_element_type=jnp.float32)
        return jnp.int32(0)

    def score_head(c, buf, t, lo_hi, _, tots):
        g, hd = divmod(t, N_HEADS)
        term = jnp.maximum(buf[:, cols(t)], 0.0) * wit_ref[g, hd:hd + 1, :]
        tots = list(tots or [None] * GROUP)
        tots[g] = term if tots[g] is None else tots[g] + term
        return tuple(tots)

    def score_store(c, lo_hi, tots):
        off = pl.multiple_of(c * kc, kc)
        causal = off + s_loc <= t_pos
        los, his = [], []
        for g in range(GROUP):
            masked = jnp.where(causal, tots[g], -inf)
            sc_ref[pl.ds(off, kc), cols(g)] = masked
            sc16_ref[pl.ds(off, kc), cols(g)] = masked.astype(jnp.bfloat16)
            los.append(fold(jnp.where(causal, tots[g], inf), jnp.minimum))
            his.append(fold(masked, jnp.maximum))
        lo, hi = lo_hi
        return (jnp.minimum(lo, jnp.concatenate(los, axis=1)),
                jnp.maximum(hi, jnp.concatenate(his, axis=1)))

    lo, hi = pipelined(score_mm, score_head, score_store,
                       (jnp.full((8, lanes), inf, jnp.float32),
                        jnp.full((8, lanes), -inf, jnp.float32)))
    lo = jnp.min(lo, axis=0, keepdims=True)
    hi = jnp.max(hi, axis=0, keepdims=True)

    def by_lane_tile(ref, c, fn):
        off = pl.multiple_of(c * kc, kc)
        return jnp.concatenate(
            [fn(ref[pl.ds(off, kc), l * LANES:(l + 1) * LANES], slice(l * LANES, (l + 1) * LANES))
             for l in range(lanes // LANES)], axis=1)

    def count_ge(v):
        def step(c, acc):
            return acc + by_lane_tile(
                sc_ref, c, lambda s, ln: fold(jnp.where(s >= v[:, ln], 1.0, 0.0), jnp.add))
        acc = pairwise(step, nch, jnp.zeros((8, lanes), jnp.float32))
        return jnp.sum(acc, axis=0, keepdims=True)

    def count_ge16(v):
        v16 = v.astype(jnp.bfloat16)
        one, zero = jnp.ones((), jnp.bfloat16), jnp.zeros((), jnp.bfloat16)
        def step(c, acc):
            return acc + by_lane_tile(
                sc16_ref, c,
                lambda s, ln: fold(jnp.where(s >= v16[:, ln], one, zero), jnp.add, PACK_ROWS))
        acc = pairwise(step, nch, jnp.zeros((PACK_ROWS, lanes), jnp.bfloat16))
        return jnp.sum(acc.astype(jnp.float32), axis=0, keepdims=True)

    def max_below(v):
        def step(c, acc):
            return jnp.maximum(acc, by_lane_tile(
                sc_ref, c, lambda s, ln: fold(jnp.where(s < v[:, ln], s, -inf), jnp.maximum)))
        acc = pairwise(step, nch, jnp.full((8, lanes), -inf, jnp.float32))
        return jnp.max(acc, axis=0, keepdims=True)

    def n_open(done):
        return jnp.sum(1 - done)

    def snap(state):
        lo, hi, thr, cge, done = state
        b = max_below(hi)
        c = count_ge(b)
        ok = c >= TOPK
        newly = ok & (done == 0)
        return (lo, jnp.where(ok, hi, b), jnp.where(newly, b, thr),
                jnp.where(newly, c, cge), jnp.where(ok, 1, done))

    def bisect(state):
        lo, hi, thr, cge, done = state
        mid = 0.5 * lo + 0.5 * hi
        ge = count_ge(mid) >= TOPK
        return (jnp.where(ge, mid, lo), jnp.where(ge, hi, mid), thr, cge, done)

    def bisect16(_, lo_hi):
        lo, hi = lo_hi
        mid = (0.5 * lo + 0.5 * hi).astype(jnp.bfloat16).astype(jnp.float32)
        ge = count_ge16(mid) >= TOPK
        return jnp.where(ge, mid, lo), jnp.where(ge, hi, mid)

    short = jnp.where((t_row + 1) <= TOPK, 1, 0)
    n0 = n_open(short)
    run = n0 > 0
    lo16 = lo.astype(jnp.bfloat16).astype(jnp.float32)
    hi16 = hi.astype(jnp.bfloat16).astype(jnp.float32)
    hi16 = (hi16 + jnp.abs(hi16) * (2 * BF16_STEP) + TINY).astype(jnp.bfloat16).astype(jnp.float32)
    lo16, hi16 = lax.fori_loop(0, jnp.where(run, BISECT16_STEPS, 0), bisect16, (lo16, hi16))
    lo32 = jnp.maximum(lo, lo16 - jnp.abs(lo16) * BF16_STEP - TINY)
    state = (lo32, hi16, lo, jnp.zeros((1, lanes), jnp.float32), short)
    state = lax.fori_loop(0, jnp.where(run, BISECT_STEPS, 0), lambda _, s: bisect(s), state)
    state = lax.cond(run, snap, lambda s: s, state)

    def finish_body(carry):
        s = snap(bisect(carry[0]))
        return s, n_open(s[4])

    state, _ = lax.while_loop(lambda carry: carry[1] > 0, finish_body, (state, n_open(state[4])))
    thr, cge = state[2], state[3]

    @pl.when(jnp.max(cge) > TOPK)
    def _():
        def gt_body(c, acc):
            off = pl.multiple_of(c * kc, kc)
            return acc + fold(jnp.where(sc_ref[pl.ds(off, kc), :] > thr, 1.0, 0.0), jnp.add)
        cnt_gt = jnp.sum(lax.fori_loop(0, nch, gt_body, jnp.zeros((8, lanes), jnp.float32)),
                         axis=0, keepdims=True)
        need = TOPK - cnt_gt
        r_i = lax.broadcasted_iota(jnp.int32, (kc, kc), 0)
        c_i = lax.broadcasted_iota(jnp.int32, (kc, kc), 1)
        tri = jnp.where(c_i <= r_i, 1.0, 0.0).astype(jnp.bfloat16)

        def tie_body(c, seen):
            off = pl.multiple_of(c * kc, kc)
            s = sc_ref[pl.ds(off, kc), :]
            eq = s == thr
            eq_f = jnp.where(eq, 1.0, 0.0)
            incl = jnp.dot(tri, eq_f.astype(jnp.bfloat16), preferred_element_type=jnp.float32)
            rank = incl - eq_f + seen
            sc_ref[pl.ds(off, kc), :] = jnp.where(eq & (rank >= need), -inf, s)
            return seen + incl[kc - 1:kc, :]

        lax.fori_loop(0, nch, tie_body, jnp.zeros((1, lanes), jnp.float32))

    acc_t[...] = jnp.zeros_like(acc_t)

    def select_bias(c):
        off = pl.multiple_of(c * kc, kc)
        return jnp.where(sc_ref[pl.ds(off, kc), :] >= thr, 0.0, NEG_BIG)

    def masked_logits(c, buf, t, bias):
        g = t // N_HEADS
        off = pl.multiple_of(c * kc, kc)
        lm = jnp.dot(kaug_ref[g, pl.ds(off, kc), :], rhs_att[:, cols(t)],
                     preferred_element_type=jnp.float32) + bias[:, cols(g)]
        buf[:, cols(t)] = lm
        return jnp.max(fold(lm, jnp.maximum), axis=0, keepdims=True)

    def softmax_pv(c, buf, t, m_old, cmax, m_list):
        m_new = jnp.maximum(m_old[t], cmax[t])
        alpha = jnp.exp2(m_old[t] - m_new)
        p = jnp.exp2((buf[:, cols(t)] - m_new).astype(jnp.bfloat16))
        pv = jnp.dot(vt_ref[t // N_HEADS, c], p, preferred_element_type=jnp.float32)
        acc_t[:, cols(t)] = acc_t[:, cols(t)] * alpha + pv
        return (m_list or ()) + (m_new,)

    pipelined(masked_logits, softmax_pv, lambda c, m_old, m_list: m_list,
              tuple(jnp.full((1, nq), NEG_BIG, jnp.float32) for _ in range(tiles)),
              prepare=select_bias)

    for t in range(tiles):
        g, hd = divmod(t, N_HEADS)
        inv_l = 1.0 / acc_t[HEAD_DIM:HEAD_DIM + 1, cols(t)]
        at_ref[g, hd * HEAD_DIM:(hd + 1) * HEAD_DIM, :] = (
            acc_t[0:HEAD_DIM, cols(t)] * inv_l).astype(at_ref.dtype)


def _attn(qt, qit, wit, vt, kaug, kiaug, qaug, *, batch, seq):
    nq = Q_BLOCK
    assert KEY_CHUNK == Q_BLOCK and seq % nq == 0 and batch % GROUP == 0
    wide = GROUP * N_HEADS * nq
    qblk = lambda b, j: (b, 0, j)
    whole = lambda b, j: (b, 0, 0)
    return pl.pallas_call(
        _attn_kernel,
        grid=(batch // GROUP, seq // nq),
        in_specs=[
            pl.BlockSpec((GROUP, ATTN_WIDTH, nq), qblk),
            pl.BlockSpec((GROUP, IDX_HEADS * IDX_DIM, nq), qblk),
            pl.BlockSpec((GROUP, IDX_HEADS, nq), qblk),
            pl.BlockSpec((GROUP, seq // KEY_CHUNK, VT_ROWS, KEY_CHUNK),
                         lambda b, j: (b, 0, 0, 0)),
            pl.BlockSpec((GROUP, seq, AUG), whole),
            pl.BlockSpec((GROUP, seq, AUG), whole),
            pl.BlockSpec((AUG - HEAD_DIM, N_HEADS * nq), lambda b, j: (0, 0)),
        ],
        out_specs=pl.BlockSpec((GROUP, ATTN_WIDTH, nq), qblk),
        out_shape=jax.ShapeDtypeStruct((batch, ATTN_WIDTH, seq), jnp.bfloat16),
        scratch_shapes=[
            pltpu.VMEM((AUG, wide), jnp.bfloat16),
            pltpu.VMEM((AUG, wide), jnp.bfloat16),
            pltpu.VMEM((seq, GROUP * nq), jnp.float32),
            pltpu.VMEM((seq, GROUP * nq), jnp.bfloat16),
            pltpu.VMEM((VT_ROWS, wide), jnp.float32),
            pltpu.VMEM((KEY_CHUNK, wide), jnp.float32),
            pltpu.VMEM((KEY_CHUNK, wide), jnp.float32),
        ],
        compiler_params=pltpu.CompilerParams(
            dimension_semantics=("arbitrary", "arbitrary"), vmem_limit_bytes=VMEM_LIMIT),
        name="attn",
    )(qt, qit, wit, vt, kaug, kiaug, qaug)


def _mix_kernel(x_ref, at_ref, u_ref, halo_ref, g_ref, wpg_ref, pscale_ref, wba_ref, wbp_ref,
                wout_ref, gffn_ref, wg_ref, wu_ref, wd_ref, o_ref, ubuf, *, tm, seq):
    i = pl.program_id(0)
    tiles_per_seq = seq // tm
    first = (i % tiles_per_seq) == 0
    halo = halo_ref[...]
    ubuf[0:HALO, :] = jnp.where(first, jnp.zeros_like(halo), halo)
    ubuf[HALO:HALO + tm, :] = u_ref[...]

    t_pos = (i * tm) % seq + lax.broadcasted_iota(jnp.int32, (tm, POOL_GROUP_WIDTH), 0)
    pooled = []
    for g, w in enumerate(POOL_WINDOWS):
        lanes = slice(g * POOL_GROUP_WIDTH, (g + 1) * POOL_GROUP_WIDTH)
        levels = w.bit_length() - 1
        cur = ubuf[:, lanes]
        step = 1
        for _ in range(levels):
            cur = cur + pltpu.roll(cur, step, axis=0)
            step *= 2
        wsum = cur[HALO:, :]
        cnt = jnp.minimum(t_pos + 1, w).astype(jnp.float32)
        d = (wsum / cnt - ubuf[HALO:HALO + tm, lanes]).astype(jnp.bfloat16)
        pooled.append(jnp.dot(d, wpg_ref[g], preferred_element_type=jnp.float32))
    p = (jnp.concatenate(pooled, axis=-1) * pscale_ref[...]).astype(jnp.bfloat16)

    a_proj = lax.dot_general(at_ref[0], wba_ref[...], (((0,), (0,)), ((), ())),
                             preferred_element_type=jnp.float32)
    p_proj = jnp.dot(p, wbp_ref[...], preferred_element_type=jnp.float32)
    gates = g_ref[...].astype(jnp.float32)
    merged = (jax.nn.sigmoid(gates[:, :D_MODEL]) * a_proj
              + jax.nn.sigmoid(gates[:, D_MODEL:]) * p_proj)
    x1 = x_ref[...] + jnp.dot(merged.astype(jnp.bfloat16), wout_ref[...],
                              preferred_element_type=jnp.float32)

    h2 = (_rms(x1) * gffn_ref[...]).astype(jnp.bfloat16)
    gate = jnp.dot(h2, wg_ref[...], preferred_element_type=jnp.float32)
    up = jnp.dot(h2, wu_ref[...], preferred_element_type=jnp.float32)
    act = (jax.nn.silu(gate) * up).astype(jnp.bfloat16)
    o_ref[...] = x1 + jnp.dot(act, wd_ref[...], preferred_element_type=jnp.float32)


def _mix(x2, at, u, gates, wpg, pscale, wba, wbp, wout, gffn, wg, wu, wd, *, batch, seq, tm):
    n = batch * seq
    tiles_per_seq = seq // tm
    row = lambda i: (i, 0)
    const2 = lambda i: (0, 0)
    resident = functools.partial(pl.BlockSpec, pipeline_mode=pl.Buffered(1))
    return pl.pallas_call(
        functools.partial(_mix_kernel, tm=tm, seq=seq),
        grid=(n // tm,),
        in_specs=[
            pl.BlockSpec((tm, D_MODEL), row),
            pl.BlockSpec((1, ATTN_WIDTH, tm), lambda i: (i // tiles_per_seq, 0, i % tiles_per_seq)),
            pl.BlockSpec((tm, POOL_WIDTH), row),
            pl.BlockSpec((HALO, POOL_WIDTH), lambda i: (jnp.maximum(i * (tm // HALO) - 1, 0), 0)),
            pl.BlockSpec((tm, 2 * D_MODEL), row),
            resident((len(POOL_WINDOWS), POOL_GROUP_WIDTH, POOL_GROUP_WIDTH), lambda i: (0, 0, 0)),
            resident((1, POOL_WIDTH), const2),
            resident((ATTN_WIDTH, D_MODEL), const2),
            resident((POOL_WIDTH, D_MODEL), const2),
            resident((D_MODEL, D_MODEL), const2),
            resident((1, D_MODEL), const2),
            resident((D_MODEL, D_FF), const2),
            resident((D_MODEL, D_FF), const2),
            resident((D_FF, D_MODEL), const2),
        ],
        out_specs=pl.BlockSpec((tm, D_MODEL), row),
        out_shape=jax.ShapeDtypeStruct((n, D_MODEL), jnp.float32),
        scratch_shapes=[pltpu.VMEM((tm + HALO, POOL_WIDTH), jnp.float32)],
        compiler_params=pltpu.CompilerParams(
            dimension_semantics=("arbitrary",), vmem_limit_bytes=VMEM_LIMIT),
        name="mix",
    )(x2, at, u, u, gates, wpg, pscale, wba, wbp, wout, gffn, wg, wu, wd)


def _alibi_rows():
    slopes = 2.0 ** (-8.0 * jnp.arange(1, N_HEADS + 1, dtype=jnp.float32) / N_HEADS) * LOG2E
    parts, rest = [], slopes
    for _ in range(SLOPE_PARTS):
        term = rest.astype(jnp.bfloat16)
        parts.append(term)
        rest = rest - term.astype(jnp.float32)
    block = jnp.stack(parts * 2, axis=0)
    block = jnp.repeat(block, Q_BLOCK, axis=1)
    pad = jnp.zeros((AUG - HEAD_DIM - 2 * SLOPE_PARTS, N_HEADS * Q_BLOCK), jnp.bfloat16)
    return jnp.concatenate([block, pad], axis=0)


def _layer(x2, norm_mix, w_in, q_norm, k_norm, w_pool_group, pool_scale, w_branch_attn,
           w_branch_pool, w_out, norm_ffn, w_ffn_gate, w_ffn_up, w_ffn_down, *, batch, seq):
    bf = jnp.bfloat16
    w16 = w_in.astype(bf)
    pad = jnp.zeros((D_MODEL, T_ROWS - T_WI0 - IDX_HEADS), bf)
    wt = jnp.concatenate([w16[:, 0:512], w16[:, 576:1152], w16[:, 1216:1224], pad], axis=1).T
    wn = jnp.concatenate([w16[:, 512:576], w16[:, 1152:1216], w16[:, 1224:3784]], axis=1)
    gk = jnp.concatenate([k_norm, jnp.zeros((LANES - HEAD_DIM,), k_norm.dtype)])[None, :]

    qt, qit, wit, vt, kaug, kiaug, u, gates = _proj(
        x2, norm_mix[None, :], wt, wn, q_norm[:, None], gk, batch=batch, seq=seq, tm=PROJ_ROWS)
    at = _attn(qt, qit, wit, vt, kaug, kiaug, _alibi_rows(), batch=batch, seq=seq)
    return _mix(x2, at, u, gates, w_pool_group.astype(bf), pool_scale[None, :],
                w_branch_attn.astype(bf), w_branch_pool.astype(bf), w_out.astype(bf),
                norm_ffn[None, :], w_ffn_gate.astype(bf), w_ffn_up.astype(bf),
                w_ffn_down.astype(bf), batch=batch, seq=seq, tm=MIX_ROWS)


def kernel(x, norm_mix, w_in, q_norm, k_norm, w_pool_group, pool_scale, w_branch_attn,
           w_branch_pool, w_out, norm_ffn, w_ffn_gate, w_ffn_up, w_ffn_down):
    batch, seq, d = x.shape
    x2 = x.reshape(batch * seq, d)
    for l in range(norm_mix.shape[0]):
        x2 = _layer(x2, norm_mix[l], w_in[l], q_norm[l], k_norm[l], w_pool_group[l],
                    pool_scale[l], w_branch_attn[l], w_branch_pool[l], w_out[l], norm_ffn[l],
                    w_ffn_gate[l], w_ffn_up[l], w_ffn_down[l], batch=batch, seq=seq)
    return x2.reshape(batch, seq, d)
```

```python
import functools

import jax
import jax.numpy as jnp
from jax import lax
from jax.experimental import pallas as pl
from jax.experimental.pallas import tpu as pltpu

D_MODEL = 1024
N_HEADS = 8
HEAD_DIM = 64
ATTN_WIDTH = N_HEADS * HEAD_DIM
IDX_HEADS = 8
IDX_DIM = 64
IDX_SCALE = (IDX_HEADS ** -0.5) * (IDX_DIM ** -0.5)
TOPK = 256
POOL_WINDOWS = (2, 4, 8, 16)
POOL_WIDTH = 512
POOL_GROUP_WIDTH = 128
D_FF = 2816
RMS_EPS = 1e-6
NEG_BIG = -1e30

LANES = 128
PROJ_ROWS = 1024
MIX_ROWS = 512
Q_BLOCK = 256
GROUP = 4
KEY_CHUNK = 256
AUG = 128
VT_ROWS = 80
HALO = 16
LOG2E = 1.4426950408889634
SLOPE_PARTS = 3
PACK_ROWS = 16
BF16_STEP = 2.0 ** -7
BF16_EXACT_INT = 256
BISECT16_STEPS = 10
BISECT_STEPS = 6
TINY = 1e-30
VMEM_LIMIT = 56 * 1024 * 1024

T_Q0, T_V0, T_QI0, T_WI0, T_ROWS = 0, 512, 576, 1088, 1104
N_KK0, N_U0, N_G0, N_COLS = 0, 128, 640, 2688


def _rms(x, eps=RMS_EPS):
    return x * lax.rsqrt(jnp.mean(x * x, axis=-1, keepdims=True) + eps)


def _regroup_w_in(w_ref, wt_ref, wn_ref):
    bf = jnp.bfloat16
    wt_ref[T_Q0:T_Q0 + ATTN_WIDTH, :] = w_ref[:, 0:512].T.astype(bf)
    kv_qi = w_ref[:, 512:1152].T
    wt_ref[T_V0:T_WI0, :] = kv_qi[HEAD_DIM:, :].astype(bf)
    ki_wi = w_ref[:, 1152:1280].T
    tail = jnp.concatenate(
        [ki_wi[IDX_DIM:IDX_DIM + IDX_HEADS, :],
         jnp.zeros((T_ROWS - T_WI0 - IDX_HEADS, D_MODEL), jnp.float32)], axis=0)
    wt_ref[T_WI0:T_ROWS, :] = tail.astype(bf)
    wn_ref[:, N_KK0:N_KK0 + LANES] = jnp.concatenate(
        [w_ref[:, 512:576], w_ref[:, 1152:1216]], axis=1).astype(bf)
    wn_ref[:, N_U0:N_COLS] = w_ref[:, 1224:3784].astype(bf)


def _proj_kernel(x_ref, gmix_ref, w_ref, gq_ref, gk_ref,
                 qt_ref, qit_ref, wit_ref, vt_ref, kaug_ref, kiaug_ref, u_ref, g_ref,
                 wt_ref, wn_ref, *, tm, seq):
    i = pl.program_id(0)

    @pl.when(i == 0)
    def _():
        _regroup_w_in(w_ref, wt_ref, wn_ref)

    x = x_ref[...]
    h = (_rms(x) * gmix_ref[...]).astype(jnp.bfloat16)
    y_t = lax.dot_general(wt_ref[...], h, (((1,), (1,)), ((), ())),
                          preferred_element_type=jnp.float32)
    y_n = jnp.dot(h, wn_ref[...], preferred_element_type=jnp.float32)

    gq = gq_ref[...]
    for hd in range(N_HEADS):
        blk = y_t[T_Q0 + hd * HEAD_DIM:T_Q0 + (hd + 1) * HEAD_DIM, :]
        ss = jnp.mean(blk * blk, axis=0, keepdims=True)
        qn = blk * lax.rsqrt(ss + RMS_EPS) * gq * (HEAD_DIM ** -0.5 * LOG2E)
        qt_ref[0, hd * HEAD_DIM:(hd + 1) * HEAD_DIM, :] = qn.astype(qt_ref.dtype)

    qit_ref[0] = y_t[T_QI0:T_QI0 + IDX_HEADS * IDX_DIM, :].astype(qit_ref.dtype)
    wit_ref[0] = y_t[T_WI0:T_WI0 + IDX_HEADS, :] * IDX_SCALE

    v_t = y_t[T_V0:T_V0 + HEAD_DIM, :]
    row = lax.broadcasted_iota(jnp.int32, (VT_ROWS - HEAD_DIM, tm), 0)
    tail = jnp.where(row == 0, 1.0, 0.0)
    vt_full = jnp.concatenate([v_t, tail], axis=0).astype(vt_ref.dtype)
    for cc in range(tm // KEY_CHUNK):
        vt_ref[0, cc] = vt_full[:, cc * KEY_CHUNK:(cc + 1) * KEY_CHUNK]

    kk = y_n[:, N_KK0:N_KK0 + LANES]
    lane = lax.broadcasted_iota(jnp.int32, (tm, LANES), 1)
    is_k = lane < HEAD_DIM
    ss = jnp.sum(jnp.where(is_k, kk * kk, 0.0), axis=-1, keepdims=True) * (1.0 / HEAD_DIM)
    kn = kk * lax.rsqrt(ss + RMS_EPS) * gk_ref[...]
    pos = (i * tm) % seq + lax.broadcasted_iota(jnp.int32, (tm, LANES), 0)
    pos_lo = pos & (BF16_EXACT_INT - 1)
    pos_hi = (pos - pos_lo).astype(jnp.float32)
    pos_lo = pos_lo.astype(jnp.float32)
    feats = jnp.where(lane < HEAD_DIM + SLOPE_PARTS, pos_hi,
                      jnp.where(lane < HEAD_DIM + 2 * SLOPE_PARTS, pos_lo, 0.0))
    kaug_ref[0] = jnp.where(is_k, kn, feats).astype(kaug_ref.dtype)
    kiaug_ref[0] = jnp.where(is_k, 0.0, kk).astype(kiaug_ref.dtype)

    u_ref[...] = y_n[:, N_U0:N_U0 + POOL_WIDTH]
    g_ref[...] = y_n[:, N_G0:N_G0 + 2 * D_MODEL].astype(g_ref.dtype)


def _proj(x2, gmix, w_in, gq, gk, *, batch, seq, tm):
    n = batch * seq
    tiles_per_seq = seq // tm
    const = lambda i: (0, 0)
    bidx = lambda i: (i // tiles_per_seq, 0, i % tiles_per_seq)
    out_shape = (
        jax.ShapeDtypeStruct((batch, ATTN_WIDTH, seq), jnp.bfloat16),
        jax.ShapeDtypeStruct((batch, IDX_HEADS * IDX_DIM, seq), jnp.bfloat16),
        jax.ShapeDtypeStruct((batch, IDX_HEADS, seq), jnp.float32),
        jax.ShapeDtypeStruct((batch, seq // KEY_CHUNK, VT_ROWS, KEY_CHUNK), jnp.bfloat16),
        jax.ShapeDtypeStruct((batch, seq, AUG), jnp.bfloat16),
        jax.ShapeDtypeStruct((batch, seq, AUG), jnp.bfloat16),
        jax.ShapeDtypeStruct((n, POOL_WIDTH), jnp.float32),
        jax.ShapeDtypeStruct((n, 2 * D_MODEL), jnp.bfloat16),
    )
    out_specs = (
        pl.BlockSpec((1, ATTN_WIDTH, tm), bidx),
        pl.BlockSpec((1, IDX_HEADS * IDX_DIM, tm), bidx),
        pl.BlockSpec((1, IDX_HEADS, tm), bidx),
        pl.BlockSpec((1, tm // KEY_CHUNK, VT_ROWS, KEY_CHUNK),
                     lambda i: (i // tiles_per_seq, i % tiles_per_seq, 0, 0)),
        pl.BlockSpec((1, tm, AUG), lambda i: (i // tiles_per_seq, i % tiles_per_seq, 0)),
        pl.BlockSpec((1, tm, AUG), lambda i: (i // tiles_per_seq, i % tiles_per_seq, 0)),
        pl.BlockSpec((tm, POOL_WIDTH), lambda i: (i, 0)),
        pl.BlockSpec((tm, 2 * D_MODEL), lambda i: (i, 0)),
    )
    in_specs = [
        pl.BlockSpec((tm, D_MODEL), lambda i: (i, 0)),
        pl.BlockSpec((1, D_MODEL), const),
        pl.BlockSpec(w_in.shape, const, pipeline_mode=pl.Buffered(1)),
        pl.BlockSpec((HEAD_DIM, 1), const),
        pl.BlockSpec((1, LANES), const),
    ]
    return pl.pallas_call(
        functools.partial(_proj_kernel, tm=tm, seq=seq),
        grid=(n // tm,),
        in_specs=in_specs,
        out_specs=out_specs,
        out_shape=out_shape,
        scratch_shapes=[
            pltpu.VMEM((T_ROWS, D_MODEL), jnp.bfloat16),
            pltpu.VMEM((D_MODEL, N_COLS), jnp.bfloat16),
        ],
        compiler_params=pltpu.CompilerParams(
            dimension_semantics=("arbitrary",), vmem_limit_bytes=VMEM_LIMIT),
        name="proj",
    )(x2, gmix, w_in, gq, gk)


def _attn_kernel(qt_ref, qit_ref, wit_ref, vt_ref, kaug_ref, kiaug_ref, qaug_ref,
                 at_ref, rhs_idx, rhs_att, sc_ref, sc16_ref, acc_t, buf_a, buf_b):
    j = pl.program_id(1)
    nq = Q_BLOCK
    kc = KEY_CHUNK
    nch = j + 1
    lanes = GROUP * nq
    tiles = GROUP * N_HEADS
    inf = jnp.float32(jnp.inf)

    def cols(t):
        return slice(t * nq, (t + 1) * nq)

    zeros64 = jnp.zeros((HEAD_DIM, nq), rhs_idx.dtype)
    for t in range(tiles):
        g, hd = divmod(t, N_HEADS)
        rows = slice(hd * HEAD_DIM, (hd + 1) * HEAD_DIM)
        rhs_idx[0:HEAD_DIM, cols(t)] = zeros64
        rhs_idx[HEAD_DIM:AUG, cols(t)] = qit_ref[g, rows, :]
        rhs_att[0:HEAD_DIM, cols(t)] = qt_ref[g, rows, :]
        rhs_att[HEAD_DIM:AUG, cols(t)] = qaug_ref[:, cols(hd)]

    t_row = j * nq + lax.broadcasted_iota(jnp.int32, (1, lanes), 1) % nq
    t_pos = j * nq + lax.broadcasted_iota(jnp.int32, (kc, nq), 1)
    s_loc = lax.broadcasted_iota(jnp.int32, (kc, nq), 0)

    def pairwise(step, n, carry):
        carry = lax.fori_loop(0, n // 2, lambda i, cr: step(2 * i + 1, step(2 * i, cr)), carry)
        return lax.cond(n % 2 == 1, lambda cr: step(n - 1, cr), lambda cr: cr, carry)

    def fold(x, op, rows=8):
        parts = [x[r * rows:(r + 1) * rows, :] for r in range(kc // rows)]
        while len(parts) > 1:
            parts = [op(parts[i], parts[i + 1]) for i in range(0, len(parts), 2)]
        return parts[0]

    bufs = (buf_a, buf_b)

    def pipelined(produce, consume, finish, carry, prepare=lambda c: None):
        def drain(c, par, carry, aux):
            part = None
            for t in range(tiles):
                part = consume(c, bufs[par], t, carry, aux, part)
            return finish(c, carry, part)

        def both(c, par, carry, aux):
            part, nxt, ctx = None, [], prepare(c + 1)
            for t in range(tiles):
                nxt.append(produce(c + 1, bufs[1 - par], t, ctx))
                part = consume(c, bufs[par], t, carry, aux, part)
            return finish(c, carry, part), tuple(nxt)

        def pair_body(i, st):
            return both(2 * i + 1, 1, *both(2 * i, 0, *st))

        ctx0 = prepare(0)
        first = tuple(produce(0, bufs[0], t, ctx0) for t in range(tiles))
        st = lax.fori_loop(0, (nch - 1) // 2, pair_body, (carry, first))

        def even_tail(st):
            return drain(nch - 1, 1, *both(nch - 2, 0, *st))

        return lax.cond(nch % 2 == 0, even_tail, lambda st: drain(nch - 1, 0, *st), st)

    def score_mm(c, buf, t, _):
        off = pl.multiple_of(c * kc, kc)
        buf[:, cols(t)] = jnp.dot(kiaug_ref[t // N_HEADS, pl.ds(off, kc), :], rhs_idx[:, cols(t)],
                                  preferred_element_type=jnp.float32)
        return jnp.int32(0)

    def score_head(c, buf, t, lo_hi, _, tots):
        g, hd = divmod(t, N_HEADS)
        term = jnp.maximum(buf[:, cols(t)], 0.0) * wit_ref[g, hd:hd + 1, :]
        tots = list(tots or [None] * GROUP)
        tots[g] = term if tots[g] is None else tots[g] + term
        return tuple(tots)

    def score_store(c, lo_hi, tots):
        off = pl.multiple_of(c * kc, kc)
        causal = off + s_loc <= t_pos
        los, his = [], []
        for g in range(GROUP):
            masked = jnp.where(causal, tots[g], -inf)
            sc_ref[pl.ds(off, kc), cols(g)] = masked
            sc16_ref[pl.ds(off, kc), cols(g)] = masked.astype(jnp.bfloat16)
            los.append(fold(jnp.where(causal, tots[g], inf), jnp.minimum))
            his.append(fold(masked, jnp.maximum))
        lo, hi = lo_hi
        return (jnp.minimum(lo, jnp.concatenate(los, axis=1)),
                jnp.maximum(hi, jnp.concatenate(his, axis=1)))

    lo, hi = pipelined(score_mm, score_head, score_store,
                       (jnp.full((8, lanes), inf, jnp.float32),
                        jnp.full((8, lanes), -inf, jnp.float32)))
    lo = jnp.min(lo, axis=0, keepdims=True)
    hi = jnp.max(hi, axis=0, keepdims=True)

    def by_lane_tile(ref, c, fn):
        off = pl.multiple_of(c * kc, kc)
        return jnp.concatenate(
            [fn(ref[pl.ds(off, kc), l * LANES:(l + 1) * LANES], slice(l * LANES, (l + 1) * LANES))
             for l in range(lanes // LANES)], axis=1)

    def count_ge(v):
        def step(c, acc):
            return acc + by_lane_tile(
                sc_ref, c, lambda s, ln: fold(jnp.where(s >= v[:, ln], 1.0, 0.0), jnp.add))
        acc = pairwise(step, nch, jnp.zeros((8, lanes), jnp.float32))
        return jnp.sum(acc, axis=0, keepdims=True)

    def count_ge16(v):
        v16 = v.astype(jnp.bfloat16)
        one, zero = jnp.ones((), jnp.bfloat16), jnp.zeros((), jnp.bfloat16)
        def step(c, acc):
            return acc + by_lane_tile(
                sc16_ref, c,
                lambda s, ln: fold(jnp.where(s >= v16[:, ln], one, zero), jnp.add, PACK_ROWS))
        acc = pairwise(step, nch, jnp.zeros((PACK_ROWS, lanes), jnp.bfloat16))
        return jnp.sum(acc.astype(jnp.float32), axis=0, keepdims=True)

    def max_below(v):
        def step(c, acc):
            return jnp.maximum(acc, by_lane_tile(
                sc_ref, c, lambda s, ln: fold(jnp.where(s < v[:, ln], s, -inf), jnp.maximum)))
        acc = pairwise(step, nch, jnp.full((8, lanes), -inf, jnp.float32))
        return jnp.max(acc, axis=0, keepdims=True)

    def n_open(done):
        return jnp.sum(1 - done)

    def snap(state):
        lo, hi, thr, cge, done = state
        b = max_below(hi)
        c = count_ge(b)
        ok = c >= TOPK
        newly = ok & (done == 0)
        return (lo, jnp.where(ok, hi, b), jnp.where(newly, b, thr),
                jnp.where(newly, c, cge), jnp.where(ok, 1, done))

    def bisect(state):
        lo, hi, thr, cge, done = state
        mid = 0.5 * lo + 0.5 * hi
        ge = count_ge(mid) >= TOPK
        return (jnp.where(ge, mid, lo), jnp.where(ge, hi, mid), thr, cge, done)

    def bisect16(_, lo_hi):
        lo, hi = lo_hi
        mid = (0.5 * lo + 0.5 * hi).astype(jnp.bfloat16).astype(jnp.float32)
        ge = count_ge16(mid) >= TOPK
        return jnp.where(ge, mid, lo), jnp.where(ge, hi, mid)

    short = jnp.where((t_row + 1) <= TOPK, 1, 0)
    n0 = n_open(short)
    run = n0 > 0
    lo16 = lo.astype(jnp.bfloat16).astype(jnp.float32)
    hi16 = hi.astype(jnp.bfloat16).astype(jnp.float32)
    hi16 = (hi16 + jnp.abs(hi16) * (2 * BF16_STEP) + TINY).astype(jnp.bfloat16).astype(jnp.float32)
    lo16, hi16 = lax.fori_loop(0, jnp.where(run, BISECT16_STEPS, 0), bisect16, (lo16, hi16))
    lo32 = jnp.maximum(lo, lo16 - jnp.abs(lo16) * BF16_STEP - TINY)
    state = (lo32, hi16, lo, jnp.zeros((1, lanes), jnp.float32), short)
    state = lax.fori_loop(0, jnp.where(run, BISECT_STEPS, 0), lambda _, s: bisect(s), state)
    state = lax.cond(run, snap, lambda s: s, state)

    def finish_body(carry):
        s = snap(bisect(carry[0]))
        return s, n_open(s[4])

    state, _ = lax.while_loop(lambda carry: carry[1] > 0, finish_body, (state, n_open(state[4])))
    thr, cge = state[2], state[3]

    @pl.when(jnp.max(cge) > TOPK)
    def _():
        def gt_body(c, acc):
            off = pl.multiple_of(c * kc, kc)
            return acc + fold(jnp.where(sc_ref[pl.ds(off, kc), :] > thr, 1.0, 0.0), jnp.add)
        cnt_gt = jnp.sum(lax.fori_loop(0, nch, gt_body, jnp.zeros((8, lanes), jnp.float32)),
                         axis=0, keepdims=True)
        need = TOPK - cnt_gt
        r_i = lax.broadcasted_iota(jnp.int32, (kc, kc), 0)
        c_i = lax.broadcasted_iota(jnp.int32, (kc, kc), 1)
        tri = jnp.where(c_i <= r_i, 1.0, 0.0).astype(jnp.bfloat16)

        def tie_body(c, seen):
            off = pl.multiple_of(c * kc, kc)
            s = sc_ref[pl.ds(off, kc), :]
            eq = s == thr
            eq_f = jnp.where(eq, 1.0, 0.0)
            incl = jnp.dot(tri, eq_f.astype(jnp.bfloat16), preferred_element_type=jnp.float32)
            rank = incl - eq_f + seen
            sc_ref[pl.ds(off, kc), :] = jnp.where(eq & (rank >= need), -inf, s)
            return seen + incl[kc - 1:kc, :]

        lax.fori_loop(0, nch, tie_body, jnp.zeros((1, lanes), jnp.float32))

    acc_t[...] = jnp.zeros_like(acc_t)

    def select_bias(c):
        off = pl.multiple_of(c * kc, kc)
        return jnp.where(sc_ref[pl.ds(off, kc), :] >= thr, 0.0, NEG_BIG)

    def masked_logits(c, buf, t, bias):
        g = t // N_HEADS
        off = pl.multiple_of(c * kc, kc)
        lm = jnp.dot(kaug_ref[g, pl.ds(off, kc), :], rhs_att[:, cols(t)],
                     preferred_element_type=jnp.float32) + bias[:, cols(g)]
        buf[:, cols(t)] = lm
        return jnp.max(fold(lm, jnp.maximum), axis=0, keepdims=True)

    def softmax_pv(c, buf, t, m_old, cmax, m_list):
        m_new = jnp.maximum(m_old[t], cmax[t])
        alpha = jnp.exp2(m_old[t] - m_new)
        p = jnp.exp2((buf[:, cols(t)] - m_new).astype(jnp.bfloat16))
        pv = jnp.dot(vt_ref[t // N_HEADS, c], p, preferred_element_type=jnp.float32)
        acc_t[:, cols(t)] = acc_t[:, cols(t)] * alpha + pv
        return (m_list or ()) + (m_new,)

    pipelined(masked_logits, softmax_pv, lambda c, m_old, m_list: m_list,
              tuple(jnp.full((1, nq), NEG_BIG, jnp.float32) for _ in range(tiles)),
              prepare=select_bias)

    for t in range(tiles):
        g, hd = divmod(t, N_HEADS)
        inv_l = 1.0 / acc_t[HEAD_DIM:HEAD_DIM + 1, cols(t)]
        at_ref[g, hd * HEAD_DIM:(hd + 1) * HEAD_DIM, :] = (
            acc_t[0:HEAD_DIM, cols(t)] * inv_l).astype(at_ref.dtype)


def _attn(qt, qit, wit, vt, kaug, kiaug, qaug, *, batch, seq):
    nq = Q_BLOCK
    assert KEY_CHUNK == Q_BLOCK and seq % nq == 0 and batch % GROUP == 0
    wide = GROUP * N_HEADS * nq
    qblk = lambda b, j: (b, 0, j)
    whole = lambda b, j: (b, 0, 0)
    return pl.pallas_call(
        _attn_kernel,
        grid=(batch // GROUP, seq // nq),
        in_specs=[
            pl.BlockSpec((GROUP, ATTN_WIDTH, nq), qblk),
            pl.BlockSpec((GROUP, IDX_HEADS * IDX_DIM, nq), qblk),
            pl.BlockSpec((GROUP, IDX_HEADS, nq), qblk),
            pl.BlockSpec((GROUP, seq // KEY_CHUNK, VT_ROWS, KEY_CHUNK),
                         lambda b, j: (b, 0, 0, 0)),
            pl.BlockSpec((GROUP, seq, AUG), whole),
            pl.BlockSpec((GROUP, seq, AUG), whole),
            pl.BlockSpec((AUG - HEAD_DIM, N_HEADS * nq), lambda b, j: (0, 0)),
        ],
        out_specs=pl.BlockSpec((GROUP, ATTN_WIDTH, nq), qblk),
        out_shape=jax.ShapeDtypeStruct((batch, ATTN_WIDTH, seq), jnp.bfloat16),
        scratch_shapes=[
            pltpu.VMEM((AUG, wide), jnp.bfloat16),
            pltpu.VMEM((AUG, wide), jnp.bfloat16),
            pltpu.VMEM((seq, GROUP * nq), jnp.float32),
            pltpu.VMEM((seq, GROUP * nq), jnp.bfloat16),
            pltpu.VMEM((VT_ROWS, wide), jnp.float32),
            pltpu.VMEM((KEY_CHUNK, wide), jnp.float32),
            pltpu.VMEM((KEY_CHUNK, wide), jnp.float32),
        ],
        compiler_params=pltpu.CompilerParams(
            dimension_semantics=("arbitrary", "arbitrary"), vmem_limit_bytes=VMEM_LIMIT),
        name="attn",
    )(qt, qit, wit, vt, kaug, kiaug, qaug)


def _mix_kernel(x_ref, at_ref, u_ref, halo_ref, g_ref, wpg_ref, pscale_ref, wba_ref, wbp_ref,
                wout_ref, gffn_ref, wg_ref, wu_ref, wd_ref, o_ref, ubuf, *, tm, seq):
    i = pl.program_id(0)
    tiles_per_seq = seq // tm
    first = (i % tiles_per_seq) == 0
    halo = halo_ref[...]
    ubuf[0:HALO, :] = jnp.where(first, jnp.zeros_like(halo), halo)
    ubuf[HALO:HALO + tm, :] = u_ref[...]

    t_pos = (i * tm) % seq + lax.broadcasted_iota(jnp.int32, (tm, POOL_GROUP_WIDTH), 0)
    pooled = []
    for g, w in enumerate(POOL_WINDOWS):
        lanes = slice(g * POOL_GROUP_WIDTH, (g + 1) * POOL_GROUP_WIDTH)
        levels = w.bit_length() - 1
        cur = ubuf[:, lanes]
        step = 1
        for _ in range(levels):
            cur = cur + pltpu.roll(cur, step, axis=0)
            step *= 2
        wsum = cur[HALO:, :]
        cnt = jnp.minimum(t_pos + 1, w).astype(jnp.float32)
        d = (wsum / cnt - ubuf[HALO:HALO + tm, lanes]).astype(jnp.bfloat16)
        pooled.append(jnp.dot(d, wpg_ref[g], preferred_element_type=jnp.float32))
    p = (jnp.concatenate(pooled, axis=-1) * pscale_ref[...]).astype(jnp.bfloat16)

    a_proj = lax.dot_general(at_ref[0], wba_ref[...], (((0,), (0,)), ((), ())),
                             preferred_element_type=jnp.float32)
    p_proj = jnp.dot(p, wbp_ref[...], preferred_element_type=jnp.float32)
    gates = g_ref[...].astype(jnp.float32)
    merged = (jax.nn.sigmoid(gates[:, :D_MODEL]) * a_proj
              + jax.nn.sigmoid(gates[:, D_MODEL:]) * p_proj)
    x1 = x_ref[...] + jnp.dot(merged.astype(jnp.bfloat16), wout_ref[...],
                              preferred_element_type=jnp.float32)

    h2 = (_rms(x1) * gffn_ref[...]).astype(jnp.bfloat16)
    gate = jnp.dot(h2, wg_ref[...], preferred_element_type=jnp.float32)
    up = jnp.dot(h2, wu_ref[...], preferred_element_type=jnp.float32)
    act = (jax.nn.silu(gate) * up).astype(jnp.bfloat16)
    o_ref[...] = x1 + jnp.dot(act, wd_ref[...], preferred_element_type=jnp.float32)


def _mix(x2, at, u, gates, wpg, pscale, wba, wbp, wout, gffn, wg, wu, wd, *, batch, seq, tm):
    n = batch * seq
    tiles_per_seq = seq // tm
    row = lambda i: (i, 0)
    const2 = lambda i: (0, 0)
    resident = functools.partial(pl.BlockSpec, pipeline_mode=pl.Buffered(1))
    return pl.pallas_call(
        functools.partial(_mix_kernel, tm=tm, seq=seq),
        grid=(n // tm,),
        in_specs=[
            pl.BlockSpec((tm, D_MODEL), row),
            pl.BlockSpec((1, ATTN_WIDTH, tm), lambda i: (i // tiles_per_seq, 0, i % tiles_per_seq)),
            pl.BlockSpec((tm, POOL_WIDTH), row),
            pl.BlockSpec((HALO, POOL_WIDTH), lambda i: (jnp.maximum(i * (tm // HALO) - 1, 0), 0)),
            pl.BlockSpec((tm, 2 * D_MODEL), row),
            resident((len(POOL_WINDOWS), POOL_GROUP_WIDTH, POOL_GROUP_WIDTH), lambda i: (0, 0, 0)),
            resident((1, POOL_WIDTH), const2),
            resident((ATTN_WIDTH, D_MODEL), const2),
            resident((POOL_WIDTH, D_MODEL), const2),
            resident((D_MODEL, D_MODEL), const2),
            resident((1, D_MODEL), const2),
            resident((D_MODEL, D_FF), const2),
            resident((D_MODEL, D_FF), const2),
            resident((D_FF, D_MODEL), const2),
        ],
        out_specs=pl.BlockSpec((tm, D_MODEL), row),
        out_shape=jax.ShapeDtypeStruct((n, D_MODEL), jnp.float32),
        scratch_shapes=[pltpu.VMEM((tm + HALO, POOL_WIDTH), jnp.float32)],
        compiler_params=pltpu.CompilerParams(
            dimension_semantics=("arbitrary",), vmem_limit_bytes=VMEM_LIMIT),
        name="mix",
    )(x2, at, u, u, gates, wpg, pscale, wba, wbp, wout, gffn, wg, wu, wd)


def _alibi_rows():
    slopes = 2.0 ** (-8.0 * jnp.arange(1, N_HEADS + 1, dtype=jnp.float32) / N_HEADS) * LOG2E
    parts, rest = [], slopes
    for _ in range(SLOPE_PARTS):
        term = rest.astype(jnp.bfloat16)
        parts.append(term)
        rest = rest - term.astype(jnp.float32)
    block = jnp.stack(parts * 2, axis=0)
    block = jnp.repeat(block, Q_BLOCK, axis=1)
    pad = jnp.zeros((AUG - HEAD_DIM - 2 * SLOPE_PARTS, N_HEADS * Q_BLOCK), jnp.bfloat16)
    return jnp.concatenate([block, pad], axis=0)


def _layer(x2, norm_mix, w_in, q_norm, k_norm, w_pool_group, pool_scale, w_branch_attn,
           w_branch_pool, w_out, norm_ffn, w_ffn_gate, w_ffn_up, w_ffn_down, *, batch, seq):
    bf = jnp.bfloat16
    gk = jnp.concatenate([k_norm, jnp.zeros((LANES - HEAD_DIM,), k_norm.dtype)])[None, :]

    qt, qit, wit, vt, kaug, kiaug, u, gates = _proj(
        x2, norm_mix[None, :], w_in, q_norm[:, None], gk, batch=batch, seq=seq, tm=PROJ_ROWS)
    at = _attn(qt, qit, wit, vt, kaug, kiaug, _alibi_rows(), batch=batch, seq=seq)
    return _mix(x2, at, u, gates, w_pool_group.astype(bf), pool_scale[None, :],
                w_branch_attn.astype(bf), w_branch_pool.astype(bf), w_out.astype(bf),
                norm_ffn[None, :], w_ffn_gate.astype(bf), w_ffn_up.astype(bf),
                w_ffn_down.astype(bf), batch=batch, seq=seq, tm=MIX_ROWS)


def kernel(x, norm_mix, w_in, q_norm, k_norm, w_pool_group, pool_scale, w_branch_attn,
           w_branch_pool, w_out, norm_ffn, w_ffn_gate, w_ffn_up, w_ffn_down):
    batch, seq, d = x.shape
    x2 = x.reshape(batch * seq, d)
    for l in range(norm_mix.shape[0]):
        x2 = _layer(x2, norm_mix[l], w_in[l], q_norm[l], k_norm[l], w_pool_group[l],
                    pool_scale[l], w_branch_attn[l], w_branch_pool[l], w_out[l], norm_ffn[l],
                    w_ffn_gate[l], w_ffn_up[l], w_ffn_down[l], batch=batch, seq=seq)
    return x2.reshape(batch, seq, d)
```

```python
import functools

import jax
import jax.numpy as jnp
from jax import lax
from jax.experimental import pallas as pl
from jax.experimental.pallas import tpu as pltpu

D_MODEL = 1024
N_HEADS = 8
HEAD_DIM = 64
ATTN_WIDTH = N_HEADS * HEAD_DIM
IDX_HEADS = 8
IDX_DIM = 64
IDX_SCALE = (IDX_HEADS ** -0.5) * (IDX_DIM ** -0.5)
TOPK = 256
POOL_WINDOWS = (2, 4, 8, 16)
POOL_WIDTH = 512
POOL_GROUP_WIDTH = 128
D_FF = 2816
RMS_EPS = 1e-6
NEG_BIG = -1e30

LANES = 128
PROJ_ROWS = 1024
MIX_ROWS = 512
Q_BLOCK = 256
GROUP = 4
KEY_CHUNK = 256
AUG = 128
VT_ROWS = 80
HALO = 16
LOG2E = 1.4426950408889634
SLOPE_PARTS = 3
PACK_ROWS = 16
BF16_STEP = 2.0 ** -7
BF16_EXACT_INT = 256
BISECT16_STEPS = 10
BISECT_STEPS = 6
TINY = 1e-30
VMEM_LIMIT = 56 * 1024 * 1024

T_Q0, T_V0, T_QI0, T_WI0, T_ROWS = 0, 512, 576, 1088, 1104
N_KK0, N_U0, N_G0, N_COLS = 0, 128, 640, 2688


def _rms(x, eps=RMS_EPS):
    return x * lax.rsqrt(jnp.mean(x * x, axis=-1, keepdims=True) + eps)


def _regroup_w_in(w_ref, wt_ref, wn_ref):
    bf = jnp.bfloat16
    wt_ref[T_Q0:T_Q0 + ATTN_WIDTH, :] = w_ref[:, 0:512].T.astype(bf)
    kv_qi = w_ref[:, 512:1152].T
    wt_ref[T_V0:T_WI0, :] = kv_qi[HEAD_DIM:, :].astype(bf)
    ki_wi = w_ref[:, 1152:1280].T
    tail = jnp.concatenate(
        [ki_wi[IDX_DIM:IDX_DIM + IDX_HEADS, :],
         jnp.zeros((T_ROWS - T_WI0 - IDX_HEADS, D_MODEL), jnp.float32)], axis=0)
    wt_ref[T_WI0:T_ROWS, :] = tail.astype(bf)
    wn_ref[:, N_KK0:N_KK0 + LANES] = jnp.concatenate(
        [w_ref[:, 512:576], w_ref[:, 1152:1216]], axis=1).astype(bf)
    wn_ref[:, N_U0:N_COLS] = w_ref[:, 1224:3784].astype(bf)


def _proj_kernel(x_ref, gmix_ref, w_ref, gq_ref, gk_ref,
                 qt_ref, qit_ref, wit_ref, vt_ref, kaug_ref, kiaug_ref, u_ref, g_ref,
                 wt_ref, wn_ref, *, tm, seq):
    i = pl.program_id(0)

    @pl.when(i == 0)
    def _():
        _regroup_w_in(w_ref, wt_ref, wn_ref)

    x = x_ref[...]
    h = (_rms(x) * gmix_ref[...]).astype(jnp.bfloat16)
    y_t = lax.dot_general(wt_ref[...], h, (((1,), (1,)), ((), ())),
                          preferred_element_type=jnp.float32)
    y_n = jnp.dot(h, wn_ref[...], preferred_element_type=jnp.float32)

    gq = gq_ref[...]
    for hd in range(N_HEADS):
        blk = y_t[T_Q0 + hd * HEAD_DIM:T_Q0 + (hd + 1) * HEAD_DIM, :]
        ss = jnp.mean(blk * blk, axis=0, keepdims=True)
        qn = blk * lax.rsqrt(ss + RMS_EPS) * gq * (HEAD_DIM ** -0.5 * LOG2E)
        qt_ref[0, hd * HEAD_DIM:(hd + 1) * HEAD_DIM, :] = qn.astype(qt_ref.dtype)

    qit_ref[0] = y_t[T_QI0:T_QI0 + IDX_HEADS * IDX_DIM, :].astype(qit_ref.dtype)
    wit_ref[0] = y_t[T_WI0:T_WI0 + IDX_HEADS, :] * IDX_SCALE

    v_t = y_t[T_V0:T_V0 + HEAD_DIM, :]
    row = lax.broadcasted_iota(jnp.int32, (VT_ROWS - HEAD_DIM, tm), 0)
    tail = jnp.where(row == 0, 1.0, 0.0)
    vt_full = jnp.concatenate([v_t, tail], axis=0).astype(vt_ref.dtype)
    for cc in range(tm // KEY_CHUNK):
        vt_ref[0, cc] = vt_full[:, cc * KEY_CHUNK:(cc + 1) * KEY_CHUNK]

    kk = y_n[:, N_KK0:N_KK0 + LANES]
    lane = lax.broadcasted_iota(jnp.int32, (tm, LANES), 1)
    is_k = lane < HEAD_DIM
    ss = jnp.sum(jnp.where(is_k, kk * kk, 0.0), axis=-1, keepdims=True) * (1.0 / HEAD_DIM)
    kn = kk * lax.rsqrt(ss + RMS_EPS) * gk_ref[...]
    pos = (i * tm) % seq + lax.broadcasted_iota(jnp.int32, (tm, LANES), 0)
    pos_lo = pos & (BF16_EXACT_INT - 1)
    pos_hi = (pos - pos_lo).astype(jnp.float32)
    pos_lo = pos_lo.astype(jnp.float32)
    feats = jnp.where(lane < HEAD_DIM + SLOPE_PARTS, pos_hi,
                      jnp.where(lane < HEAD_DIM + 2 * SLOPE_PARTS, pos_lo, 0.0))
    kaug_ref[0] = jnp.where(is_k, kn, feats).astype(kaug_ref.dtype)
    kiaug_ref[0] = jnp.where(is_k, 0.0, kk).astype(kiaug_ref.dtype)

    u_ref[...] = y_n[:, N_U0:N_U0 + POOL_WIDTH]
    g_ref[...] = y_n[:, N_G0:N_G0 + 2 * D_MODEL].astype(g_ref.dtype)


def _proj(x2, gmix, w_in_layers, gq, gk, *, batch, seq, tm):
    w_in, layer = w_in_layers
    n = batch * seq
    tiles_per_seq = seq // tm
    const = lambda i: (0, 0)
    bidx = lambda i: (i // tiles_per_seq, 0, i % tiles_per_seq)
    out_shape = (
        jax.ShapeDtypeStruct((batch, ATTN_WIDTH, seq), jnp.bfloat16),
        jax.ShapeDtypeStruct((batch, IDX_HEADS * IDX_DIM, seq), jnp.bfloat16),
        jax.ShapeDtypeStruct((batch, IDX_HEADS, seq), jnp.float32),
        jax.ShapeDtypeStruct((batch, seq // KEY_CHUNK, VT_ROWS, KEY_CHUNK), jnp.bfloat16),
        jax.ShapeDtypeStruct((batch, seq, AUG), jnp.bfloat16),
        jax.ShapeDtypeStruct((batch, seq, AUG), jnp.bfloat16),
        jax.ShapeDtypeStruct((n, POOL_WIDTH), jnp.float32),
        jax.ShapeDtypeStruct((n, 2 * D_MODEL), jnp.bfloat16),
    )
    out_specs = (
        pl.BlockSpec((1, ATTN_WIDTH, tm), bidx),
        pl.BlockSpec((1, IDX_HEADS * IDX_DIM, tm), bidx),
        pl.BlockSpec((1, IDX_HEADS, tm), bidx),
        pl.BlockSpec((1, tm // KEY_CHUNK, VT_ROWS, KEY_CHUNK),
                     lambda i: (i // tiles_per_seq, i % tiles_per_seq, 0, 0)),
        pl.BlockSpec((1, tm, AUG), lambda i: (i // tiles_per_seq, i % tiles_per_seq, 0)),
        pl.BlockSpec((1, tm, AUG), lambda i: (i // tiles_per_seq, i % tiles_per_seq, 0)),
        pl.BlockSpec((tm, POOL_WIDTH), lambda i: (i, 0)),
        pl.BlockSpec((tm, 2 * D_MODEL), lambda i: (i, 0)),
    )
    in_specs = [
        pl.BlockSpec((tm, D_MODEL), lambda i: (i, 0)),
        pl.BlockSpec((1, D_MODEL), const),
        pl.BlockSpec((None,) + w_in.shape[1:], lambda i: (layer, 0, 0),
                     pipeline_mode=pl.Buffered(1)),
        pl.BlockSpec((HEAD_DIM, 1), const),
        pl.BlockSpec((1, LANES), const),
    ]
    return pl.pallas_call(
        functools.partial(_proj_kernel, tm=tm, seq=seq),
        grid=(n // tm,),
        in_specs=in_specs,
        out_specs=out_specs,
        out_shape=out_shape,
        scratch_shapes=[
            pltpu.VMEM((T_ROWS, D_MODEL), jnp.bfloat16),
            pltpu.VMEM((D_MODEL, N_COLS), jnp.bfloat16),
        ],
        compiler_params=pltpu.CompilerParams(
            dimension_semantics=("arbitrary",), vmem_limit_bytes=VMEM_LIMIT),
        name="proj",
    )(x2, gmix, w_in, gq, gk)


def _attn_kernel(qt_ref, qit_ref, wit_ref, vt_ref, kaug_ref, kiaug_ref, qaug_ref,
                 at_ref, rhs_idx, rhs_att, sc_ref, sc16_ref, acc_t, buf_a, buf_b):
    j = pl.program_id(1)
    nq = Q_BLOCK
    kc = KEY_CHUNK
    nch = j + 1
    lanes = GROUP * nq
    tiles = GROUP * N_HEADS
    inf = jnp.float32(jnp.inf)

    def cols(t):
        return slice(t * nq, (t + 1) * nq)

    zeros64 = jnp.zeros((HEAD_DIM, nq), rhs_idx.dtype)
    for t in range(tiles):
        g, hd = divmod(t, N_HEADS)
        rows = slice(hd * HEAD_DIM, (hd + 1) * HEAD_DIM)
        rhs_idx[0:HEAD_DIM, cols(t)] = zeros64
        rhs_idx[HEAD_DIM:AUG, cols(t)] = qit_ref[g, rows, :]
        rhs_att[0:HEAD_DIM, cols(t)] = qt_ref[g, rows, :]
        rhs_att[HEAD_DIM:AUG, cols(t)] = qaug_ref[:, cols(hd)]

    t_row = j * nq + lax.broadcasted_iota(jnp.int32, (1, lanes), 1) % nq
    t_pos = j * nq + lax.broadcasted_iota(jnp.int32, (kc, nq), 1)
    s_loc = lax.broadcasted_iota(jnp.int32, (kc, nq), 0)

    def pairwise(step, n, carry):
        carry = lax.fori_loop(0, n // 2, lambda i, cr: step(2 * i + 1, step(2 * i, cr)), carry)
        return lax.cond(n % 2 == 1, lambda cr: step(n - 1, cr), lambda cr: cr, carry)

    def fold(x, op, rows=8):
        parts = [x[r * rows:(r + 1) * rows, :] for r in range(kc // rows)]
        while len(parts) > 1:
            parts = [op(parts[i], parts[i + 1]) for i in range(0, len(parts), 2)]
        return parts[0]

    bufs = (buf_a, buf_b)

    def pipelined(produce, consume, finish, carry, prepare=lambda c: None):
        def drain(c, par, carry, aux):
            part = None
            for t in range(tiles):
                part = consume(c, bufs[par], t, carry, aux, part)
            return finish(c, carry, part)

        def both(c, par, carry, aux):
            part, nxt, ctx = None, [], prepare(c + 1)
            for t in range(tiles):
                nxt.append(produce(c + 1, bufs[1 - par], t, ctx))
                part = consume(c, bufs[par], t, carry, aux, part)
            return finish(c, carry, part), tuple(nxt)

        def pair_body(i, st):
            return both(2 * i + 1, 1, *both(2 * i, 0, *st))

        ctx0 = prepare(0)
        first = tuple(produce(0, bufs[0], t, ctx0) for t in range(tiles))
        st = lax.fori_loop(0, (nch - 1) // 2, pair_body, (carry, first))

        def even_tail(st):
            return drain(nch - 1, 1, *both(nch - 2, 0, *st))

        return lax.cond(nch % 2 == 0, even_tail, lambda st: drain(nch - 1, 0, *st), st)

    def score_mm(c, buf, t, _):
        off = pl.multiple_of(c * kc, kc)
        buf[:, cols(t)] = jnp.dot(kiaug_ref[t // N_HEADS, pl.ds(off, kc), :], rhs_idx[:, cols(t)],
                                  preferred_element_type=jnp.float32)
        return jnp.int32(0)

    def score_head(c, buf, t, lo_hi, _, part):
        g, hd = divmod(t, N_HEADS)
        tot, los, his = part or (None, (), ())
        term = jnp.maximum(buf[:, cols(t)], 0.0) * wit_ref[g, hd:hd + 1, :]
        tot = term if hd == 0 else tot + term
        if hd < N_HEADS - 1:
            return tot, los, his
        off = pl.multiple_of(c * kc, kc)
        causal = off + s_loc <= t_pos
        masked = jnp.where(causal, tot, -inf)
        sc_ref[pl.ds(off, kc), cols(g)] = masked
        sc16_ref[pl.ds(off, kc), cols(g)] = masked.astype(jnp.bfloat16)
        return (None, los + (fold(jnp.where(causal, tot, inf), jnp.minimum),),
                his + (fold(masked, jnp.maximum),))

    def score_store(c, lo_hi, part):
        lo, hi = lo_hi
        return (jnp.minimum(lo, jnp.concatenate(part[1], axis=1)),
                jnp.maximum(hi, jnp.concatenate(part[2], axis=1)))

    lo, hi = pipelined(score_mm, score_head, score_store,
                       (jnp.full((8, lanes), inf, jnp.float32),
                        jnp.full((8, lanes), -inf, jnp.float32)))
    lo = jnp.min(lo, axis=0, keepdims=True)
    hi = jnp.max(hi, axis=0, keepdims=True)

    def by_lane_tile(ref, c, fn):
        off = pl.multiple_of(c * kc, kc)
        return jnp.concatenate(
            [fn(ref[pl.ds(off, kc), l * LANES:(l + 1) * LANES], slice(l * LANES, (l + 1) * LANES))
             for l in range(lanes // LANES)], axis=1)

    def count_ge(v):
        def step(c, acc):
            return acc + by_lane_tile(
                sc_ref, c, lambda s, ln: fold(jnp.where(s >= v[:, ln], 1.0, 0.0), jnp.add))
        acc = pairwise(step, nch, jnp.zeros((8, lanes), jnp.float32))
        return jnp.sum(acc, axis=0, keepdims=True)

    def count_ge16(v):
        v16 = v.astype(jnp.bfloat16)
        one, zero = jnp.ones((), jnp.bfloat16), jnp.zeros((), jnp.bfloat16)
        def step(c, acc):
            return acc + by_lane_tile(
                sc16_ref, c,
                lambda s, ln: fold(jnp.where(s >= v16[:, ln], one, zero), jnp.add, PACK_ROWS))
        acc = pairwise(step, nch, jnp.zeros((PACK_ROWS, lanes), jnp.bfloat16))
        return jnp.sum(acc.astype(jnp.float32), axis=0, keepdims=True)

    def max_below(v):
        def step(c, acc):
            return jnp.maximum(acc, by_lane_tile(
                sc_ref, c, lambda s, ln: fold(jnp.where(s < v[:, ln], s, -inf), jnp.maximum)))
        acc = pairwise(step, nch, jnp.full((8, lanes), -inf, jnp.float32))
        return jnp.max(acc, axis=0, keepdims=True)

    def n_open(done):
        return jnp.sum(1 - done)

    def snap(state):
        lo, hi, thr, cge, done = state
        b = max_below(hi)
        c = count_ge(b)
        ok = c >= TOPK
        newly = ok & (done == 0)
        return (lo, jnp.where(ok, hi, b), jnp.where(newly, b, thr),
                jnp.where(newly, c, cge), jnp.where(ok, 1, done))

    def bisect(state):
        lo, hi, thr, cge, done = state
        mid = 0.5 * lo + 0.5 * hi
        ge = count_ge(mid) >= TOPK
        return (jnp.where(ge, mid, lo), jnp.where(ge, hi, mid), thr, cge, done)

    def bisect16(_, lo_hi):
        lo, hi = lo_hi
        mid = (0.5 * lo + 0.5 * hi).astype(jnp.bfloat16).astype(jnp.float32)
        ge = count_ge16(mid) >= TOPK
        return jnp.where(ge, mid, lo), jnp.where(ge, hi, mid)

    short = jnp.where((t_row + 1) <= TOPK, 1, 0)
    n0 = n_open(short)
    run = n0 > 0
    lo16 = lo.astype(jnp.bfloat16).astype(jnp.float32)
    hi16 = hi.astype(jnp.bfloat16).astype(jnp.float32)
    hi16 = (hi16 + jnp.abs(hi16) * (2 * BF16_STEP) + TINY).astype(jnp.bfloat16).astype(jnp.float32)
    lo16, hi16 = lax.fori_loop(0, jnp.where(run, BISECT16_STEPS, 0), bisect16, (lo16, hi16))
    lo32 = jnp.maximum(lo, lo16 - jnp.abs(lo16) * BF16_STEP - TINY)
    state = (lo32, hi16, lo, jnp.zeros((1, lanes), jnp.float32), short)
    state = lax.fori_loop(0, jnp.where(run, BISECT_STEPS, 0), lambda _, s: bisect(s), state)
    state = lax.cond(run, snap, lambda s: s, state)

    def finish_body(carry):
        s = snap(bisect(carry[0]))
        return s, n_open(s[4])

    state, _ = lax.while_loop(lambda carry: carry[1] > 0, finish_body, (state, n_open(state[4])))
    thr, cge = state[2], state[3]

    @pl.when(jnp.max(cge) > TOPK)
    def _():
        def gt_body(c, acc):
            off = pl.multiple_of(c * kc, kc)
            return acc + fold(jnp.where(sc_ref[pl.ds(off, kc), :] > thr, 1.0, 0.0), jnp.add)
        cnt_gt = jnp.sum(lax.fori_loop(0, nch, gt_body, jnp.zeros((8, lanes), jnp.float32)),
                         axis=0, keepdims=True)
        need = TOPK - cnt_gt
        r_i = lax.broadcasted_iota(jnp.int32, (kc, kc), 0)
        c_i = lax.broadcasted_iota(jnp.int32, (kc, kc), 1)
        tri = jnp.where(c_i <= r_i, 1.0, 0.0).astype(jnp.bfloat16)

        def tie_body(c, seen):
            off = pl.multiple_of(c * kc, kc)
            s = sc_ref[pl.ds(off, kc), :]
            eq = s == thr
            eq_f = jnp.where(eq, 1.0, 0.0)
            incl = jnp.dot(tri, eq_f.astype(jnp.bfloat16), preferred_element_type=jnp.float32)
            rank = incl - eq_f + seen
            sc_ref[pl.ds(off, kc), :] = jnp.where(eq & (rank >= need), -inf, s)
            return seen + incl[kc - 1:kc, :]

        lax.fori_loop(0, nch, tie_body, jnp.zeros((1, lanes), jnp.float32))

    acc_t[...] = jnp.zeros_like(acc_t)

    def select_bias(c):
        off = pl.multiple_of(c * kc, kc)
        return jnp.where(sc_ref[pl.ds(off, kc), :] >= thr, 0.0, NEG_BIG)

    def masked_logits(c, buf, t, bias):
        g = t // N_HEADS
        off = pl.multiple_of(c * kc, kc)
        lm = jnp.dot(kaug_ref[g, pl.ds(off, kc), :], rhs_att[:, cols(t)],
                     preferred_element_type=jnp.float32) + bias[:, cols(g)]
        buf[:, cols(t)] = lm
        return jnp.max(fold(lm, jnp.maximum), axis=0, keepdims=True)

    def softmax_pv(c, buf, t, m_old, cmax, m_list):
        m_new = jnp.maximum(m_old[t], cmax[t])
        alpha = jnp.exp2(m_old[t] - m_new)
        p = jnp.exp2((buf[:, cols(t)] - m_new).astype(jnp.bfloat16))
        pv = jnp.dot(vt_ref[t // N_HEADS, c], p, preferred_element_type=jnp.float32)
        acc_t[:, cols(t)] = acc_t[:, cols(t)] * alpha + pv
        return (m_list or ()) + (m_new,)

    pipelined(masked_logits, softmax_pv, lambda c, m_old, m_list: m_list,
              tuple(jnp.full((1, nq), NEG_BIG, jnp.float32) for _ in range(tiles)),
              prepare=select_bias)

    for t in range(tiles):
        g, hd = divmod(t, N_HEADS)
        inv_l = 1.0 / acc_t[HEAD_DIM:HEAD_DIM + 1, cols(t)]
        at_ref[g, hd * HEAD_DIM:(hd + 1) * HEAD_DIM, :] = (
            acc_t[0:HEAD_DIM, cols(t)] * inv_l).astype(at_ref.dtype)


def _attn(qt, qit, wit, vt, kaug, kiaug, qaug, *, batch, seq):
    nq = Q_BLOCK
    assert KEY_CHUNK == Q_BLOCK and seq % nq == 0 and batch % GROUP == 0
    wide = GROUP * N_HEADS * nq
    qblk = lambda b, j: (b, 0, j)
    whole = lambda b, j: (b, 0, 0)
    return pl.pallas_call(
        _attn_kernel,
        grid=(batch // GROUP, seq // nq),
        in_specs=[
            pl.BlockSpec((GROUP, ATTN_WIDTH, nq), qblk),
            pl.BlockSpec((GROUP, IDX_HEADS * IDX_DIM, nq), qblk),
            pl.BlockSpec((GROUP, IDX_HEADS, nq), qblk),
            pl.BlockSpec((GROUP, seq // KEY_CHUNK, VT_ROWS, KEY_CHUNK),
                         lambda b, j: (b, 0, 0, 0)),
            pl.BlockSpec((GROUP, seq, AUG), whole),
            pl.BlockSpec((GROUP, seq, AUG), whole),
            pl.BlockSpec((AUG - HEAD_DIM, N_HEADS * nq), lambda b, j: (0, 0)),
        ],
        out_specs=pl.BlockSpec((GROUP, ATTN_WIDTH, nq), qblk),
        out_shape=jax.ShapeDtypeStruct((batch, ATTN_WIDTH, seq), jnp.bfloat16),
        scratch_shapes=[
            pltpu.VMEM((AUG, wide), jnp.bfloat16),
            pltpu.VMEM((AUG, wide), jnp.bfloat16),
            pltpu.VMEM((seq, GROUP * nq), jnp.float32),
            pltpu.VMEM((seq, GROUP * nq), jnp.bfloat16),
            pltpu.VMEM((VT_ROWS, wide), jnp.float32),
            pltpu.VMEM((KEY_CHUNK, wide), jnp.float32),
            pltpu.VMEM((KEY_CHUNK, wide), jnp.float32),
        ],
        compiler_params=pltpu.CompilerParams(
            dimension_semantics=("arbitrary", "arbitrary"), vmem_limit_bytes=VMEM_LIMIT),
        name="attn",
    )(qt, qit, wit, vt, kaug, kiaug, qaug)


def _mix_kernel(x_ref, at_ref, u_ref, halo_ref, g_ref, wpg_ref, pscale_ref, wba_ref, wbp_ref,
                wout_ref, gffn_ref, wg_ref, wu_ref, wd_ref, o_ref, ubuf, *, tm, seq):
    i = pl.program_id(0)
    tiles_per_seq = seq // tm
    first = (i % tiles_per_seq) == 0
    halo = halo_ref[...]
    ubuf[0:HALO, :] = jnp.where(first, jnp.zeros_like(halo), halo)
    ubuf[HALO:HALO + tm, :] = u_ref[...]

    t_pos = (i * tm) % seq + lax.broadcasted_iota(jnp.int32, (tm, POOL_GROUP_WIDTH), 0)
    pooled = []
    for g, w in enumerate(POOL_WINDOWS):
        lanes = slice(g * POOL_GROUP_WIDTH, (g + 1) * POOL_GROUP_WIDTH)
        levels = w.bit_length() - 1
        cur = ubuf[:, lanes]
        step = 1
        for _ in range(levels):
            cur = cur + pltpu.roll(cur, step, axis=0)
            step *= 2
        wsum = cur[HALO:, :]
        cnt = jnp.minimum(t_pos + 1, w).astype(jnp.float32)
        d = (wsum / cnt - ubuf[HALO:HALO + tm, lanes]).astype(jnp.bfloat16)
        pooled.append(jnp.dot(d, wpg_ref[g], preferred_element_type=jnp.float32))
    p = (jnp.concatenate(pooled, axis=-1) * pscale_ref[...]).astype(jnp.bfloat16)

    a_proj = lax.dot_general(at_ref[0], wba_ref[...], (((0,), (0,)), ((), ())),
                             preferred_element_type=jnp.float32)
    p_proj = jnp.dot(p, wbp_ref[...], preferred_element_type=jnp.float32)
    gates = g_ref[...].astype(jnp.float32)
    merged = (jax.nn.sigmoid(gates[:, :D_MODEL]) * a_proj
              + jax.nn.sigmoid(gates[:, D_MODEL:]) * p_proj)
    x1 = x_ref[...] + jnp.dot(merged.astype(jnp.bfloat16), wout_ref[...],
                              preferred_element_type=jnp.float32)

    h2 = (_rms(x1) * gffn_ref[...]).astype(jnp.bfloat16)
    gate = jnp.dot(h2, wg_ref[...], preferred_element_type=jnp.float32)
    up = jnp.dot(h2, wu_ref[...], preferred_element_type=jnp.float32)
    act = (jax.nn.silu(gate) * up).astype(jnp.bfloat16)
    o_ref[...] = x1 + jnp.dot(act, wd_ref[...], preferred_element_type=jnp.float32)


def _mix(x2, at, u, gates, wpg, pscale, wba, wbp, wout, gffn, wg, wu, wd, *, batch, seq, tm):
    n = batch * seq
    tiles_per_seq = seq // tm
    row = lambda i: (i, 0)
    const2 = lambda i: (0, 0)
    resident = functools.partial(pl.BlockSpec, pipeline_mode=pl.Buffered(1))
    return pl.pallas_call(
        functools.partial(_mix_kernel, tm=tm, seq=seq),
        grid=(n // tm,),
        in_specs=[
            pl.BlockSpec((tm, D_MODEL), row),
            pl.BlockSpec((1, ATTN_WIDTH, tm), lambda i: (i // tiles_per_seq, 0, i % tiles_per_seq)),
            pl.BlockSpec((tm, POOL_WIDTH), row),
            pl.BlockSpec((HALO, POOL_WIDTH), lambda i: (jnp.maximum(i * (tm // HALO) - 1, 0), 0)),
            pl.BlockSpec((tm, 2 * D_MODEL), row),
            resident((len(POOL_WINDOWS), POOL_GROUP_WIDTH, POOL_GROUP_WIDTH), lambda i: (0, 0, 0)),
            resident((1, POOL_WIDTH), const2),
            resident((ATTN_WIDTH, D_MODEL), const2),
            resident((POOL_WIDTH, D_MODEL), const2),
            resident((D_MODEL, D_MODEL), const2),
            resident((1, D_MODEL), const2),
            resident((D_MODEL, D_FF), const2),
            resident((D_MODEL, D_FF), const2),
            resident((D_FF, D_MODEL), const2),
        ],
        out_specs=pl.BlockSpec((tm, D_MODEL), row),
        out_shape=jax.ShapeDtypeStruct((n, D_MODEL), jnp.float32),
        scratch_shapes=[pltpu.VMEM((tm + HALO, POOL_WIDTH), jnp.float32)],
        compiler_params=pltpu.CompilerParams(
            dimension_semantics=("arbitrary",), vmem_limit_bytes=VMEM_LIMIT),
        name="mix",
    )(x2, at, u, u, gates, wpg, pscale, wba, wbp, wout, gffn, wg, wu, wd)


def _alibi_rows():
    slopes = 2.0 ** (-8.0 * jnp.arange(1, N_HEADS + 1, dtype=jnp.float32) / N_HEADS) * LOG2E
    parts, rest = [], slopes
    for _ in range(SLOPE_PARTS):
        term = rest.astype(jnp.bfloat16)
        parts.append(term)
        rest = rest - term.astype(jnp.float32)
    block = jnp.stack(parts * 2, axis=0)
    block = jnp.repeat(block, Q_BLOCK, axis=1)
    pad = jnp.zeros((AUG - HEAD_DIM - 2 * SLOPE_PARTS, N_HEADS * Q_BLOCK), jnp.bfloat16)
    return jnp.concatenate([block, pad], axis=0)


def _layer(x2, norm_mix, w_in, q_norm, k_norm, w_pool_group, pool_scale, w_branch_attn,
           w_branch_pool, w_out, norm_ffn, w_ffn_gate, w_ffn_up, w_ffn_down, *, batch, seq):
    bf = jnp.bfloat16
    gk = jnp.concatenate([k_norm, jnp.zeros((LANES - HEAD_DIM,), k_norm.dtype)])[None, :]

    qt, qit, wit, vt, kaug, kiaug, u, gates = _proj(
        x2, norm_mix[None, :], w_in, q_norm[:, None], gk, batch=batch, seq=seq, tm=PROJ_ROWS)
    at = _attn(qt, qit, wit, vt, kaug, kiaug, _alibi_rows(), batch=batch, seq=seq)
    return _mix(x2, at, u, gates, w_pool_group.astype(bf), pool_scale[None, :],
                w_branch_attn.astype(bf), w_branch_pool.astype(bf), w_out.astype(bf),
                norm_ffn[None, :], w_ffn_gate.astype(bf), w_ffn_up.astype(bf),
                w_ffn_down.astype(bf), batch=batch, seq=seq, tm=MIX_ROWS)


def kernel(x, norm_mix, w_in, q_norm, k_norm, w_pool_group, pool_scale, w_branch_attn,
           w_branch_pool, w_out, norm_ffn, w_ffn_gate, w_ffn_up, w_ffn_down):
    batch, seq, d = x.shape
    x2 = x.reshape(batch * seq, d)
    for l in range(norm_mix.shape[0]):
        x2 = _layer(x2, norm_mix[l], (w_in, l), q_norm[l], k_norm[l], w_pool_group[l],
                    pool_scale[l], w_branch_attn[l], w_branch_pool[l], w_out[l], norm_ffn[l],
                    w_ffn_gate[l], w_ffn_up[l], w_ffn_down[l], batch=batch, seq=seq)
    return x2.reshape(batch, seq, d)
```

```python
import functools

import jax
import jax.numpy as jnp
from jax import lax
from jax.experimental import pallas as pl
from jax.experimental.pallas import tpu as pltpu

D_MODEL = 1024
N_HEADS = 8
HEAD_DIM = 64
ATTN_WIDTH = N_HEADS * HEAD_DIM
IDX_HEADS = 8
IDX_DIM = 64
IDX_SCALE = (IDX_HEADS ** -0.5) * (IDX_DIM ** -0.5)
TOPK = 256
POOL_WINDOWS = (2, 4, 8, 16)
POOL_WIDTH = 512
POOL_GROUP_WIDTH = 128
D_FF = 2816
RMS_EPS = 1e-6
NEG_BIG = -1e30

LANES = 128
PROJ_ROWS = 1024
MIX_ROWS = 512
Q_BLOCK = 256
GROUP = 4
KEY_CHUNK = 256
AUG = 128
VT_ROWS = 80
HALO = 16
LOG2E = 1.4426950408889634
SLOPE_PARTS = 3
PACK_ROWS = 16
BF16_STEP = 2.0 ** -7
BF16_EXACT_INT = 256
BISECT16_STEPS = 10
BISECT_STEPS = 6
TINY = 1e-30
VMEM_LIMIT = 56 * 1024 * 1024

T_Q0, T_V0, T_QI0, T_WI0, T_ROWS = 0, 512, 576, 1088, 1104
N_KK0, N_U0, N_G0, N_COLS = 0, 128, 640, 2688


def _rms(x, eps=RMS_EPS):
    return x * lax.rsqrt(jnp.mean(x * x, axis=-1, keepdims=True) + eps)


def _regroup_w_in(w_ref, wt_ref, wn_ref):
    bf = jnp.bfloat16
    wt_ref[T_Q0:T_V0, :] = w_ref[0:512, :].astype(bf)
    wt_ref[T_V0:T_WI0, :] = w_ref[576:1152, :].astype(bf)
    tail = jnp.concatenate(
        [w_ref[1216:1224, :], jnp.zeros((T_ROWS - T_WI0 - IDX_HEADS, D_MODEL), jnp.float32)],
        axis=0)
    wt_ref[T_WI0:T_ROWS, :] = tail.astype(bf)
    k_ki = jnp.concatenate([w_ref[512:576, :], w_ref[1152:1216, :]], axis=0)
    wn_ref[:, N_KK0:N_KK0 + LANES] = k_ki.T.astype(bf)
    wn_ref[:, N_U0:N_COLS] = w_ref[1224:3784, :].T.astype(bf)


def _proj_kernel(x_ref, gmix_ref, w_ref, gq_ref, gk_ref,
                 qt_ref, qit_ref, wit_ref, vt_ref, kaug_ref, kiaug_ref, u_ref, g_ref,
                 wt_ref, wn_ref, *, tm, seq):
    i = pl.program_id(0)

    @pl.when(i == 0)
    def _():
        _regroup_w_in(w_ref, wt_ref, wn_ref)

    x = x_ref[...]
    h = (_rms(x) * gmix_ref[...]).astype(jnp.bfloat16)
    y_t = lax.dot_general(wt_ref[...], h, (((1,), (1,)), ((), ())),
                          preferred_element_type=jnp.float32)
    y_n = jnp.dot(h, wn_ref[...], preferred_element_type=jnp.float32)

    gq = gq_ref[...]
    for hd in range(N_HEADS):
        blk = y_t[T_Q0 + hd * HEAD_DIM:T_Q0 + (hd + 1) * HEAD_DIM, :]
        ss = jnp.mean(blk * blk, axis=0, keepdims=True)
        qn = blk * lax.rsqrt(ss + RMS_EPS) * gq * (HEAD_DIM ** -0.5 * LOG2E)
        qt_ref[0, hd * HEAD_DIM:(hd + 1) * HEAD_DIM, :] = qn.astype(qt_ref.dtype)

    qit_ref[0] = y_t[T_QI0:T_QI0 + IDX_HEADS * IDX_DIM, :].astype(qit_ref.dtype)
    wit_ref[0] = y_t[T_WI0:T_WI0 + IDX_HEADS, :] * IDX_SCALE

    v_t = y_t[T_V0:T_V0 + HEAD_DIM, :]
    row = lax.broadcasted_iota(jnp.int32, (VT_ROWS - HEAD_DIM, tm), 0)
    tail = jnp.where(row == 0, 1.0, 0.0)
    vt_full = jnp.concatenate([v_t, tail], axis=0).astype(vt_ref.dtype)
    for cc in range(tm // KEY_CHUNK):
        vt_ref[0, cc] = vt_full[:, cc * KEY_CHUNK:(cc + 1) * KEY_CHUNK]

    kk = y_n[:, N_KK0:N_KK0 + LANES]
    lane = lax.broadcasted_iota(jnp.int32, (tm, LANES), 1)
    is_k = lane < HEAD_DIM
    ss = jnp.sum(jnp.where(is_k, kk * kk, 0.0), axis=-1, keepdims=True) * (1.0 / HEAD_DIM)
    kn = kk * lax.rsqrt(ss + RMS_EPS) * gk_ref[...]
    pos = (i * tm) % seq + lax.broadcasted_iota(jnp.int32, (tm, LANES), 0)
    pos_lo = pos & (BF16_EXACT_INT - 1)
    pos_hi = (pos - pos_lo).astype(jnp.float32)
    pos_lo = pos_lo.astype(jnp.float32)
    feats = jnp.where(lane < HEAD_DIM + SLOPE_PARTS, pos_hi,
                      jnp.where(lane < HEAD_DIM + 2 * SLOPE_PARTS, pos_lo, 0.0))
    kaug_ref[0] = jnp.where(is_k, kn, feats).astype(kaug_ref.dtype)
    kiaug_ref[0] = jnp.where(is_k, 0.0, kk).astype(kiaug_ref.dtype)

    u_ref[...] = y_n[:, N_U0:N_U0 + POOL_WIDTH]
    g_ref[...] = y_n[:, N_G0:N_G0 + 2 * D_MODEL].astype(g_ref.dtype)


def _proj(x2, gmix, w_in_layers, gq, gk, *, batch, seq, tm):
    w_in_t, layer = w_in_layers
    n = batch * seq
    tiles_per_seq = seq // tm
    const = lambda i: (0, 0)
    bidx = lambda i: (i // tiles_per_seq, 0, i % tiles_per_seq)
    out_shape = (
        jax.ShapeDtypeStruct((batch, ATTN_WIDTH, seq), jnp.bfloat16),
        jax.ShapeDtypeStruct((batch, IDX_HEADS * IDX_DIM, seq), jnp.bfloat16),
        jax.ShapeDtypeStruct((batch, IDX_HEADS, seq), jnp.float32),
        jax.ShapeDtypeStruct((batch, seq // KEY_CHUNK, VT_ROWS, KEY_CHUNK), jnp.bfloat16),
        jax.ShapeDtypeStruct((batch, seq, AUG), jnp.bfloat16),
        jax.ShapeDtypeStruct((batch, seq, AUG), jnp.bfloat16),
        jax.ShapeDtypeStruct((n, POOL_WIDTH), jnp.float32),
        jax.ShapeDtypeStruct((n, 2 * D_MODEL), jnp.bfloat16),
    )
    out_specs = (
        pl.BlockSpec((1, ATTN_WIDTH, tm), bidx),
        pl.BlockSpec((1, IDX_HEADS * IDX_DIM, tm), bidx),
        pl.BlockSpec((1, IDX_HEADS, tm), bidx),
        pl.BlockSpec((1, tm // KEY_CHUNK, VT_ROWS, KEY_CHUNK),
                     lambda i: (i // tiles_per_seq, i % tiles_per_seq, 0, 0)),
        pl.BlockSpec((1, tm, AUG), lambda i: (i // tiles_per_seq, i % tiles_per_seq, 0)),
        pl.BlockSpec((1, tm, AUG), lambda i: (i // tiles_per_seq, i % tiles_per_seq, 0)),
        pl.BlockSpec((tm, POOL_WIDTH), lambda i: (i, 0)),
        pl.BlockSpec((tm, 2 * D_MODEL), lambda i: (i, 0)),
    )
    in_specs = [
        pl.BlockSpec((tm, D_MODEL), lambda i: (i, 0)),
        pl.BlockSpec((1, D_MODEL), const),
        pl.BlockSpec((None,) + w_in_t.shape[1:], lambda i: (layer, 0, 0),
                     pipeline_mode=pl.Buffered(1)),
        pl.BlockSpec((HEAD_DIM, 1), const),
        pl.BlockSpec((1, LANES), const),
    ]
    return pl.pallas_call(
        functools.partial(_proj_kernel, tm=tm, seq=seq),
        grid=(n // tm,),
        in_specs=in_specs,
        out_specs=out_specs,
        out_shape=out_shape,
        scratch_shapes=[
            pltpu.VMEM((T_ROWS, D_MODEL), jnp.bfloat16),
            pltpu.VMEM((D_MODEL, N_COLS), jnp.bfloat16),
        ],
        compiler_params=pltpu.CompilerParams(
            dimension_semantics=("arbitrary",), vmem_limit_bytes=VMEM_LIMIT),
        name="proj",
    )(x2, gmix, w_in_t, gq, gk)


def _attn_kernel(qt_ref, qit_ref, wit_ref, vt_ref, kaug_ref, kiaug_ref, qaug_ref,
                 at_ref, rhs_idx, rhs_att, sc_ref, sc16_ref, acc_t, buf_a, buf_b):
    j = pl.program_id(1)
    nq = Q_BLOCK
    kc = KEY_CHUNK
    nch = j + 1
    lanes = GROUP * nq
    tiles = GROUP * N_HEADS
    inf = jnp.float32(jnp.inf)

    def cols(t):
        return slice(t * nq, (t + 1) * nq)

    zeros64 = jnp.zeros((HEAD_DIM, nq), rhs_idx.dtype)
    for t in range(tiles):
        g, hd = divmod(t, N_HEADS)
        rows = slice(hd * HEAD_DIM, (hd + 1) * HEAD_DIM)
        rhs_idx[0:HEAD_DIM, cols(t)] = zeros64
        rhs_idx[HEAD_DIM:AUG, cols(t)] = qit_ref[g, rows, :]
        rhs_att[0:HEAD_DIM, cols(t)] = qt_ref[g, rows, :]
        rhs_att[HEAD_DIM:AUG, cols(t)] = qaug_ref[:, cols(hd)]

    t_row = j * nq + lax.broadcasted_iota(jnp.int32, (1, lanes), 1) % nq
    t_pos = j * nq + lax.broadcasted_iota(jnp.int32, (kc, nq), 1)
    s_loc = lax.broadcasted_iota(jnp.int32, (kc, nq), 0)

    def pairwise(step, n, carry):
        carry = lax.fori_loop(0, n // 2, lambda i, cr: step(2 * i + 1, step(2 * i, cr)), carry)
        return lax.cond(n % 2 == 1, lambda cr: step(n - 1, cr), lambda cr: cr, carry)

    def fold(x, op, rows=8):
        parts = [x[r * rows:(r + 1) * rows, :] for r in range(kc // rows)]
        while len(parts) > 1:
            parts = [op(parts[i], parts[i + 1]) for i in range(0, len(parts), 2)]
        return parts[0]

    bufs = (buf_a, buf_b)

    def pipelined(produce, consume, finish, carry, prepare=lambda c: None):
        def drain(c, par, carry, aux):
            part = None
            for t in range(tiles):
                part = consume(c, bufs[par], t, carry, aux, part)
            return finish(c, carry, part)

        def both(c, par, carry, aux):
            part, nxt, ctx = None, [], prepare(c + 1)
            for t in range(tiles):
                nxt.append(produce(c + 1, bufs[1 - par], t, ctx))
                part = consume(c, bufs[par], t, carry, aux, part)
            return finish(c, carry, part), tuple(nxt)

        def pair_body(i, st):
            return both(2 * i + 1, 1, *both(2 * i, 0, *st))

        ctx0 = prepare(0)
        first = tuple(produce(0, bufs[0], t, ctx0) for t in range(tiles))
        st = lax.fori_loop(0, (nch - 1) // 2, pair_body, (carry, first))

        def even_tail(st):
            return drain(nch - 1, 1, *both(nch - 2, 0, *st))

        return lax.cond(nch % 2 == 0, even_tail, lambda st: drain(nch - 1, 0, *st), st)

    def score_mm(c, buf, t, _):
        off = pl.multiple_of(c * kc, kc)
        buf[:, cols(t)] = jnp.dot(kiaug_ref[t // N_HEADS, pl.ds(off, kc), :], rhs_idx[:, cols(t)],
                                  preferred_element_type=jnp.float32)
        return jnp.int32(0)

    def score_head(c, buf, t, lo_hi, _, part):
        g, hd = divmod(t, N_HEADS)
        tot, los, his = part or (None, (), ())
        term = jnp.maximum(buf[:, cols(t)], 0.0) * wit_ref[g, hd:hd + 1, :]
        tot = term if hd == 0 else tot + term
        if hd < N_HEADS - 1:
            return tot, los, his
        off = pl.multiple_of(c * kc, kc)
        causal = off + s_loc <= t_pos
        masked = jnp.where(causal, tot, -inf)
        sc_ref[pl.ds(off, kc), cols(g)] = masked
        sc16_ref[pl.ds(off, kc), cols(g)] = masked.astype(jnp.bfloat16)
        return (None, los + (fold(jnp.where(causal, tot, inf), jnp.minimum),),
                his + (fold(masked, jnp.maximum),))

    def score_store(c, lo_hi, part):
        lo, hi = lo_hi
        return (jnp.minimum(lo, jnp.concatenate(part[1], axis=1)),
                jnp.maximum(hi, jnp.concatenate(part[2], axis=1)))

    lo, hi = pipelined(score_mm, score_head, score_store,
                       (jnp.full((8, lanes), inf, jnp.float32),
                        jnp.full((8, lanes), -inf, jnp.float32)))
    lo = jnp.min(lo, axis=0, keepdims=True)
    hi = jnp.max(hi, axis=0, keepdims=True)

    def by_lane_tile(ref, c, fn):
        off = pl.multiple_of(c * kc, kc)
        return jnp.concatenate(
            [fn(ref[pl.ds(off, kc), l * LANES:(l + 1) * LANES], slice(l * LANES, (l + 1) * LANES))
             for l in range(lanes // LANES)], axis=1)

    def count_ge(v):
        def step(c, acc):
            return acc + by_lane_tile(
                sc_ref, c, lambda s, ln: fold(jnp.where(s >= v[:, ln], 1.0, 0.0), jnp.add))
        acc = pairwise(step, nch, jnp.zeros((8, lanes), jnp.float32))
        return jnp.sum(acc, axis=0, keepdims=True)

    def count_ge16(v):
        v16 = v.astype(jnp.bfloat16)
        one, zero = jnp.ones((), jnp.bfloat16), jnp.zeros((), jnp.bfloat16)
        def step(c, acc):
            return acc + by_lane_tile(
                sc16_ref, c,
                lambda s, ln: fold(jnp.where(s >= v16[:, ln], one, zero), jnp.add, PACK_ROWS))
        acc = pairwise(step, nch, jnp.zeros((PACK_ROWS, lanes), jnp.bfloat16))
        return jnp.sum(acc.astype(jnp.float32), axis=0, keepdims=True)

    def max_below(v):
        def step(c, acc):
            return jnp.maximum(acc, by_lane_tile(
                sc_ref, c, lambda s, ln: fold(jnp.where(s < v[:, ln], s, -inf), jnp.maximum)))
        acc = pairwise(step, nch, jnp.full((8, lanes), -inf, jnp.float32))
        return jnp.max(acc, axis=0, keepdims=True)

    def n_open(done):
        return jnp.sum(1 - done)

    def snap(state):
        lo, hi, thr, cge, done = state
        b = max_below(hi)
        c = count_ge(b)
        ok = c >= TOPK
        newly = ok & (done == 0)
        return (lo, jnp.where(ok, hi, b), jnp.where(newly, b, thr),
                jnp.where(newly, c, cge), jnp.where(ok, 1, done))

    def bisect(state):
        lo, hi, thr, cge, done = state
        mid = 0.5 * lo + 0.5 * hi
        ge = count_ge(mid) >= TOPK
        return (jnp.where(ge, mid, lo), jnp.where(ge, hi, mid), thr, cge, done)

    def bisect16(_, lo_hi):
        lo, hi = lo_hi
        mid = (0.5 * lo + 0.5 * hi).astype(jnp.bfloat16).astype(jnp.float32)
        ge = count_ge16(mid) >= TOPK
        return jnp.where(ge, mid, lo), jnp.where(ge, hi, mid)

    short = jnp.where((t_row + 1) <= TOPK, 1, 0)
    n0 = n_open(short)
    run = n0 > 0
    lo16 = lo.astype(jnp.bfloat16).astype(jnp.float32)
    hi16 = hi.astype(jnp.bfloat16).astype(jnp.float32)
    hi16 = (hi16 + jnp.abs(hi16) * (2 * BF16_STEP) + TINY).astype(jnp.bfloat16).astype(jnp.float32)
    lo16, hi16 = lax.fori_loop(0, jnp.where(run, BISECT16_STEPS, 0), bisect16, (lo16, hi16))
    lo32 = jnp.maximum(lo, lo16 - jnp.abs(lo16) * BF16_STEP - TINY)
    state = (lo32, hi16, lo, jnp.zeros((1, lanes), jnp.float32), short)
    state = lax.fori_loop(0, jnp.where(run, BISECT_STEPS, 0), lambda _, s: bisect(s), state)
    state = lax.cond(run, snap, lambda s: s, state)

    def finish_body(carry):
        s = snap(bisect(carry[0]))
        return s, n_open(s[4])

    state, _ = lax.while_loop(lambda carry: carry[1] > 0, finish_body, (state, n_open(state[4])))
    thr, cge = state[2], state[3]

    @pl.when(jnp.max(cge) > TOPK)
    def _():
        def gt_body(c, acc):
            off = pl.multiple_of(c * kc, kc)
            return acc + fold(jnp.where(sc_ref[pl.ds(off, kc), :] > thr, 1.0, 0.0), jnp.add)
        cnt_gt = jnp.sum(lax.fori_loop(0, nch, gt_body, jnp.zeros((8, lanes), jnp.float32)),
                         axis=0, keepdims=True)
        need = TOPK - cnt_gt
        r_i = lax.broadcasted_iota(jnp.int32, (kc, kc), 0)
        c_i = lax.broadcasted_iota(jnp.int32, (kc, kc), 1)
        tri = jnp.where(c_i <= r_i, 1.0, 0.0).astype(jnp.bfloat16)

        def tie_body(c, seen):
            off = pl.multiple_of(c * kc, kc)
            s = sc_ref[pl.ds(off, kc), :]
            eq = s == thr
            eq_f = jnp.where(eq, 1.0, 0.0)
            incl = jnp.dot(tri, eq_f.astype(jnp.bfloat16), preferred_element_type=jnp.float32)
            rank = incl - eq_f + seen
            sc_ref[pl.ds(off, kc), :] = jnp.where(eq & (rank >= need), -inf, s)
            return seen + incl[kc - 1:kc, :]

        lax.fori_loop(0, nch, tie_body, jnp.zeros((1, lanes), jnp.float32))

    acc_t[...] = jnp.zeros_like(acc_t)

    def select_bias(c):
        off = pl.multiple_of(c * kc, kc)
        return jnp.where(sc_ref[pl.ds(off, kc), :] >= thr, 0.0, NEG_BIG)

    def masked_logits(c, buf, t, bias):
        g = t // N_HEADS
        off = pl.multiple_of(c * kc, kc)
        lm = jnp.dot(kaug_ref[g, pl.ds(off, kc), :], rhs_att[:, cols(t)],
                     preferred_element_type=jnp.float32) + bias[:, cols(g)]
        buf[:, cols(t)] = lm
        return jnp.max(fold(lm, jnp.maximum), axis=0, keepdims=True)

    def softmax_pv(c, buf, t, m_old, cmax, m_list):
        m_new = jnp.maximum(m_old[t], cmax[t])
        alpha = jnp.exp2(m_old[t] - m_new)
        p = jnp.exp2((buf[:, cols(t)] - m_new).astype(jnp.bfloat16))
        pv = jnp.dot(vt_ref[t // N_HEADS, c], p, preferred_element_type=jnp.float32)
        acc_t[:, cols(t)] = acc_t[:, cols(t)] * alpha + pv
        return (m_list or ()) + (m_new,)

    pipelined(masked_logits, softmax_pv, lambda c, m_old, m_list: m_list,
              tuple(jnp.full((1, nq), NEG_BIG, jnp.float32) for _ in range(tiles)),
              prepare=select_bias)

    for t in range(tiles):
        g, hd = divmod(t, N_HEADS)
        inv_l = 1.0 / acc_t[HEAD_DIM:HEAD_DIM + 1, cols(t)]
        at_ref[g, hd * HEAD_DIM:(hd + 1) * HEAD_DIM, :] = (
            acc_t[0:HEAD_DIM, cols(t)] * inv_l).astype(at_ref.dtype)


def _attn(qt, qit, wit, vt, kaug, kiaug, qaug, *, batch, seq):
    nq = Q_BLOCK
    assert KEY_CHUNK == Q_BLOCK and seq % nq == 0 and batch % GROUP == 0
    wide = GROUP * N_HEADS * nq
    qblk = lambda b, j: (b, 0, j)
    whole = lambda b, j: (b, 0, 0)
    return pl.pallas_call(
        _attn_kernel,
        grid=(batch // GROUP, seq // nq),
        in_specs=[
            pl.BlockSpec((GROUP, ATTN_WIDTH, nq), qblk),
            pl.BlockSpec((GROUP, IDX_HEADS * IDX_DIM, nq), qblk),
            pl.BlockSpec((GROUP, IDX_HEADS, nq), qblk),
            pl.BlockSpec((GROUP, seq // KEY_CHUNK, VT_ROWS, KEY_CHUNK),
                         lambda b, j: (b, 0, 0, 0)),
            pl.BlockSpec((GROUP, seq, AUG), whole),
            pl.BlockSpec((GROUP, seq, AUG), whole),
            pl.BlockSpec((AUG - HEAD_DIM, N_HEADS * nq), lambda b, j: (0, 0)),
        ],
        out_specs=pl.BlockSpec((GROUP, ATTN_WIDTH, nq), qblk),
        out_shape=jax.ShapeDtypeStruct((batch, ATTN_WIDTH, seq), jnp.bfloat16),
        scratch_shapes=[
            pltpu.VMEM((AUG, wide), jnp.bfloat16),
            pltpu.VMEM((AUG, wide), jnp.bfloat16),
            pltpu.VMEM((seq, GROUP * nq), jnp.float32),
            pltpu.VMEM((seq, GROUP * nq), jnp.bfloat16),
            pltpu.VMEM((VT_ROWS, wide), jnp.float32),
            pltpu.VMEM((KEY_CHUNK, wide), jnp.float32),
            pltpu.VMEM((KEY_CHUNK, wide), jnp.float32),
        ],
        compiler_params=pltpu.CompilerParams(
            dimension_semantics=("arbitrary", "arbitrary"), vmem_limit_bytes=VMEM_LIMIT),
        name="attn",
    )(qt, qit, wit, vt, kaug, kiaug, qaug)


def _mix_kernel(x_ref, at_ref, u_ref, halo_ref, g_ref, wpg_ref, pscale_ref, wba_ref, wbp_ref,
                wout_ref, gffn_ref, wg_ref, wu_ref, wd_ref, o_ref, ubuf, *, tm, seq):
    i = pl.program_id(0)
    tiles_per_seq = seq // tm
    first = (i % tiles_per_seq) == 0
    halo = halo_ref[...]
    ubuf[0:HALO, :] = jnp.where(first, jnp.zeros_like(halo), halo)
    ubuf[HALO:HALO + tm, :] = u_ref[...]

    t_pos = (i * tm) % seq + lax.broadcasted_iota(jnp.int32, (tm, POOL_GROUP_WIDTH), 0)
    pooled = []
    for g, w in enumerate(POOL_WINDOWS):
        lanes = slice(g * POOL_GROUP_WIDTH, (g + 1) * POOL_GROUP_WIDTH)
        levels = w.bit_length() - 1
        cur = ubuf[:, lanes]
        step = 1
        for _ in range(levels):
            cur = cur + pltpu.roll(cur, step, axis=0)
            step *= 2
        wsum = cur[HALO:, :]
        cnt = jnp.minimum(t_pos + 1, w).astype(jnp.float32)
        d = (wsum / cnt - ubuf[HALO:HALO + tm, lanes]).astype(jnp.bfloat16)
        pooled.append(jnp.dot(d, wpg_ref[g], preferred_element_type=jnp.float32))
    p = (jnp.concatenate(pooled, axis=-1) * pscale_ref[...]).astype(jnp.bfloat16)

    a_proj = lax.dot_general(at_ref[0], wba_ref[...], (((0,), (0,)), ((), ())),
                             preferred_element_type=jnp.float32)
    p_proj = jnp.dot(p, wbp_ref[...], preferred_element_type=jnp.float32)
    gates = g_ref[...].astype(jnp.float32)
    merged = (jax.nn.sigmoid(gates[:, :D_MODEL]) * a_proj
              + jax.nn.sigmoid(gates[:, D_MODEL:]) * p_proj)
    x1 = x_ref[...] + jnp.dot(merged.astype(jnp.bfloat16), wout_ref[...],
                              preferred_element_type=jnp.float32)

    h2 = (_rms(x1) * gffn_ref[...]).astype(jnp.bfloat16)
    gate = jnp.dot(h2, wg_ref[...], preferred_element_type=jnp.float32)
    up = jnp.dot(h2, wu_ref[...], preferred_element_type=jnp.float32)
    act = (jax.nn.silu(gate) * up).astype(jnp.bfloat16)
    o_ref[...] = x1 + jnp.dot(act, wd_ref[...], preferred_element_type=jnp.float32)


def _mix(x2, at, u, gates, wpg, pscale, wba, wbp, wout, gffn, wg, wu, wd, *, batch, seq, tm):
    n = batch * seq
    tiles_per_seq = seq // tm
    row = lambda i: (i, 0)
    const2 = lambda i: (0, 0)
    resident = functools.partial(pl.BlockSpec, pipeline_mode=pl.Buffered(1))
    return pl.pallas_call(
        functools.partial(_mix_kernel, tm=tm, seq=seq),
        grid=(n // tm,),
        in_specs=[
            pl.BlockSpec((tm, D_MODEL), row),
            pl.BlockSpec((1, ATTN_WIDTH, tm), lambda i: (i // tiles_per_seq, 0, i % tiles_per_seq)),
            pl.BlockSpec((tm, POOL_WIDTH), row),
            pl.BlockSpec((HALO, POOL_WIDTH), lambda i: (jnp.maximum(i * (tm // HALO) - 1, 0), 0)),
            pl.BlockSpec((tm, 2 * D_MODEL), row),
            resident((len(POOL_WINDOWS), POOL_GROUP_WIDTH, POOL_GROUP_WIDTH), lambda i: (0, 0, 0)),
            resident((1, POOL_WIDTH), const2),
            resident((ATTN_WIDTH, D_MODEL), const2),
            resident((POOL_WIDTH, D_MODEL), const2),
            resident((D_MODEL, D_MODEL), const2),
            resident((1, D_MODEL), const2),
            resident((D_MODEL, D_FF), const2),
            resident((D_MODEL, D_FF), const2),
            resident((D_FF, D_MODEL), const2),
        ],
        out_specs=pl.BlockSpec((tm, D_MODEL), row),
        out_shape=jax.ShapeDtypeStruct((n, D_MODEL), jnp.float32),
        scratch_shapes=[pltpu.VMEM((tm + HALO, POOL_WIDTH), jnp.float32)],
        compiler_params=pltpu.CompilerParams(
            dimension_semantics=("arbitrary",), vmem_limit_bytes=VMEM_LIMIT),
        name="mix",
    )(x2, at, u, u, gates, wpg, pscale, wba, wbp, wout, gffn, wg, wu, wd)


def _alibi_rows():
    slopes = 2.0 ** (-8.0 * jnp.arange(1, N_HEADS + 1, dtype=jnp.float32) / N_HEADS) * LOG2E
    parts, rest = [], slopes
    for _ in range(SLOPE_PARTS):
        term = rest.astype(jnp.bfloat16)
        parts.append(term)
        rest = rest - term.astype(jnp.float32)
    block = jnp.stack(parts * 2, axis=0)
    block = jnp.repeat(block, Q_BLOCK, axis=1)
    pad = jnp.zeros((AUG - HEAD_DIM - 2 * SLOPE_PARTS, N_HEADS * Q_BLOCK), jnp.bfloat16)
    return jnp.concatenate([block, pad], axis=0)


def _layer(x2, norm_mix, w_in, q_norm, k_norm, w_pool_group, pool_scale, w_branch_attn,
           w_branch_pool, w_out, norm_ffn, w_ffn_gate, w_ffn_up, w_ffn_down, *, batch, seq):
    bf = jnp.bfloat16
    gk = jnp.concatenate([k_norm, jnp.zeros((LANES - HEAD_DIM,), k_norm.dtype)])[None, :]

    qt, qit, wit, vt, kaug, kiaug, u, gates = _proj(
        x2, norm_mix[None, :], w_in, q_norm[:, None], gk, batch=batch, seq=seq, tm=PROJ_ROWS)
    at = _attn(qt, qit, wit, vt, kaug, kiaug, _alibi_rows(), batch=batch, seq=seq)
    return _mix(x2, at, u, gates, w_pool_group.astype(bf), pool_scale[None, :],
                w_branch_attn.astype(bf), w_branch_pool.astype(bf), w_out.astype(bf),
                norm_ffn[None, :], w_ffn_gate.astype(bf), w_ffn_up.astype(bf),
                w_ffn_down.astype(bf), batch=batch, seq=seq, tm=MIX_ROWS)


def kernel(x, norm_mix, w_in, q_norm, k_norm, w_pool_group, pool_scale, w_branch_attn,
           w_branch_pool, w_out, norm_ffn, w_ffn_gate, w_ffn_up, w_ffn_down):
    batch, seq, d = x.shape
    x2 = x.reshape(batch * seq, d)
    w_in_t = jnp.swapaxes(w_in, 1, 2)
    for l in range(norm_mix.shape[0]):
        x2 = _layer(x2, norm_mix[l], (w_in_t, l), q_norm[l], k_norm[l], w_pool_group[l],
                    pool_scale[l], w_branch_attn[l], w_branch_pool[l], w_out[l], norm_ffn[l],
                    w_ffn_gate[l], w_ffn_up[l], w_ffn_down[l], batch=batch, seq=seq)
    return x2.reshape(batch, seq, d)
```

```python
import functools

import jax
import jax.numpy as jnp
from jax import lax
from jax.experimental import pallas as pl
from jax.experimental.pallas import tpu as pltpu

D_MODEL = 1024
N_HEADS = 8
HEAD_DIM = 64
ATTN_WIDTH = N_HEADS * HEAD_DIM
IDX_HEADS = 8
IDX_DIM = 64
IDX_SCALE = (IDX_HEADS ** -0.5) * (IDX_DIM ** -0.5)
TOPK = 256
POOL_WINDOWS = (2, 4, 8, 16)
POOL_WIDTH = 512
POOL_GROUP_WIDTH = 128
D_FF = 2816
RMS_EPS = 1e-6
NEG_BIG = -1e30

LANES = 128
PROJ_ROWS = 1024
MIX_ROWS = 512
Q_BLOCK = 256
GROUP = 4
KEY_CHUNK = 256
AUG = 128
VT_ROWS = 80
HALO = 16
LOG2E = 1.4426950408889634
SLOPE_PARTS = 3
PACK_ROWS = 16
BF16_STEP = 2.0 ** -7
BF16_EXACT_INT = 256
BISECT16_STEPS = 10
BISECT_STEPS = 6
TINY = 1e-30
VMEM_LIMIT = 56 * 1024 * 1024

T_Q0, T_V0, T_QI0, T_WI0, T_ROWS = 0, 512, 576, 1088, 1104
N_KK0, N_U0, N_G0, N_COLS = 0, 128, 640, 2688


def _rms(x, eps=RMS_EPS):
    return x * lax.rsqrt(jnp.mean(x * x, axis=-1, keepdims=True) + eps)


def _regroup_w_in(w_ref, wt_ref, wn_ref):
    bf = jnp.bfloat16
    wt_ref[T_Q0:T_V0, :] = w_ref[0:512, :].astype(bf)
    wt_ref[T_V0:T_WI0, :] = w_ref[576:1152, :].astype(bf)
    tail = jnp.concatenate(
        [w_ref[1216:1224, :], jnp.zeros((T_ROWS - T_WI0 - IDX_HEADS, D_MODEL), jnp.float32)],
        axis=0)
    wt_ref[T_WI0:T_ROWS, :] = tail.astype(bf)
    k_ki = jnp.concatenate([w_ref[512:576, :], w_ref[1152:1216, :]], axis=0)
    wn_ref[:, N_KK0:N_KK0 + LANES] = k_ki.T.astype(bf)
    wn_ref[:, N_U0:N_COLS] = w_ref[1224:3784, :].T.astype(bf)


def _proj_kernel(x_ref, gmix_ref, w_ref, gq_ref, gk_ref,
                 qt_ref, qit_ref, wit_ref, vt_ref, kaug_ref, kiaug_ref, u_ref, g_ref,
                 wt_ref, wn_ref, *, tm, seq):
    i = pl.program_id(0)

    @pl.when(i == 0)
    def _():
        _regroup_w_in(w_ref, wt_ref, wn_ref)

    x = x_ref[...]
    h = (_rms(x) * gmix_ref[...]).astype(jnp.bfloat16)
    y_t = lax.dot_general(wt_ref[...], h, (((1,), (1,)), ((), ())),
                          preferred_element_type=jnp.float32)
    y_n = jnp.dot(h, wn_ref[...], preferred_element_type=jnp.float32)

    gq = gq_ref[...]
    for hd in range(N_HEADS):
        blk = y_t[T_Q0 + hd * HEAD_DIM:T_Q0 + (hd + 1) * HEAD_DIM, :]
        ss = jnp.mean(blk * blk, axis=0, keepdims=True)
        qn = blk * lax.rsqrt(ss + RMS_EPS) * gq * (HEAD_DIM ** -0.5 * LOG2E)
        qt_ref[0, hd * HEAD_DIM:(hd + 1) * HEAD_DIM, :] = qn.astype(qt_ref.dtype)

    qit_ref[0] = y_t[T_QI0:T_QI0 + IDX_HEADS * IDX_DIM, :].astype(qit_ref.dtype)
    wit_ref[0] = y_t[T_WI0:T_WI0 + IDX_HEADS, :] * IDX_SCALE

    v_t = y_t[T_V0:T_V0 + HEAD_DIM, :]
    row = lax.broadcasted_iota(jnp.int32, (VT_ROWS - HEAD_DIM, tm), 0)
    tail = jnp.where(row == 0, 1.0, 0.0)
    vt_full = jnp.concatenate([v_t, tail], axis=0).astype(vt_ref.dtype)
    for cc in range(tm // KEY_CHUNK):
        vt_ref[0, cc] = vt_full[:, cc * KEY_CHUNK:(cc + 1) * KEY_CHUNK]

    kk = y_n[:, N_KK0:N_KK0 + LANES]
    lane = lax.broadcasted_iota(jnp.int32, (tm, LANES), 1)
    is_k = lane < HEAD_DIM
    ss = jnp.sum(jnp.where(is_k, kk * kk, 0.0), axis=-1, keepdims=True) * (1.0 / HEAD_DIM)
    kn = kk * lax.rsqrt(ss + RMS_EPS) * gk_ref[...]
    pos = (i * tm) % seq + lax.broadcasted_iota(jnp.int32, (tm, LANES), 0)
    pos_lo = pos & (BF16_EXACT_INT - 1)
    pos_hi = (pos - pos_lo).astype(jnp.float32)
    pos_lo = pos_lo.astype(jnp.float32)
    feats = jnp.where(lane < HEAD_DIM + SLOPE_PARTS, pos_hi,
                      jnp.where(lane < HEAD_DIM + 2 * SLOPE_PARTS, pos_lo, 0.0))
    kaug_ref[0] = jnp.where(is_k, kn, feats).astype(kaug_ref.dtype)
    kiaug_ref[0] = jnp.where(is_k, 0.0, kk).astype(kiaug_ref.dtype)

    u_ref[...] = y_n[:, N_U0:N_U0 + POOL_WIDTH]
    g_ref[...] = y_n[:, N_G0:N_G0 + 2 * D_MODEL].astype(g_ref.dtype)


def _proj(x2, gmix, w_in_layers, gq, gk, *, batch, seq, tm):
    w_in_t, layer = w_in_layers
    n = batch * seq
    tiles_per_seq = seq // tm
    const = lambda i: (0, 0)
    bidx = lambda i: (i // tiles_per_seq, 0, i % tiles_per_seq)
    out_shape = (
        jax.ShapeDtypeStruct((batch, ATTN_WIDTH, seq), jnp.bfloat16),
        jax.ShapeDtypeStruct((batch, IDX_HEADS * IDX_DIM, seq), jnp.bfloat16),
        jax.ShapeDtypeStruct((batch, IDX_HEADS, seq), jnp.float32),
        jax.ShapeDtypeStruct((batch, seq // KEY_CHUNK, VT_ROWS, KEY_CHUNK), jnp.bfloat16),
        jax.ShapeDtypeStruct((batch, seq, AUG), jnp.bfloat16),
        jax.ShapeDtypeStruct((batch, seq, AUG), jnp.bfloat16),
        jax.ShapeDtypeStruct((n, POOL_WIDTH), jnp.float32),
        jax.ShapeDtypeStruct((n, 2 * D_MODEL), jnp.bfloat16),
    )
    out_specs = (
        pl.BlockSpec((1, ATTN_WIDTH, tm), bidx),
        pl.BlockSpec((1, IDX_HEADS * IDX_DIM, tm), bidx),
        pl.BlockSpec((1, IDX_HEADS, tm), bidx),
        pl.BlockSpec((1, tm // KEY_CHUNK, VT_ROWS, KEY_CHUNK),
                     lambda i: (i // tiles_per_seq, i % tiles_per_seq, 0, 0)),
        pl.BlockSpec((1, tm, AUG), lambda i: (i // tiles_per_seq, i % tiles_per_seq, 0)),
        pl.BlockSpec((1, tm, AUG), lambda i: (i // tiles_per_seq, i % tiles_per_seq, 0)),
        pl.BlockSpec((tm, POOL_WIDTH), lambda i: (i, 0)),
        pl.BlockSpec((tm, 2 * D_MODEL), lambda i: (i, 0)),
    )
    in_specs = [
        pl.BlockSpec((tm, D_MODEL), lambda i: (i, 0)),
        pl.BlockSpec((1, D_MODEL), const),
        pl.BlockSpec((None,) + w_in_t.shape[1:], lambda i: (layer, 0, 0),
                     pipeline_mode=pl.Buffered(1)),
        pl.BlockSpec((HEAD_DIM, 1), const),
        pl.BlockSpec((1, LANES), const),
    ]
    return pl.pallas_call(
        functools.partial(_proj_kernel, tm=tm, seq=seq),
        grid=(n // tm,),
        in_specs=in_specs,
        out_specs=out_specs,
        out_shape=out_shape,
        scratch_shapes=[
            pltpu.VMEM((T_ROWS, D_MODEL), jnp.bfloat16),
            pltpu.VMEM((D_MODEL, N_COLS), jnp.bfloat16),
        ],
        compiler_params=pltpu.CompilerParams(
            dimension_semantics=("arbitrary",), vmem_limit_bytes=VMEM_LIMIT),
        name="proj",
    )(x2, gmix, w_in_t, gq, gk)


def _attn_kernel(qt_ref, qit_ref, wit_ref, vt_ref, kaug_ref, kiaug_ref, qaug_ref,
                 at_ref, rhs_idx, rhs_att, sc_ref, sc16_ref, acc_t, buf_a, buf_b):
    j = pl.program_id(1)
    nq = Q_BLOCK
    kc = KEY_CHUNK
    nch = j + 1
    lanes = GROUP * nq
    tiles = GROUP * N_HEADS
    inf = jnp.float32(jnp.inf)

    def cols(t):
        return slice(t * nq, (t + 1) * nq)

    zeros64 = jnp.zeros((HEAD_DIM, nq), rhs_idx.dtype)
    for t in range(tiles):
        g, hd = divmod(t, N_HEADS)
        rows = slice(hd * HEAD_DIM, (hd + 1) * HEAD_DIM)
        rhs_idx[0:HEAD_DIM, cols(t)] = zeros64
        rhs_idx[HEAD_DIM:AUG, cols(t)] = qit_ref[g, rows, :]
        rhs_att[0:HEAD_DIM, cols(t)] = qt_ref[g, rows, :]
        rhs_att[HEAD_DIM:AUG, cols(t)] = qaug_ref[:, cols(hd)]

    t_row = j * nq + lax.broadcasted_iota(jnp.int32, (1, lanes), 1) % nq
    t_pos = j * nq + lax.broadcasted_iota(jnp.int32, (kc, nq), 1)
    s_loc = lax.broadcasted_iota(jnp.int32, (kc, nq), 0)

    def pairwise(step, n, carry):
        carry = lax.fori_loop(0, n // 2, lambda i, cr: step(2 * i + 1, step(2 * i, cr)), carry)
        return lax.cond(n % 2 == 1, lambda cr: step(n - 1, cr), lambda cr: cr, carry)

    def fold(x, op, rows=8):
        parts = [x[r * rows:(r + 1) * rows, :] for r in range(kc // rows)]
        while len(parts) > 1:
            parts = [op(parts[i], parts[i + 1]) for i in range(0, len(parts), 2)]
        return parts[0]

    bufs = (buf_a, buf_b)

    def pipelined(produce, consume, finish, carry, prepare=lambda c: None, first=None,
                  while_draining=None):
        def drain(c, par, carry, aux):
            part = None
            for t in range(tiles):
                if while_draining is not None:
                    while_draining(t, bufs[1 - par])
                part = consume(c, bufs[par], t, carry, aux, part)
            return finish(c, carry, part)

        def both(c, par, carry, aux):
            part, nxt, ctx = None, [], prepare(c + 1)
            for t in range(tiles):
                nxt.append(produce(c + 1, bufs[1 - par], t, ctx))
                part = consume(c, bufs[par], t, carry, aux, part)
            return finish(c, carry, part), tuple(nxt)

        def pair_body(i, st):
            return both(2 * i + 1, 1, *both(2 * i, 0, *st))

        if first is None:
            ctx0 = prepare(0)
            first = tuple(produce(0, bufs[0], t, ctx0) for t in range(tiles))
        st = lax.fori_loop(0, (nch - 1) // 2, pair_body, (carry, first))

        def even_tail(st):
            return drain(nch - 1, 1, *both(nch - 2, 0, *st))

        return lax.cond(nch % 2 == 0, even_tail, lambda st: drain(nch - 1, 0, *st), st)

    def score_mm(c, buf, t, _):
        off = pl.multiple_of(c * kc, kc)
        buf[:, cols(t)] = jnp.dot(kiaug_ref[t // N_HEADS, pl.ds(off, kc), :], rhs_idx[:, cols(t)],
                                  preferred_element_type=jnp.float32)
        return jnp.int32(0)

    def score_head(c, buf, t, lo_hi, _, part):
        g, hd = divmod(t, N_HEADS)
        tot, los, his = part or (None, (), ())
        term = jnp.maximum(buf[:, cols(t)], 0.0) * wit_ref[g, hd:hd + 1, :]
        tot = term if hd == 0 else tot + term
        if hd < N_HEADS - 1:
            return tot, los, his
        off = pl.multiple_of(c * kc, kc)
        causal = off + s_loc <= t_pos
        masked = jnp.where(causal, tot, -inf)
        sc_ref[pl.ds(off, kc), cols(g)] = masked
        sc16_ref[pl.ds(off, kc), cols(g)] = masked.astype(jnp.bfloat16)
        return (None, los + (fold(jnp.where(causal, tot, inf), jnp.minimum),),
                his + (fold(masked, jnp.maximum),))

    def score_store(c, lo_hi, part):
        lo, hi = lo_hi
        return (jnp.minimum(lo, jnp.concatenate(part[1], axis=1)),
                jnp.maximum(hi, jnp.concatenate(part[2], axis=1)))

    def raw_logits0(t, buf):
        buf[:, cols(t)] = jnp.dot(kaug_ref[t // N_HEADS, 0:kc, :], rhs_att[:, cols(t)],
                                  preferred_element_type=jnp.float32)

    lo, hi = pipelined(score_mm, score_head, score_store,
                       (jnp.full((8, lanes), inf, jnp.float32),
                        jnp.full((8, lanes), -inf, jnp.float32)),
                       while_draining=raw_logits0)
    lo = jnp.min(lo, axis=0, keepdims=True)
    hi = jnp.max(hi, axis=0, keepdims=True)

    def by_lane_tile(ref, c, fn):
        off = pl.multiple_of(c * kc, kc)
        return jnp.concatenate(
            [fn(ref[pl.ds(off, kc), l * LANES:(l + 1) * LANES], slice(l * LANES, (l + 1) * LANES))
             for l in range(lanes // LANES)], axis=1)

    def count_ge(v):
        def step(c, acc):
            return acc + by_lane_tile(
                sc_ref, c, lambda s, ln: fold(jnp.where(s >= v[:, ln], 1.0, 0.0), jnp.add))
        acc = pairwise(step, nch, jnp.zeros((8, lanes), jnp.float32))
        return jnp.sum(acc, axis=0, keepdims=True)

    def count_ge16(v):
        v16 = v.astype(jnp.bfloat16)
        one, zero = jnp.ones((), jnp.bfloat16), jnp.zeros((), jnp.bfloat16)
        def step(c, acc):
            return acc + by_lane_tile(
                sc16_ref, c,
                lambda s, ln: fold(jnp.where(s >= v16[:, ln], one, zero), jnp.add, PACK_ROWS))
        acc = pairwise(step, nch, jnp.zeros((PACK_ROWS, lanes), jnp.bfloat16))
        return jnp.sum(acc.astype(jnp.float32), axis=0, keepdims=True)

    def max_below(v):
        def step(c, acc):
            return jnp.maximum(acc, by_lane_tile(
                sc_ref, c, lambda s, ln: fold(jnp.where(s < v[:, ln], s, -inf), jnp.maximum)))
        acc = pairwise(step, nch, jnp.full((8, lanes), -inf, jnp.float32))
        return jnp.max(acc, axis=0, keepdims=True)

    def n_open(done):
        return jnp.sum(1 - done)

    def snap(state):
        lo, hi, thr, cge, done = state
        b = max_below(hi)
        c = count_ge(b)
        ok = c >= TOPK
        newly = ok & (done == 0)
        return (lo, jnp.where(ok, hi, b), jnp.where(newly, b, thr),
                jnp.where(newly, c, cge), jnp.where(ok, 1, done))

    def bisect(state):
        lo, hi, thr, cge, done = state
        mid = 0.5 * lo + 0.5 * hi
        ge = count_ge(mid) >= TOPK
        return (jnp.where(ge, mid, lo), jnp.where(ge, hi, mid), thr, cge, done)

    def bisect16(_, lo_hi):
        lo, hi = lo_hi
        mid = (0.5 * lo + 0.5 * hi).astype(jnp.bfloat16).astype(jnp.float32)
        ge = count_ge16(mid) >= TOPK
        return jnp.where(ge, mid, lo), jnp.where(ge, hi, mid)

    short = jnp.where((t_row + 1) <= TOPK, 1, 0)
    n0 = n_open(short)
    run = n0 > 0
    lo16 = lo.astype(jnp.bfloat16).astype(jnp.float32)
    hi16 = hi.astype(jnp.bfloat16).astype(jnp.float32)
    hi16 = (hi16 + jnp.abs(hi16) * (2 * BF16_STEP) + TINY).astype(jnp.bfloat16).astype(jnp.float32)
    lo16, hi16 = lax.fori_loop(0, jnp.where(run, BISECT16_STEPS, 0), bisect16, (lo16, hi16))
    lo32 = jnp.maximum(lo, lo16 - jnp.abs(lo16) * BF16_STEP - TINY)
    state = (lo32, hi16, lo, jnp.zeros((1, lanes), jnp.float32), short)
    state = lax.fori_loop(0, jnp.where(run, BISECT_STEPS, 0), lambda _, s: bisect(s), state)
    state = lax.cond(run, snap, lambda s: s, state)

    def finish_body(carry):
        s = snap(bisect(carry[0]))
        return s, n_open(s[4])

    state, _ = lax.while_loop(lambda carry: carry[1] > 0, finish_body, (state, n_open(state[4])))
    thr, cge = state[2], state[3]

    @pl.when(jnp.max(cge) > TOPK)
    def _():
        def gt_body(c, acc):
            off = pl.multiple_of(c * kc, kc)
            return acc + fold(jnp.where(sc_ref[pl.ds(off, kc), :] > thr, 1.0, 0.0), jnp.add)
        cnt_gt = jnp.sum(lax.fori_loop(0, nch, gt_body, jnp.zeros((8, lanes), jnp.float32)),
                         axis=0, keepdims=True)
        need = TOPK - cnt_gt
        r_i = lax.broadcasted_iota(jnp.int32, (kc, kc), 0)
        c_i = lax.broadcasted_iota(jnp.int32, (kc, kc), 1)
        tri = jnp.where(c_i <= r_i, 1.0, 0.0).astype(jnp.bfloat16)

        def tie_body(c, seen):
            off = pl.multiple_of(c * kc, kc)
            s = sc_ref[pl.ds(off, kc), :]
            eq = s == thr
            eq_f = jnp.where(eq, 1.0, 0.0)
            incl = jnp.dot(tri, eq_f.astype(jnp.bfloat16), preferred_element_type=jnp.float32)
            rank = incl - eq_f + seen
            sc_ref[pl.ds(off, kc), :] = jnp.where(eq & (rank >= need), -inf, s)
            return seen + incl[kc - 1:kc, :]

        lax.fori_loop(0, nch, tie_body, jnp.zeros((1, lanes), jnp.float32))

    acc_t[...] = jnp.zeros_like(acc_t)

    def select_bias(c):
        off = pl.multiple_of(c * kc, kc)
        return jnp.where(sc_ref[pl.ds(off, kc), :] >= thr, 0.0, NEG_BIG)

    def masked_logits(c, buf, t, bias):
        g = t // N_HEADS
        off = pl.multiple_of(c * kc, kc)
        lm = jnp.dot(kaug_ref[g, pl.ds(off, kc), :], rhs_att[:, cols(t)],
                     preferred_element_type=jnp.float32) + bias[:, cols(g)]
        buf[:, cols(t)] = lm
        return jnp.max(fold(lm, jnp.maximum), axis=0, keepdims=True)

    def softmax_pv(c, buf, t, m_old, cmax, m_list):
        m_new = jnp.maximum(m_old[t], cmax[t])
        alpha = jnp.exp2(m_old[t] - m_new)
        p = jnp.exp2((buf[:, cols(t)] - m_new).astype(jnp.bfloat16))
        pv = jnp.dot(vt_ref[t // N_HEADS, c], p, preferred_element_type=jnp.float32)
        acc_t[:, cols(t)] = acc_t[:, cols(t)] * alpha + pv
        return (m_list or ()) + (m_new,)

    def mask_chunk0(raw):
        bias = select_bias(0)
        cmax = []
        for t in range(tiles):
            lm = raw[:, cols(t)] + bias[:, cols(t // N_HEADS)]
            bufs[0][:, cols(t)] = lm
            cmax.append(jnp.max(fold(lm, jnp.maximum), axis=0, keepdims=True))
        return tuple(cmax)

    first = lax.cond(nch % 2 == 0, lambda: mask_chunk0(bufs[0]), lambda: mask_chunk0(bufs[1]))
    pipelined(masked_logits, softmax_pv, lambda c, m_old, m_list: m_list,
              tuple(jnp.full((1, nq), NEG_BIG, jnp.float32) for _ in range(tiles)),
              prepare=select_bias, first=first)

    for t in range(tiles):
        g, hd = divmod(t, N_HEADS)
        inv_l = 1.0 / acc_t[HEAD_DIM:HEAD_DIM + 1, cols(t)]
        at_ref[g, hd * HEAD_DIM:(hd + 1) * HEAD_DIM, :] = (
            acc_t[0:HEAD_DIM, cols(t)] * inv_l).astype(at_ref.dtype)


def _attn(qt, qit, wit, vt, kaug, kiaug, qaug, *, batch, seq):
    nq = Q_BLOCK
    assert KEY_CHUNK == Q_BLOCK and seq % nq == 0 and batch % GROUP == 0
    wide = GROUP * N_HEADS * nq
    qblk = lambda b, j: (b, 0, j)
    whole = lambda b, j: (b, 0, 0)
    return pl.pallas_call(
        _attn_kernel,
        grid=(batch // GROUP, seq // nq),
        in_specs=[
            pl.BlockSpec((GROUP, ATTN_WIDTH, nq), qblk),
            pl.BlockSpec((GROUP, IDX_HEADS * IDX_DIM, nq), qblk),
            pl.BlockSpec((GROUP, IDX_HEADS, nq), qblk),
            pl.BlockSpec((GROUP, seq // KEY_CHUNK, VT_ROWS, KEY_CHUNK),
                         lambda b, j: (b, 0, 0, 0)),
            pl.BlockSpec((GROUP, seq, AUG), whole),
            pl.BlockSpec((GROUP, seq, AUG), whole),
            pl.BlockSpec((AUG - HEAD_DIM, N_HEADS * nq), lambda b, j: (0, 0)),
        ],
        out_specs=pl.BlockSpec((GROUP, ATTN_WIDTH, nq), qblk),
        out_shape=jax.ShapeDtypeStruct((batch, ATTN_WIDTH, seq), jnp.bfloat16),
        scratch_shapes=[
            pltpu.VMEM((AUG, wide), jnp.bfloat16),
            pltpu.VMEM((AUG, wide), jnp.bfloat16),
            pltpu.VMEM((seq, GROUP * nq), jnp.float32),
            pltpu.VMEM((seq, GROUP * nq), jnp.bfloat16),
            pltpu.VMEM((VT_ROWS, wide), jnp.float32),
            pltpu.VMEM((KEY_CHUNK, wide), jnp.float32),
            pltpu.VMEM((KEY_CHUNK, wide), jnp.float32),
        ],
        compiler_params=pltpu.CompilerParams(
            dimension_semantics=("arbitrary", "arbitrary"), vmem_limit_bytes=VMEM_LIMIT),
        name="attn",
    )(qt, qit, wit, vt, kaug, kiaug, qaug)


def _mix_kernel(x_ref, at_ref, u_ref, halo_ref, g_ref, wpg_ref, pscale_ref, wba_ref, wbp_ref,
                wout_ref, gffn_ref, wg_ref, wu_ref, wd_ref, o_ref, ubuf, *, tm, seq):
    i = pl.program_id(0)
    tiles_per_seq = seq // tm
    first = (i % tiles_per_seq) == 0
    halo = halo_ref[...]
    ubuf[0:HALO, :] = jnp.where(first, jnp.zeros_like(halo), halo)
    ubuf[HALO:HALO + tm, :] = u_ref[...]

    t_pos = (i * tm) % seq + lax.broadcasted_iota(jnp.int32, (tm, POOL_GROUP_WIDTH), 0)
    pooled = []
    for g, w in enumerate(POOL_WINDOWS):
        lanes = slice(g * POOL_GROUP_WIDTH, (g + 1) * POOL_GROUP_WIDTH)
        levels = w.bit_length() - 1
        cur = ubuf[:, lanes]
        step = 1
        for _ in range(levels):
            cur = cur + pltpu.roll(cur, step, axis=0)
            step *= 2
        wsum = cur[HALO:, :]
        cnt = jnp.minimum(t_pos + 1, w).astype(jnp.float32)
        d = (wsum / cnt - ubuf[HALO:HALO + tm, lanes]).astype(jnp.bfloat16)
        pooled.append(jnp.dot(d, wpg_ref[g], preferred_element_type=jnp.float32))
    p = (jnp.concatenate(pooled, axis=-1) * pscale_ref[...]).astype(jnp.bfloat16)

    a_proj = lax.dot_general(at_ref[0], wba_ref[...], (((0,), (0,)), ((), ())),
                             preferred_element_type=jnp.float32)
    p_proj = jnp.dot(p, wbp_ref[...], preferred_element_type=jnp.float32)
    gates = g_ref[...].astype(jnp.float32)
    merged = (jax.nn.sigmoid(gates[:, :D_MODEL]) * a_proj
              + jax.nn.sigmoid(gates[:, D_MODEL:]) * p_proj)
    x1 = x_ref[...] + jnp.dot(merged.astype(jnp.bfloat16), wout_ref[...],
                              preferred_element_type=jnp.float32)

    h2 = (_rms(x1) * gffn_ref[...]).astype(jnp.bfloat16)
    gate = jnp.dot(h2, wg_ref[...], preferred_element_type=jnp.float32)
    up = jnp.dot(h2, wu_ref[...], preferred_element_type=jnp.float32)
    act = (jax.nn.silu(gate) * up).astype(jnp.bfloat16)
    o_ref[...] = x1 + jnp.dot(act, wd_ref[...], preferred_element_type=jnp.float32)


def _mix(x2, at, u, gates, wpg, pscale, wba, wbp, wout, gffn, wg, wu, wd, *, batch, seq, tm):
    n = batch * seq
    tiles_per_seq = seq // tm
    row = lambda i: (i, 0)
    const2 = lambda i: (0, 0)
    resident = functools.partial(pl.BlockSpec, pipeline_mode=pl.Buffered(1))
    return pl.pallas_call(
        functools.partial(_mix_kernel, tm=tm, seq=seq),
        grid=(n // tm,),
        in_specs=[
            pl.BlockSpec((tm, D_MODEL), row),
            pl.BlockSpec((1, ATTN_WIDTH, tm), lambda i: (i // tiles_per_seq, 0, i % tiles_per_seq)),
            pl.BlockSpec((tm, POOL_WIDTH), row),
            pl.BlockSpec((HALO, POOL_WIDTH), lambda i: (jnp.maximum(i * (tm // HALO) - 1, 0), 0)),
            pl.BlockSpec((tm, 2 * D_MODEL), row),
            resident((len(POOL_WINDOWS), POOL_GROUP_WIDTH, POOL_GROUP_WIDTH), lambda i: (0, 0, 0)),
            resident((1, POOL_WIDTH), const2),
            resident((ATTN_WIDTH, D_MODEL), const2),
            resident((POOL_WIDTH, D_MODEL), const2),
            resident((D_MODEL, D_MODEL), const2),
            resident((1, D_MODEL), const2),
            resident((D_MODEL, D_FF), const2),
            resident((D_MODEL, D_FF), const2),
            resident((D_FF, D_MODEL), const2),
        ],
        out_specs=pl.BlockSpec((tm, D_MODEL), row),
        out_shape=jax.ShapeDtypeStruct((n, D_MODEL), jnp.float32),
        scratch_shapes=[pltpu.VMEM((tm + HALO, POOL_WIDTH), jnp.float32)],
        compiler_params=pltpu.CompilerParams(
            dimension_semantics=("arbitrary",), vmem_limit_bytes=VMEM_LIMIT),
        name="mix",
    )(x2, at, u, u, gates, wpg, pscale, wba, wbp, wout, gffn, wg, wu, wd)


def _alibi_rows():
    slopes = 2.0 ** (-8.0 * jnp.arange(1, N_HEADS + 1, dtype=jnp.float32) / N_HEADS) * LOG2E
    parts, rest = [], slopes
    for _ in range(SLOPE_PARTS):
        term = rest.astype(jnp.bfloat16)
        parts.append(term)
        rest = rest - term.astype(jnp.float32)
    block = jnp.stack(parts * 2, axis=0)
    block = jnp.repeat(block, Q_BLOCK, axis=1)
    pad = jnp.zeros((AUG - HEAD_DIM - 2 * SLOPE_PARTS, N_HEADS * Q_BLOCK), jnp.bfloat16)
    return jnp.concatenate([block, pad], axis=0)


def _layer(x2, norm_mix, w_in, q_norm, k_norm, w_pool_group, pool_scale, w_branch_attn,
           w_branch_pool, w_out, norm_ffn, w_ffn_gate, w_ffn_up, w_ffn_down, *, batch, seq):
    bf = jnp.bfloat16
    gk = jnp.concatenate([k_norm, jnp.zeros((LANES - HEAD_DIM,), k_norm.dtype)])[None, :]

    qt, qit, wit, vt, kaug, kiaug, u, gates = _proj(
        x2, norm_mix[None, :], w_in, q_norm[:, None], gk, batch=batch, seq=seq, tm=PROJ_ROWS)
    at = _attn(qt, qit, wit, vt, kaug, kiaug, _alibi_rows(), batch=batch, seq=seq)
    return _mix(x2, at, u, gates, w_pool_group.astype(bf), pool_scale[None, :],
                w_branch_attn.astype(bf), w_branch_pool.astype(bf), w_out.astype(bf),
                norm_ffn[None, :], w_ffn_gate.astype(bf), w_ffn_up.astype(bf),
                w_ffn_down.astype(bf), batch=batch, seq=seq, tm=MIX_ROWS)


def kernel(x, norm_mix, w_in, q_norm, k_norm, w_pool_group, pool_scale, w_branch_attn,
           w_branch_pool, w_out, norm_ffn, w_ffn_gate, w_ffn_up, w_ffn_down):
    batch, seq, d = x.shape
    x2 = x.reshape(batch * seq, d)
    w_in_t = jnp.swapaxes(w_in, 1, 2)
    for l in range(norm_mix.shape[0]):
        x2 = _layer(x2, norm_mix[l], (w_in_t, l), q_norm[l], k_norm[l], w_pool_group[l],
                    pool_scale[l], w_branch_attn[l], w_branch_pool[l], w_out[l], norm_ffn[l],
                    w_ffn_gate[l], w_ffn_up[l], w_ffn_down[l], batch=batch, seq=seq)
    return x2.reshape(batch, seq, d)
```

```python
import functools

import jax
import jax.numpy as jnp
from jax import lax
from jax.experimental import pallas as pl
from jax.experimental.pallas import tpu as pltpu

D_MODEL = 1024
N_HEADS = 8
HEAD_DIM = 64
ATTN_WIDTH = N_HEADS * HEAD_DIM
IDX_HEADS = 8
IDX_DIM = 64
IDX_SCALE = (IDX_HEADS ** -0.5) * (IDX_DIM ** -0.5)
TOPK = 256
POOL_WINDOWS = (2, 4, 8, 16)
POOL_WIDTH = 512
POOL_GROUP_WIDTH = 128
D_FF = 2816
RMS_EPS = 1e-6
NEG_BIG = -1e30

LANES = 128
PROJ_ROWS = 1024
MIX_ROWS = 512
Q_BLOCK = 256
GROUP = 4
KEY_CHUNK = 256
AUG = 128
VT_ROWS = 80
HALO = 16
LOG2E = 1.4426950408889634
SLOPE_PARTS = 3
PACK_ROWS = 16
BF16_STEP = 2.0 ** -7
BF16_EXACT_INT = 256
BISECT16_STEPS = 10
BISECT_STEPS = 6
TINY = 1e-30
VMEM_LIMIT = 56 * 1024 * 1024

_WIDTHS = (ATTN_WIDTH, HEAD_DIM, HEAD_DIM, IDX_HEADS * IDX_DIM, IDX_DIM, IDX_HEADS,
           POOL_WIDTH, 2 * D_MODEL)
W_Q0, W_K0, W_V0, W_QI0, W_KI0, W_WI0, W_U0, W_G0, W_END = (
    sum(_WIDTHS[:n]) for n in range(len(_WIDTHS) + 1))
T_Q0 = 0
T_V0 = T_Q0 + ATTN_WIDTH
T_QI0 = T_V0 + HEAD_DIM
T_WI0 = T_QI0 + IDX_HEADS * IDX_DIM
T_ROWS = -(-(T_WI0 + IDX_HEADS) // PACK_ROWS) * PACK_ROWS
N_KK0 = 0
N_U0 = N_KK0 + LANES
N_G0 = N_U0 + POOL_WIDTH
N_COLS = N_G0 + 2 * D_MODEL


def _rms(x, eps=RMS_EPS):
    return x * lax.rsqrt(jnp.mean(x * x, axis=-1, keepdims=True) + eps)


def _regroup_w_in(w_ref, wt_ref, wn_ref):
    bf = jnp.bfloat16
    wt_ref[T_Q0:T_V0, :] = w_ref[W_Q0:W_K0, :].astype(bf)
    wt_ref[T_V0:T_WI0, :] = w_ref[W_V0:W_KI0, :].astype(bf)
    tail = jnp.concatenate(
        [w_ref[W_WI0:W_U0, :], jnp.zeros((T_ROWS - T_WI0 - IDX_HEADS, D_MODEL), jnp.float32)],
        axis=0)
    wt_ref[T_WI0:T_ROWS, :] = tail.astype(bf)
    k_ki = jnp.concatenate([w_ref[W_K0:W_V0, :], w_ref[W_KI0:W_WI0, :]], axis=0)
    wn_ref[:, N_KK0:N_KK0 + LANES] = k_ki.T.astype(bf)
    wn_ref[:, N_U0:N_COLS] = w_ref[W_U0:W_END, :].T.astype(bf)


def _proj_kernel(x_ref, gmix_ref, w_ref, gq_ref, gk_ref,
                 qt_ref, qit_ref, wit_ref, vt_ref, kaug_ref, kiaug_ref, u_ref, g_ref,
                 wt_ref, wn_ref, *, tm, seq):
    i = pl.program_id(0)

    @pl.when(i == 0)
    def _():
        _regroup_w_in(w_ref, wt_ref, wn_ref)

    x = x_ref[...]
    h = (_rms(x) * gmix_ref[...]).astype(jnp.bfloat16)
    y_t = lax.dot_general(wt_ref[...], h, (((1,), (1,)), ((), ())),
                          preferred_element_type=jnp.float32)
    y_n = jnp.dot(h, wn_ref[...], preferred_element_type=jnp.float32)

    gq = gq_ref[...]
    for hd in range(N_HEADS):
        blk = y_t[T_Q0 + hd * HEAD_DIM:T_Q0 + (hd + 1) * HEAD_DIM, :]
        ss = jnp.mean(blk * blk, axis=0, keepdims=True)
        qn = blk * lax.rsqrt(ss + RMS_EPS) * gq * (HEAD_DIM ** -0.5 * LOG2E)
        qt_ref[0, hd * HEAD_DIM:(hd + 1) * HEAD_DIM, :] = qn.astype(qt_ref.dtype)

    qit_ref[0] = y_t[T_QI0:T_QI0 + IDX_HEADS * IDX_DIM, :].astype(qit_ref.dtype)
    wit_ref[0] = y_t[T_WI0:T_WI0 + IDX_HEADS, :] * IDX_SCALE

    v_t = y_t[T_V0:T_V0 + HEAD_DIM, :]
    row = lax.broadcasted_iota(jnp.int32, (VT_ROWS - HEAD_DIM, tm), 0)
    tail = jnp.where(row == 0, 1.0, 0.0)
    vt_full = jnp.concatenate([v_t, tail], axis=0).astype(vt_ref.dtype)
    for cc in range(tm // KEY_CHUNK):
        vt_ref[0, cc] = vt_full[:, cc * KEY_CHUNK:(cc + 1) * KEY_CHUNK]

    kk = y_n[:, N_KK0:N_KK0 + LANES]
    lane = lax.broadcasted_iota(jnp.int32, (tm, LANES), 1)
    is_k = lane < HEAD_DIM
    ss = jnp.sum(jnp.where(is_k, kk * kk, 0.0), axis=-1, keepdims=True) * (1.0 / HEAD_DIM)
    kn = kk * lax.rsqrt(ss + RMS_EPS) * gk_ref[...]
    pos = (i * tm) % seq + lax.broadcasted_iota(jnp.int32, (tm, LANES), 0)
    pos_lo = pos & (BF16_EXACT_INT - 1)
    pos_hi = (pos - pos_lo).astype(jnp.float32)
    pos_lo = pos_lo.astype(jnp.float32)
    feats = jnp.where(lane < HEAD_DIM + SLOPE_PARTS, pos_hi,
                      jnp.where(lane < HEAD_DIM + 2 * SLOPE_PARTS, pos_lo, 0.0))
    kaug_ref[0] = jnp.where(is_k, kn, feats).astype(kaug_ref.dtype)
    kiaug_ref[0] = jnp.where(is_k, 0.0, kk).astype(kiaug_ref.dtype)

    u_ref[...] = y_n[:, N_U0:N_U0 + POOL_WIDTH]
    g_ref[...] = y_n[:, N_G0:N_G0 + 2 * D_MODEL].astype(g_ref.dtype)


def _proj(x2, gmix, w_in_layers, gq, gk, *, batch, seq, tm):
    w_in_t, layer = w_in_layers
    n = batch * seq
    tiles_per_seq = seq // tm
    const = lambda i: (0, 0)
    bidx = lambda i: (i // tiles_per_seq, 0, i % tiles_per_seq)
    out_shape = (
        jax.ShapeDtypeStruct((batch, ATTN_WIDTH, seq), jnp.bfloat16),
        jax.ShapeDtypeStruct((batch, IDX_HEADS * IDX_DIM, seq), jnp.bfloat16),
        jax.ShapeDtypeStruct((batch, IDX_HEADS, seq), jnp.float32),
        jax.ShapeDtypeStruct((batch, seq // KEY_CHUNK, VT_ROWS, KEY_CHUNK), jnp.bfloat16),
        jax.ShapeDtypeStruct((batch, seq, AUG), jnp.bfloat16),
        jax.ShapeDtypeStruct((batch, seq, AUG), jnp.bfloat16),
        jax.ShapeDtypeStruct((n, POOL_WIDTH), jnp.float32),
        jax.ShapeDtypeStruct((n, 2 * D_MODEL), jnp.bfloat16),
    )
    out_specs = (
        pl.BlockSpec((1, ATTN_WIDTH, tm), bidx),
        pl.BlockSpec((1, IDX_HEADS * IDX_DIM, tm), bidx),
        pl.BlockSpec((1, IDX_HEADS, tm), bidx),
        pl.BlockSpec((1, tm // KEY_CHUNK, VT_ROWS, KEY_CHUNK),
                     lambda i: (i // tiles_per_seq, i % tiles_per_seq, 0, 0)),
        pl.BlockSpec((1, tm, AUG), lambda i: (i // tiles_per_seq, i % tiles_per_seq, 0)),
        pl.BlockSpec((1, tm, AUG), lambda i: (i // tiles_per_seq, i % tiles_per_seq, 0)),
        pl.BlockSpec((tm, POOL_WIDTH), lambda i: (i, 0)),
        pl.BlockSpec((tm, 2 * D_MODEL), lambda i: (i, 0)),
    )
    in_specs = [
        pl.BlockSpec((tm, D_MODEL), lambda i: (i, 0)),
        pl.BlockSpec((1, D_MODEL), const),
        pl.BlockSpec((None,) + w_in_t.shape[1:], lambda i: (layer, 0, 0),
                     pipeline_mode=pl.Buffered(1)),
        pl.BlockSpec((HEAD_DIM, 1), const),
        pl.BlockSpec((1, LANES), const),
    ]
    return pl.pallas_call(
        functools.partial(_proj_kernel, tm=tm, seq=seq),
        grid=(n // tm,),
        in_specs=in_specs,
        out_specs=out_specs,
        out_shape=out_shape,
        scratch_shapes=[
            pltpu.VMEM((T_ROWS, D_MODEL), jnp.bfloat16),
            pltpu.VMEM((D_MODEL, N_COLS), jnp.bfloat16),
        ],
        compiler_params=pltpu.CompilerParams(
            dimension_semantics=("arbitrary",), vmem_limit_bytes=VMEM_LIMIT),
        name="proj",
    )(x2, gmix, w_in_t, gq, gk)


def _attn_kernel(qt_ref, qit_ref, wit_ref, vt_ref, kaug_ref, kiaug_ref, qaug_ref,
                 at_ref, rhs_idx, rhs_att, sc_ref, sc16_ref, acc_t, buf_a, buf_b):
    j = pl.program_id(1)
    nq = Q_BLOCK
    kc = KEY_CHUNK
    nch = j + 1
    lanes = GROUP * nq
    tiles = GROUP * N_HEADS
    inf = jnp.float32(jnp.inf)

    def cols(t):
        return slice(t * nq, (t + 1) * nq)

    zeros64 = jnp.zeros((HEAD_DIM, nq), rhs_idx.dtype)
    for t in range(tiles):
        g, hd = divmod(t, N_HEADS)
        rows = slice(hd * HEAD_DIM, (hd + 1) * HEAD_DIM)
        rhs_idx[0:HEAD_DIM, cols(t)] = zeros64
        rhs_idx[HEAD_DIM:AUG, cols(t)] = qit_ref[g, rows, :]
        rhs_att[0:HEAD_DIM, cols(t)] = qt_ref[g, rows, :]
        rhs_att[HEAD_DIM:AUG, cols(t)] = qaug_ref[:, cols(hd)]

    t_row = j * nq + lax.broadcasted_iota(jnp.int32, (1, lanes), 1) % nq
    t_pos = j * nq + lax.broadcasted_iota(jnp.int32, (kc, nq), 1)
    s_loc = lax.broadcasted_iota(jnp.int32, (kc, nq), 0)

    def pairwise(step, n, carry):
        carry = lax.fori_loop(0, n // 2, lambda i, cr: step(2 * i + 1, step(2 * i, cr)), carry)
        return lax.cond(n % 2 == 1, lambda cr: step(n - 1, cr), lambda cr: cr, carry)

    def fold(x, op, rows=8):
        parts = [x[r * rows:(r + 1) * rows, :] for r in range(kc // rows)]
        while len(parts) > 1:
            parts = [op(parts[i], parts[i + 1]) for i in range(0, len(parts), 2)]
        return parts[0]

    bufs = (buf_a, buf_b)

    def pipelined(produce, consume, finish, carry, prepare=lambda c: None):
        def drain(c, par, carry, aux):
            part = None
            for t in range(tiles):
                part = consume(c, bufs[par], t, carry, aux, part)
            return finish(c, carry, part)

        def both(c, par, carry, aux):
            part, nxt, ctx = None, [], prepare(c + 1)
            for t in range(tiles):
                nxt.append(produce(c + 1, bufs[1 - par], t, ctx))
                part = consume(c, bufs[par], t, carry, aux, part)
            return finish(c, carry, part), tuple(nxt)

        def pair_body(i, st):
            return both(2 * i + 1, 1, *both(2 * i, 0, *st))

        ctx0 = prepare(0)
        first = tuple(produce(0, bufs[0], t, ctx0) for t in range(tiles))
        st = lax.fori_loop(0, (nch - 1) // 2, pair_body, (carry, first))

        def even_tail(st):
            return drain(nch - 1, 1, *both(nch - 2, 0, *st))

        return lax.cond(nch % 2 == 0, even_tail, lambda st: drain(nch - 1, 0, *st), st)

    def score_step(c, lo_hi):
        off = pl.multiple_of(c * kc, kc)
        causal = off + s_loc <= t_pos
        los, his = [], []
        for g in range(GROUP):
            tot = None
            for hd in range(IDX_HEADS):
                t = g * N_HEADS + hd
                x = jnp.dot(kiaug_ref[g, pl.ds(off, kc), :], rhs_idx[:, cols(t)],
                            preferred_element_type=jnp.float32)
                term = jnp.maximum(x, 0.0) * wit_ref[g, hd:hd + 1, :]
                tot = term if tot is None else tot + term
            masked = jnp.where(causal, tot, -inf)
            sc_ref[pl.ds(off, kc), cols(g)] = masked
            sc16_ref[pl.ds(off, kc), cols(g)] = masked.astype(jnp.bfloat16)
            los.append(fold(jnp.where(causal, tot, inf), jnp.minimum))
            his.append(fold(masked, jnp.maximum))
        lo, hi = lo_hi
        return (jnp.minimum(lo, jnp.concatenate(los, axis=1)),
                jnp.maximum(hi, jnp.concatenate(his, axis=1)))

    lo, hi = pairwise(score_step, nch,
                      (jnp.full((8, lanes), inf, jnp.float32),
                       jnp.full((8, lanes), -inf, jnp.float32)))
    lo = jnp.min(lo, axis=0, keepdims=True)
    hi = jnp.max(hi, axis=0, keepdims=True)

    def by_lane_tile(ref, c, fn):
        off = pl.multiple_of(c * kc, kc)
        return jnp.concatenate(
            [fn(ref[pl.ds(off, kc), l * LANES:(l + 1) * LANES], slice(l * LANES, (l + 1) * LANES))
             for l in range(lanes // LANES)], axis=1)

    def count_ge(v):
        def step(c, acc):
            return acc + by_lane_tile(
                sc_ref, c, lambda s, ln: fold(jnp.where(s >= v[:, ln], 1.0, 0.0), jnp.add))
        acc = pairwise(step, nch, jnp.zeros((8, lanes), jnp.float32))
        return jnp.sum(acc, axis=0, keepdims=True)

    def count_ge16(v):
        v16 = v.astype(jnp.bfloat16)
        one, zero = jnp.ones((), jnp.bfloat16), jnp.zeros((), jnp.bfloat16)
        def step(c, acc):
            return acc + by_lane_tile(
                sc16_ref, c,
                lambda s, ln: fold(jnp.where(s >= v16[:, ln], one, zero), jnp.add, PACK_ROWS))
        acc = pairwise(step, nch, jnp.zeros((PACK_ROWS, lanes), jnp.bfloat16))
        return jnp.sum(acc.astype(jnp.float32), axis=0, keepdims=True)

    def max_below(v):
        def step(c, acc):
            return jnp.maximum(acc, by_lane_tile(
                sc_ref, c, lambda s, ln: fold(jnp.where(s < v[:, ln], s, -inf), jnp.maximum)))
        acc = pairwise(step, nch, jnp.full((8, lanes), -inf, jnp.float32))
        return jnp.max(acc, axis=0, keepdims=True)

    def n_open(done):
        return jnp.sum(1 - done)

    def snap(state):
        lo, hi, thr, cge, done = state
        b = max_below(hi)
        c = count_ge(b)
        ok = c >= TOPK
        newly = ok & (done == 0)
        return (lo, jnp.where(ok, hi, b), jnp.where(newly, b, thr),
                jnp.where(newly, c, cge), jnp.where(ok, 1, done))

    def bisect(state):
        lo, hi, thr, cge, done = state
        mid = 0.5 * lo + 0.5 * hi
        ge = count_ge(mid) >= TOPK
        return (jnp.where(ge, mid, lo), jnp.where(ge, hi, mid), thr, cge, done)

    def bisect16(_, lo_hi):
        lo, hi = lo_hi
        mid = (0.5 * lo + 0.5 * hi).astype(jnp.bfloat16).astype(jnp.float32)
        ge = count_ge16(mid) >= TOPK
        return jnp.where(ge, mid, lo), jnp.where(ge, hi, mid)

    short = jnp.where((t_row + 1) <= TOPK, 1, 0)
    n0 = n_open(short)
    run = n0 > 0
    lo16 = lo.astype(jnp.bfloat16).astype(jnp.float32)
    hi16 = hi.astype(jnp.bfloat16).astype(jnp.float32)
    hi16 = (hi16 + jnp.abs(hi16) * (2 * BF16_STEP) + TINY).astype(jnp.bfloat16).astype(jnp.float32)
    lo16, hi16 = lax.fori_loop(0, jnp.where(run, BISECT16_STEPS, 0), bisect16, (lo16, hi16))
    lo32 = jnp.maximum(lo, lo16 - jnp.abs(lo16) * BF16_STEP - TINY)
    state = (lo32, hi16, lo, jnp.zeros((1, lanes), jnp.float32), short)
    state = lax.fori_loop(0, jnp.where(run, BISECT_STEPS, 0), lambda _, s: bisect(s), state)
    state = lax.cond(run, snap, lambda s: s, state)

    def finish_body(carry):
        s = snap(bisect(carry[0]))
        return s, n_open(s[4])

    state, _ = lax.while_loop(lambda carry: carry[1] > 0, finish_body, (state, n_open(state[4])))
    thr, cge = state[2], state[3]

    @pl.when(jnp.max(cge) > TOPK)
    def _():
        def gt_body(c, acc):
            off = pl.multiple_of(c * kc, kc)
            return acc + fold(jnp.where(sc_ref[pl.ds(off, kc), :] > thr, 1.0, 0.0), jnp.add)
        cnt_gt = jnp.sum(lax.fori_loop(0, nch, gt_body, jnp.zeros((8, lanes), jnp.float32)),
                         axis=0, keepdims=True)
        need = TOPK - cnt_gt
        r_i = lax.broadcasted_iota(jnp.int32, (kc, kc), 0)
        c_i = lax.broadcasted_iota(jnp.int32, (kc, kc), 1)
        tri = jnp.where(c_i <= r_i, 1.0, 0.0).astype(jnp.bfloat16)

        def tie_body(c, seen):
            off = pl.multiple_of(c * kc, kc)
            s = sc_ref[pl.ds(off, kc), :]
            eq = s == thr
            eq_f = jnp.where(eq, 1.0, 0.0)
            incl = jnp.dot(tri, eq_f.astype(jnp.bfloat16), preferred_element_type=jnp.float32)
            rank = incl - eq_f + seen
            sc_ref[pl.ds(off, kc), :] = jnp.where(eq & (rank >= need), -inf, s)
            return seen + incl[kc - 1:kc, :]

        lax.fori_loop(0, nch, tie_body, jnp.zeros((1, lanes), jnp.float32))

    acc_t[...] = jnp.zeros_like(acc_t)

    def select_bias(c):
        off = pl.multiple_of(c * kc, kc)
        return jnp.where(sc_ref[pl.ds(off, kc), :] >= thr, 0.0, NEG_BIG)

    def masked_logits(c, buf, t, bias):
        g = t // N_HEADS
        off = pl.multiple_of(c * kc, kc)
        lm = jnp.dot(kaug_ref[g, pl.ds(off, kc), :], rhs_att[:, cols(t)],
                     preferred_element_type=jnp.float32) + bias[:, cols(g)]
        buf[:, cols(t)] = lm
        return jnp.max(fold(lm, jnp.maximum), axis=0, keepdims=True)

    def softmax_pv(c, buf, t, m_old, cmax, m_list):
        m_new = jnp.maximum(m_old[t], cmax[t])
        alpha = jnp.exp2(m_old[t] - m_new)
        p = jnp.exp2((buf[:, cols(t)] - m_new).astype(jnp.bfloat16))
        pv = jnp.dot(vt_ref[t // N_HEADS, c], p, preferred_element_type=jnp.float32)
        acc_t[:, cols(t)] = acc_t[:, cols(t)] * alpha + pv
        return (m_list or ()) + (m_new,)

    pipelined(masked_logits, softmax_pv, lambda c, m_old, m_list: m_list,
              tuple(jnp.full((1, nq), NEG_BIG, jnp.float32) for _ in range(tiles)),
              prepare=select_bias)

    for t in range(tiles):
        g, hd = divmod(t, N_HEADS)
        inv_l = 1.0 / acc_t[HEAD_DIM:HEAD_DIM + 1, cols(t)]
        at_ref[g, hd * HEAD_DIM:(hd + 1) * HEAD_DIM, :] = (
            acc_t[0:HEAD_DIM, cols(t)] * inv_l).astype(at_ref.dtype)


def _attn(qt, qit, wit, vt, kaug, kiaug, qaug, *, batch, seq):
    nq = Q_BLOCK
    assert KEY_CHUNK == Q_BLOCK and seq % nq == 0 and batch % GROUP == 0
    wide = GROUP * N_HEADS * nq
    qblk = lambda b, j: (b, 0, j)
    whole = lambda b, j: (b, 0, 0)
    return pl.pallas_call(
        _attn_kernel,
        grid=(batch // GROUP, seq // nq),
        in_specs=[
            pl.BlockSpec((GROUP, ATTN_WIDTH, nq), qblk),
            pl.BlockSpec((GROUP, IDX_HEADS * IDX_DIM, nq), qblk),
            pl.BlockSpec((GROUP, IDX_HEADS, nq), qblk),
            pl.BlockSpec((GROUP, seq // KEY_CHUNK, VT_ROWS, KEY_CHUNK),
                         lambda b, j: (b, 0, 0, 0)),
            pl.BlockSpec((GROUP, seq, AUG), whole),
            pl.BlockSpec((GROUP, seq, AUG), whole),
            pl.BlockSpec((AUG - HEAD_DIM, N_HEADS * nq), lambda b, j: (0, 0)),
        ],
        out_specs=pl.BlockSpec((GROUP, ATTN_WIDTH, nq), qblk),
        out_shape=jax.ShapeDtypeStruct((batch, ATTN_WIDTH, seq), jnp.bfloat16),
        scratch_shapes=[
            pltpu.VMEM((AUG, wide), jnp.bfloat16),
            pltpu.VMEM((AUG, wide), jnp.bfloat16),
            pltpu.VMEM((seq, GROUP * nq), jnp.float32),
            pltpu.VMEM((seq, GROUP * nq), jnp.bfloat16),
            pltpu.VMEM((VT_ROWS, wide), jnp.float32),
            pltpu.VMEM((KEY_CHUNK, wide), jnp.float32),
            pltpu.VMEM((KEY_CHUNK, wide), jnp.float32),
        ],
        compiler_params=pltpu.CompilerParams(
            dimension_semantics=("arbitrary", "arbitrary"), vmem_limit_bytes=VMEM_LIMIT),
        name="attn",
    )(qt, qit, wit, vt, kaug, kiaug, qaug)


def _mix_kernel(x_ref, at_ref, u_ref, halo_ref, g_ref, wpg_ref, pscale_ref, wba_ref, wbp_ref,
                wout_ref, gffn_ref, wg_ref, wu_ref, wd_ref, o_ref, ubuf, *, tm, seq):
    i = pl.program_id(0)
    tiles_per_seq = seq // tm
    first = (i % tiles_per_seq) == 0
    halo = halo_ref[...]
    ubuf[0:HALO, :] = jnp.where(first, jnp.zeros_like(halo), halo)
    ubuf[HALO:HALO + tm, :] = u_ref[...]

    t_pos = (i * tm) % seq + lax.broadcasted_iota(jnp.int32, (tm, POOL_GROUP_WIDTH), 0)
    pooled = []
    for g, w in enumerate(POOL_WINDOWS):
        lanes = slice(g * POOL_GROUP_WIDTH, (g + 1) * POOL_GROUP_WIDTH)
        levels = w.bit_length() - 1
        cur = ubuf[:, lanes]
        step = 1
        for _ in range(levels):
            cur = cur + pltpu.roll(cur, step, axis=0)
            step *= 2
        wsum = cur[HALO:, :]
        cnt = jnp.minimum(t_pos + 1, w).astype(jnp.float32)
        d = (wsum / cnt - ubuf[HALO:HALO + tm, lanes]).astype(jnp.bfloat16)
        pooled.append(jnp.dot(d, wpg_ref[g], preferred_element_type=jnp.float32))
    p = (jnp.concatenate(pooled, axis=-1) * pscale_ref[...]).astype(jnp.bfloat16)

    a_proj = lax.dot_general(at_ref[0], wba_ref[...], (((0,), (0,)), ((), ())),
                             preferred_element_type=jnp.float32)
    p_proj = jnp.dot(p, wbp_ref[...], preferred_element_type=jnp.float32)
    gates = g_ref[...].astype(jnp.float32)
    merged = (jax.nn.sigmoid(gates[:, :D_MODEL]) * a_proj
              + jax.nn.sigmoid(gates[:, D_MODEL:]) * p_proj)
    x1 = x_ref[...] + jnp.dot(merged.astype(jnp.bfloat16), wout_ref[...],
                              preferred_element_type=jnp.float32)

    h2 = (_rms(x1) * gffn_ref[...]).astype(jnp.bfloat16)
    gate = jnp.dot(h2, wg_ref[...], preferred_element_type=jnp.float32)
    up = jnp.dot(h2, wu_ref[...], preferred_element_type=jnp.float32)
    act = (jax.nn.silu(gate) * up).astype(jnp.bfloat16)
    o_ref[...] = x1 + jnp.dot(act, wd_ref[...], preferred_element_type=jnp.float32)


def _mix(x2, at, u, gates, wpg, pscale, wba, wbp, wout, gffn, wg, wu, wd, *, batch, seq, tm):
    n = batch * seq
    tiles_per_seq = seq // tm
    row = lambda i: (i, 0)
    const2 = lambda i: (0, 0)
    resident = functools.partial(pl.BlockSpec, pipeline_mode=pl.Buffered(1))
    return pl.pallas_call(
        functools.partial(_mix_kernel, tm=tm, seq=seq),
        grid=(n // tm,),
        in_specs=[
            pl.BlockSpec((tm, D_MODEL), row),
            pl.BlockSpec((1, ATTN_WIDTH, tm), lambda i: (i // tiles_per_seq, 0, i % tiles_per_seq)),
            pl.BlockSpec((tm, POOL_WIDTH), row),
            pl.BlockSpec((HALO, POOL_WIDTH), lambda i: (jnp.maximum(i * (tm // HALO) - 1, 0), 0)),
            pl.BlockSpec((tm, 2 * D_MODEL), row),
            resident((len(POOL_WINDOWS), POOL_GROUP_WIDTH, POOL_GROUP_WIDTH), lambda i: (0, 0, 0)),
            resident((1, POOL_WIDTH), const2),
            resident((ATTN_WIDTH, D_MODEL), const2),
            resident((POOL_WIDTH, D_MODEL), const2),
            resident((D_MODEL, D_MODEL), const2),
            resident((1, D_MODEL), const2),
            resident((D_MODEL, D_FF), const2),
            resident((D_MODEL, D_FF), const2),
            resident((D_FF, D_MODEL), const2),
        ],
        out_specs=pl.BlockSpec((tm, D_MODEL), row),
        out_shape=jax.ShapeDtypeStruct((n, D_MODEL), jnp.float32),
        scratch_shapes=[pltpu.VMEM((tm + HALO, POOL_WIDTH), jnp.float32)],
        compiler_params=pltpu.CompilerParams(
            dimension_semantics=("arbitrary",), vmem_limit_bytes=VMEM_LIMIT),
        name="mix",
    )(x2, at, u, u, gates, wpg, pscale, wba, wbp, wout, gffn, wg, wu, wd)


def _alibi_rows():
    slopes = 2.0 ** (-8.0 * jnp.arange(1, N_HEADS + 1, dtype=jnp.float32) / N_HEADS) * LOG2E
    parts, rest = [], slopes
    for _ in range(SLOPE_PARTS):
        term = rest.astype(jnp.bfloat16)
        parts.append(term)
        rest = rest - term.astype(jnp.float32)
    block = jnp.stack(parts * 2, axis=0)
    block = jnp.repeat(block, Q_BLOCK, axis=1)
    pad = jnp.zeros((AUG - HEAD_DIM - 2 * SLOPE_PARTS, N_HEADS * Q_BLOCK), jnp.bfloat16)
    return jnp.concatenate([block, pad], axis=0)


def _layer(x2, norm_mix, w_in, q_norm, k_norm, w_pool_group, pool_scale, w_branch_attn,
           w_branch_pool, w_out, norm_ffn, w_ffn_gate, w_ffn_up, w_ffn_down, *, batch, seq):
    bf = jnp.bfloat16
    gk = jnp.concatenate([k_norm, jnp.zeros((LANES - HEAD_DIM,), k_norm.dtype)])[None, :]

    qt, qit, wit, vt, kaug, kiaug, u, gates = _proj(
        x2, norm_mix[None, :], w_in, q_norm[:, None], gk, batch=batch, seq=seq, tm=PROJ_ROWS)
    at = _attn(qt, qit, wit, vt, kaug, kiaug, _alibi_rows(), batch=batch, seq=seq)
    return _mix(x2, at, u, gates, w_pool_group.astype(bf), pool_scale[None, :],
                w_branch_attn.astype(bf), w_branch_pool.astype(bf), w_out.astype(bf),
                norm_ffn[None, :], w_ffn_gate.astype(bf), w_ffn_up.astype(bf),
                w_ffn_down.astype(bf), batch=batch, seq=seq, tm=MIX_ROWS)


def kernel(x, norm_mix, w_in, q_norm, k_norm, w_pool_group, pool_scale, w_branch_attn,
           w_branch_pool, w_out, norm_ffn, w_ffn_gate, w_ffn_up, w_ffn_down):
    batch, seq, d = x.shape
    x2 = x.reshape(batch * seq, d)
    w_in_t = jnp.swapaxes(w_in, 1, 2)
    for l in range(norm_mix.shape[0]):
        x2 = _layer(x2, norm_mix[l], (w_in_t, l), q_norm[l], k_norm[l], w_pool_group[l],
                    pool_scale[l], w_branch_attn[l], w_branch_pool[l], w_out[l], norm_ffn[l],
                    w_ffn_gate[l], w_ffn_up[l], w_ffn_down[l], batch=batch, seq=seq)
    return x2.reshape(batch, seq, d)
```

```python
import functools

import jax
import jax.numpy as jnp
from jax import lax
from jax.experimental import pallas as pl
from jax.experimental.pallas import tpu as pltpu

D_MODEL = 1024
N_HEADS = 8
HEAD_DIM = 64
ATTN_WIDTH = N_HEADS * HEAD_DIM
IDX_HEADS = 8
IDX_DIM = 64
IDX_SCALE = (IDX_HEADS ** -0.5) * (IDX_DIM ** -0.5)
TOPK = 256
POOL_WINDOWS = (2, 4, 8, 16)
POOL_WIDTH = 512
POOL_GROUP_WIDTH = 128
D_FF = 2816
RMS_EPS = 1e-6
NEG_BIG = -1e30

LANES = 128
PROJ_ROWS = 1024
MIX_ROWS = 512
Q_BLOCK = 256
GROUP = 4
KEY_CHUNK = 256
AUG = 128
VT_ROWS = 80
HALO = 16
LOG2E = 1.4426950408889634
SLOPE_PARTS = 3
PACK_ROWS = 16
BF16_STEP = 2.0 ** -7
BF16_EXACT_INT = 256
BISECT16_STEPS = 10
BISECT_STEPS = 6
TINY = 1e-30
MAX_FINISH_TRIPS = 512
VMEM_LIMIT = 56 * 1024 * 1024

_WIDTHS = (ATTN_WIDTH, HEAD_DIM, HEAD_DIM, IDX_HEADS * IDX_DIM, IDX_DIM, IDX_HEADS,
           POOL_WIDTH, 2 * D_MODEL)
W_Q0, W_K0, W_V0, W_QI0, W_KI0, W_WI0, W_U0, W_G0, W_END = (
    sum(_WIDTHS[:n]) for n in range(len(_WIDTHS) + 1))
T_Q0 = 0
T_V0 = T_Q0 + ATTN_WIDTH
T_QI0 = T_V0 + HEAD_DIM
T_WI0 = T_QI0 + IDX_HEADS * IDX_DIM
T_ROWS = -(-(T_WI0 + IDX_HEADS) // PACK_ROWS) * PACK_ROWS
N_KK0 = 0
N_U0 = N_KK0 + LANES
N_G0 = N_U0 + POOL_WIDTH
N_COLS = N_G0 + 2 * D_MODEL


def _rms(x, eps=RMS_EPS):
    return x * lax.rsqrt(jnp.mean(x * x, axis=-1, keepdims=True) + eps)


def _regroup_w_in(w_ref, wt_ref, wn_ref):
    bf = jnp.bfloat16
    wt_ref[T_Q0:T_V0, :] = w_ref[W_Q0:W_K0, :].astype(bf)
    wt_ref[T_V0:T_WI0, :] = w_ref[W_V0:W_KI0, :].astype(bf)
    tail = jnp.concatenate(
        [w_ref[W_WI0:W_U0, :], jnp.zeros((T_ROWS - T_WI0 - IDX_HEADS, D_MODEL), jnp.float32)],
        axis=0)
    wt_ref[T_WI0:T_ROWS, :] = tail.astype(bf)
    k_ki = jnp.concatenate([w_ref[W_K0:W_V0, :], w_ref[W_KI0:W_WI0, :]], axis=0)
    wn_ref[:, N_KK0:N_KK0 + LANES] = k_ki.T.astype(bf)
    wn_ref[:, N_U0:N_COLS] = w_ref[W_U0:W_END, :].T.astype(bf)


def _proj_kernel(x_ref, gmix_ref, w_ref, gq_ref, gk_ref,
                 qt_ref, qit_ref, wit_ref, vt_ref, kaug_ref, kiaug_ref, u_ref, g_ref,
                 wt_ref, wn_ref, *, tm, seq):
    i = pl.program_id(0)

    @pl.when(i == 0)
    def _():
        _regroup_w_in(w_ref, wt_ref, wn_ref)

    x = x_ref[...]
    h = (_rms(x) * gmix_ref[...]).astype(jnp.bfloat16)
    y_t = lax.dot_general(wt_ref[...], h, (((1,), (1,)), ((), ())),
                          preferred_element_type=jnp.float32)
    y_n = jnp.dot(h, wn_ref[...], preferred_element_type=jnp.float32)

    gq = gq_ref[...]
    for hd in range(N_HEADS):
        blk = y_t[T_Q0 + hd * HEAD_DIM:T_Q0 + (hd + 1) * HEAD_DIM, :]
        ss = jnp.mean(blk * blk, axis=0, keepdims=True)
        qn = blk * lax.rsqrt(ss + RMS_EPS) * gq * (HEAD_DIM ** -0.5 * LOG2E)
        qt_ref[0, hd * HEAD_DIM:(hd + 1) * HEAD_DIM, :] = qn.astype(qt_ref.dtype)

    qit_ref[0] = y_t[T_QI0:T_QI0 + IDX_HEADS * IDX_DIM, :].astype(qit_ref.dtype)
    wit_ref[0] = y_t[T_WI0:T_WI0 + IDX_HEADS, :] * IDX_SCALE

    v_t = y_t[T_V0:T_V0 + HEAD_DIM, :]
    row = lax.broadcasted_iota(jnp.int32, (VT_ROWS - HEAD_DIM, tm), 0)
    tail = jnp.where(row == 0, 1.0, 0.0)
    vt_full = jnp.concatenate([v_t, tail], axis=0).astype(vt_ref.dtype)
    for cc in range(tm // KEY_CHUNK):
        vt_ref[0, cc] = vt_full[:, cc * KEY_CHUNK:(cc + 1) * KEY_CHUNK]

    kk = y_n[:, N_KK0:N_KK0 + LANES]
    lane = lax.broadcasted_iota(jnp.int32, (tm, LANES), 1)
    is_k = lane < HEAD_DIM
    ss = jnp.sum(jnp.where(is_k, kk * kk, 0.0), axis=-1, keepdims=True) * (1.0 / HEAD_DIM)
    kn = kk * lax.rsqrt(ss + RMS_EPS) * gk_ref[...]
    pos = (i * tm) % seq + lax.broadcasted_iota(jnp.int32, (tm, LANES), 0)
    pos_lo = pos & (BF16_EXACT_INT - 1)
    pos_hi = (pos - pos_lo).astype(jnp.float32)
    pos_lo = pos_lo.astype(jnp.float32)
    feats = jnp.where(lane < HEAD_DIM + SLOPE_PARTS, pos_hi,
                      jnp.where(lane < HEAD_DIM + 2 * SLOPE_PARTS, pos_lo, 0.0))
    kaug_ref[0] = jnp.where(is_k, kn, feats).astype(kaug_ref.dtype)
    kiaug_ref[0] = jnp.where(is_k, 0.0, kk).astype(kiaug_ref.dtype)

    u_ref[...] = y_n[:, N_U0:N_U0 + POOL_WIDTH]
    g_ref[...] = y_n[:, N_G0:N_G0 + 2 * D_MODEL].astype(g_ref.dtype)


def _proj(x2, gmix, w_in_layers, gq, gk, *, batch, seq, tm):
    w_in_t, layer = w_in_layers
    n = batch * seq
    tiles_per_seq = seq // tm
    const = lambda i: (0, 0)
    bidx = lambda i: (i // tiles_per_seq, 0, i % tiles_per_seq)
    out_shape = (
        jax.ShapeDtypeStruct((batch, ATTN_WIDTH, seq), jnp.bfloat16),
        jax.ShapeDtypeStruct((batch, IDX_HEADS * IDX_DIM, seq), jnp.bfloat16),
        jax.ShapeDtypeStruct((batch, IDX_HEADS, seq), jnp.float32),
        jax.ShapeDtypeStruct((batch, seq // KEY_CHUNK, VT_ROWS, KEY_CHUNK), jnp.bfloat16),
        jax.ShapeDtypeStruct((batch, seq, AUG), jnp.bfloat16),
        jax.ShapeDtypeStruct((batch, seq, AUG), jnp.bfloat16),
        jax.ShapeDtypeStruct((n, POOL_WIDTH), jnp.float32),
        jax.ShapeDtypeStruct((n, 2 * D_MODEL), jnp.bfloat16),
    )
    out_specs = (
        pl.BlockSpec((1, ATTN_WIDTH, tm), bidx),
        pl.BlockSpec((1, IDX_HEADS * IDX_DIM, tm), bidx),
        pl.BlockSpec((1, IDX_HEADS, tm), bidx),
        pl.BlockSpec((1, tm // KEY_CHUNK, VT_ROWS, KEY_CHUNK),
                     lambda i: (i // tiles_per_seq, i % tiles_per_seq, 0, 0)),
        pl.BlockSpec((1, tm, AUG), lambda i: (i // tiles_per_seq, i % tiles_per_seq, 0)),
        pl.BlockSpec((1, tm, AUG), lambda i: (i // tiles_per_seq, i % tiles_per_seq, 0)),
        pl.BlockSpec((tm, POOL_WIDTH), lambda i: (i, 0)),
        pl.BlockSpec((tm, 2 * D_MODEL), lambda i: (i, 0)),
    )
    in_specs = [
        pl.BlockSpec((tm, D_MODEL), lambda i: (i, 0)),
        pl.BlockSpec((1, D_MODEL), const),
        pl.BlockSpec((None,) + w_in_t.shape[1:], lambda i: (layer, 0, 0),
                     pipeline_mode=pl.Buffered(1)),
        pl.BlockSpec((HEAD_DIM, 1), const),
        pl.BlockSpec((1, LANES), const),
    ]
    return pl.pallas_call(
        functools.partial(_proj_kernel, tm=tm, seq=seq),
        grid=(n // tm,),
        in_specs=in_specs,
        out_specs=out_specs,
        out_shape=out_shape,
        scratch_shapes=[
            pltpu.VMEM((T_ROWS, D_MODEL), jnp.bfloat16),
            pltpu.VMEM((D_MODEL, N_COLS), jnp.bfloat16),
        ],
        compiler_params=pltpu.CompilerParams(
            dimension_semantics=("arbitrary",), vmem_limit_bytes=VMEM_LIMIT),
        name="proj",
    )(x2, gmix, w_in_t, gq, gk)


def _attn_kernel(qt_ref, qit_ref, wit_ref, vt_ref, kaug_ref, kiaug_ref, qaug_ref,
                 at_ref, rhs_idx, rhs_att, sc_ref, sc16_ref, acc_t, buf_a, buf_b):
    j = pl.program_id(1)
    nq = Q_BLOCK
    kc = KEY_CHUNK
    nch = j + 1
    lanes = GROUP * nq
    tiles = GROUP * N_HEADS
    inf = jnp.float32(jnp.inf)

    def cols(t):
        return slice(t * nq, (t + 1) * nq)

    zeros64 = jnp.zeros((HEAD_DIM, nq), rhs_idx.dtype)
    for t in range(tiles):
        g, hd = divmod(t, N_HEADS)
        rows = slice(hd * HEAD_DIM, (hd + 1) * HEAD_DIM)
        rhs_idx[0:HEAD_DIM, cols(t)] = zeros64
        rhs_idx[HEAD_DIM:AUG, cols(t)] = qit_ref[g, rows, :]
        rhs_att[0:HEAD_DIM, cols(t)] = qt_ref[g, rows, :]
        rhs_att[HEAD_DIM:AUG, cols(t)] = qaug_ref[:, cols(hd)]

    t_row = j * nq + lax.broadcasted_iota(jnp.int32, (1, lanes), 1) % nq
    t_pos = j * nq + lax.broadcasted_iota(jnp.int32, (kc, nq), 1)
    s_loc = lax.broadcasted_iota(jnp.int32, (kc, nq), 0)

    def pairwise(step, n, carry):
        carry = lax.fori_loop(0, n // 2, lambda i, cr: step(2 * i + 1, step(2 * i, cr)), carry)
        return lax.cond(n % 2 == 1, lambda cr: step(n - 1, cr), lambda cr: cr, carry)

    def fold(x, op, rows=8):
        parts = [x[r * rows:(r + 1) * rows, :] for r in range(kc // rows)]
        while len(parts) > 1:
            parts = [op(parts[i], parts[i + 1]) for i in range(0, len(parts), 2)]
        return parts[0]

    bufs = (buf_a, buf_b)

    def pipelined(produce, consume, finish, carry, prepare=lambda c: None):
        def drain(c, par, carry, aux):
            part = None
            for t in range(tiles):
                part = consume(c, bufs[par], t, carry, aux, part)
            return finish(c, carry, part)

        def both(c, par, carry, aux):
            part, nxt, ctx = None, [], prepare(c + 1)
            for t in range(tiles):
                nxt.append(produce(c + 1, bufs[1 - par], t, ctx))
                part = consume(c, bufs[par], t, carry, aux, part)
            return finish(c, carry, part), tuple(nxt)

        def pair_body(i, st):
            return both(2 * i + 1, 1, *both(2 * i, 0, *st))

        ctx0 = prepare(0)
        first = tuple(produce(0, bufs[0], t, ctx0) for t in range(tiles))
        st = lax.fori_loop(0, (nch - 1) // 2, pair_body, (carry, first))

        def even_tail(st):
            return drain(nch - 1, 1, *both(nch - 2, 0, *st))

        return lax.cond(nch % 2 == 0, even_tail, lambda st: drain(nch - 1, 0, *st), st)

    def score_step(c, lo_hi):
        off = pl.multiple_of(c * kc, kc)
        causal = off + s_loc <= t_pos
        los, his = [], []
        for g in range(GROUP):
            tot = None
            for hd in range(IDX_HEADS):
                t = g * N_HEADS + hd
                x = jnp.dot(kiaug_ref[g, pl.ds(off, kc), :], rhs_idx[:, cols(t)],
                            preferred_element_type=jnp.float32)
                term = jnp.maximum(x, 0.0) * wit_ref[g, hd:hd + 1, :]
                tot = term if tot is None else tot + term
            masked = jnp.where(causal, tot, -inf)
            sc_ref[pl.ds(off, kc), cols(g)] = masked
            sc16_ref[pl.ds(off, kc), cols(g)] = masked.astype(jnp.bfloat16)
            los.append(fold(jnp.where(causal, tot, inf), jnp.minimum))
            his.append(fold(masked, jnp.maximum))
        lo, hi = lo_hi
        return (jnp.minimum(lo, jnp.concatenate(los, axis=1)),
                jnp.maximum(hi, jnp.concatenate(his, axis=1)))

    lo, hi = pairwise(score_step, nch,
                      (jnp.full((8, lanes), inf, jnp.float32),
                       jnp.full((8, lanes), -inf, jnp.float32)))
    lo = jnp.min(lo, axis=0, keepdims=True)
    hi = jnp.max(hi, axis=0, keepdims=True)

    def by_lane_tile(ref, c, fn):
        off = pl.multiple_of(c * kc, kc)
        return jnp.concatenate(
            [fn(ref[pl.ds(off, kc), l * LANES:(l + 1) * LANES], slice(l * LANES, (l + 1) * LANES))
             for l in range(lanes // LANES)], axis=1)

    def count_ge(v):
        def step(c, acc):
            return acc + by_lane_tile(
                sc_ref, c, lambda s, ln: fold(jnp.where(s >= v[:, ln], 1.0, 0.0), jnp.add))
        acc = pairwise(step, nch, jnp.zeros((8, lanes), jnp.float32))
        return jnp.sum(acc, axis=0, keepdims=True)

    def count_ge16(v):
        v16 = v.astype(jnp.bfloat16)
        one, zero = jnp.ones((), jnp.bfloat16), jnp.zeros((), jnp.bfloat16)
        def step(c, acc):
            return acc + by_lane_tile(
                sc16_ref, c,
                lambda s, ln: fold(jnp.where(s >= v16[:, ln], one, zero), jnp.add, PACK_ROWS))
        acc = pairwise(step, nch, jnp.zeros((PACK_ROWS, lanes), jnp.bfloat16))
        return jnp.sum(acc.astype(jnp.float32), axis=0, keepdims=True)

    def max_below(v):
        def step(c, acc):
            return jnp.maximum(acc, by_lane_tile(
                sc_ref, c, lambda s, ln: fold(jnp.where(s < v[:, ln], s, -inf), jnp.maximum)))
        acc = pairwise(step, nch, jnp.full((8, lanes), -inf, jnp.float32))
        return jnp.max(acc, axis=0, keepdims=True)

    def n_open(done):
        return jnp.sum(1 - done)

    def snap(state):
        lo, hi, thr, cge, done = state
        b = max_below(hi)
        c = count_ge(b)
        ok = c >= TOPK
        newly = ok & (done == 0)
        return (lo, jnp.where(ok, hi, b), jnp.where(newly, b, thr),
                jnp.where(newly, c, cge), jnp.where(ok, 1, done))

    def bisect(state):
        lo, hi, thr, cge, done = state
        mid = 0.5 * lo + 0.5 * hi
        ge = count_ge(mid) >= TOPK
        return (jnp.where(ge, mid, lo), jnp.where(ge, hi, mid), thr, cge, done)

    def bisect16(_, lo_hi):
        lo, hi = lo_hi
        mid = (0.5 * lo + 0.5 * hi).astype(jnp.bfloat16).astype(jnp.float32)
        ge = count_ge16(mid) >= TOPK
        return jnp.where(ge, mid, lo), jnp.where(ge, hi, mid)

    short = jnp.where((t_row + 1) <= TOPK, 1, 0)
    n0 = n_open(short)
    run = n0 > 0
    lo16 = lo.astype(jnp.bfloat16).astype(jnp.float32)
    hi16 = hi.astype(jnp.bfloat16).astype(jnp.float32)
    hi16 = (hi16 + jnp.abs(hi16) * (2 * BF16_STEP) + TINY).astype(jnp.bfloat16).astype(jnp.float32)
    lo16, hi16 = lax.fori_loop(0, jnp.where(run, BISECT16_STEPS, 0), bisect16, (lo16, hi16))
    lo32 = jnp.maximum(lo, lo16 - jnp.abs(lo16) * BF16_STEP - TINY)
    state = (lo32, hi16, lo, jnp.zeros((1, lanes), jnp.float32), short)
    state = lax.fori_loop(0, jnp.where(run, BISECT_STEPS, 0), lambda _, s: bisect(s), state)
    state = lax.cond(run, snap, lambda s: s, state)

    def finish_body(carry):
        s = snap(bisect(carry[0]))
        return s, n_open(s[4]), carry[2] + 1

    state, _, _ = lax.while_loop(lambda carry: (carry[1] > 0) & (carry[2] < MAX_FINISH_TRIPS),
                                 finish_body, (state, n_open(state[4]), jnp.int32(0)))
    thr, cge = state[2], state[3]

    @pl.when(jnp.max(cge) > TOPK)
    def _():
        def gt_body(c, acc):
            off = pl.multiple_of(c * kc, kc)
            return acc + fold(jnp.where(sc_ref[pl.ds(off, kc), :] > thr, 1.0, 0.0), jnp.add)
        cnt_gt = jnp.sum(lax.fori_loop(0, nch, gt_body, jnp.zeros((8, lanes), jnp.float32)),
                         axis=0, keepdims=True)
        need = TOPK - cnt_gt
        r_i = lax.broadcasted_iota(jnp.int32, (kc, kc), 0)
        c_i = lax.broadcasted_iota(jnp.int32, (kc, kc), 1)
        tri = jnp.where(c_i <= r_i, 1.0, 0.0).astype(jnp.bfloat16)

        def tie_body(c, seen):
            off = pl.multiple_of(c * kc, kc)
            s = sc_ref[pl.ds(off, kc), :]
            eq = s == thr
            eq_f = jnp.where(eq, 1.0, 0.0)
            incl = jnp.dot(tri, eq_f.astype(jnp.bfloat16), preferred_element_type=jnp.float32)
            rank = incl - eq_f + seen
            sc_ref[pl.ds(off, kc), :] = jnp.where(eq & (rank >= need), -inf, s)
            return seen + incl[kc - 1:kc, :]

        lax.fori_loop(0, nch, tie_body, jnp.zeros((1, lanes), jnp.float32))

    acc_t[...] = jnp.zeros_like(acc_t)

    def select_bias(c):
        off = pl.multiple_of(c * kc, kc)
        return jnp.where(sc_ref[pl.ds(off, kc), :] >= thr, 0.0, NEG_BIG)

    def masked_logits(c, buf, t, bias):
        g = t // N_HEADS
        off = pl.multiple_of(c * kc, kc)
        lm = jnp.dot(kaug_ref[g, pl.ds(off, kc), :], rhs_att[:, cols(t)],
                     preferred_element_type=jnp.float32) + bias[:, cols(g)]
        buf[:, cols(t)] = lm
        return jnp.max(fold(lm, jnp.maximum), axis=0, keepdims=True)

    def softmax_pv(c, buf, t, m_old, cmax, m_list):
        m_new = jnp.maximum(m_old[t], cmax[t])
        alpha = jnp.exp2(m_old[t] - m_new)
        p = jnp.exp2((buf[:, cols(t)] - m_new).astype(jnp.bfloat16))
        pv = jnp.dot(vt_ref[t // N_HEADS, c], p, preferred_element_type=jnp.float32)
        acc_t[:, cols(t)] = acc_t[:, cols(t)] * alpha + pv
        return (m_list or ()) + (m_new,)

    pipelined(masked_logits, softmax_pv, lambda c, m_old, m_list: m_list,
              tuple(jnp.full((1, nq), NEG_BIG, jnp.float32) for _ in range(tiles)),
              prepare=select_bias)

    for t in range(tiles):
        g, hd = divmod(t, N_HEADS)
        inv_l = 1.0 / acc_t[HEAD_DIM:HEAD_DIM + 1, cols(t)]
        at_ref[g, hd * HEAD_DIM:(hd + 1) * HEAD_DIM, :] = (
            acc_t[0:HEAD_DIM, cols(t)] * inv_l).astype(at_ref.dtype)


def _attn(qt, qit, wit, vt, kaug, kiaug, qaug, *, batch, seq):
    nq = Q_BLOCK
    assert KEY_CHUNK == Q_BLOCK and seq % nq == 0 and batch % GROUP == 0
    wide = GROUP * N_HEADS * nq
    qblk = lambda b, j: (b, 0, j)
    whole = lambda b, j: (b, 0, 0)
    return pl.pallas_call(
        _attn_kernel,
        grid=(batch // GROUP, seq // nq),
        in_specs=[
            pl.BlockSpec((GROUP, ATTN_WIDTH, nq), qblk),
            pl.BlockSpec((GROUP, IDX_HEADS * IDX_DIM, nq), qblk),
            pl.BlockSpec((GROUP, IDX_HEADS, nq), qblk),
            pl.BlockSpec((GROUP, seq // KEY_CHUNK, VT_ROWS, KEY_CHUNK),
                         lambda b, j: (b, 0, 0, 0)),
            pl.BlockSpec((GROUP, seq, AUG), whole),
            pl.BlockSpec((GROUP, seq, AUG), whole),
            pl.BlockSpec((AUG - HEAD_DIM, N_HEADS * nq), lambda b, j: (0, 0)),
        ],
        out_specs=pl.BlockSpec((GROUP, ATTN_WIDTH, nq), qblk),
        out_shape=jax.ShapeDtypeStruct((batch, ATTN_WIDTH, seq), jnp.bfloat16),
        scratch_shapes=[
            pltpu.VMEM((AUG, wide), jnp.bfloat16),
            pltpu.VMEM((AUG, wide), jnp.bfloat16),
            pltpu.VMEM((seq, GROUP * nq), jnp.float32),
            pltpu.VMEM((seq, GROUP * nq), jnp.bfloat16),
            pltpu.VMEM((VT_ROWS, wide), jnp.float32),
            pltpu.VMEM((KEY_CHUNK, wide), jnp.float32),
            pltpu.VMEM((KEY_CHUNK, wide), jnp.float32),
        ],
        compiler_params=pltpu.CompilerParams(
            dimension_semantics=("arbitrary", "arbitrary"), vmem_limit_bytes=VMEM_LIMIT),
        name="attn",
    )(qt, qit, wit, vt, kaug, kiaug, qaug)


def _mix_kernel(x_ref, at_ref, u_ref, halo_ref, g_ref, wpg_ref, pscale_ref, wba_ref, wbp_ref,
                wout_ref, gffn_ref, wg_ref, wu_ref, wd_ref, o_ref, ubuf, *, tm, seq):
    i = pl.program_id(0)
    tiles_per_seq = seq // tm
    first = (i % tiles_per_seq) == 0
    halo = halo_ref[...]
    ubuf[0:HALO, :] = jnp.where(first, jnp.zeros_like(halo), halo)
    ubuf[HALO:HALO + tm, :] = u_ref[...]

    t_pos = (i * tm) % seq + lax.broadcasted_iota(jnp.int32, (tm, POOL_GROUP_WIDTH), 0)
    pooled = []
    for g, w in enumerate(POOL_WINDOWS):
        lanes = slice(g * POOL_GROUP_WIDTH, (g + 1) * POOL_GROUP_WIDTH)
        levels = w.bit_length() - 1
        cur = ubuf[:, lanes]
        step = 1
        for _ in range(levels):
            cur = cur + pltpu.roll(cur, step, axis=0)
            step *= 2
        wsum = cur[HALO:, :]
        cnt = jnp.minimum(t_pos + 1, w).astype(jnp.float32)
        d = (wsum / cnt - ubuf[HALO:HALO + tm, lanes]).astype(jnp.bfloat16)
        pooled.append(jnp.dot(d, wpg_ref[g], preferred_element_type=jnp.float32))
    p = (jnp.concatenate(pooled, axis=-1) * pscale_ref[...]).astype(jnp.bfloat16)

    a_proj = lax.dot_general(at_ref[0], wba_ref[...], (((0,), (0,)), ((), ())),
                             preferred_element_type=jnp.float32)
    p_proj = jnp.dot(p, wbp_ref[...], preferred_element_type=jnp.float32)
    gates = g_ref[...].astype(jnp.float32)
    merged = (jax.nn.sigmoid(gates[:, :D_MODEL]) * a_proj
              + jax.nn.sigmoid(gates[:, D_MODEL:]) * p_proj)
    x1 = x_ref[...] + jnp.dot(merged.astype(jnp.bfloat16), wout_ref[...],
                              preferred_element_type=jnp.float32)

    h2 = (_rms(x1) * gffn_ref[...]).astype(jnp.bfloat16)
    gate = jnp.dot(h2, wg_ref[...], preferred_element_type=jnp.float32)
    up = jnp.dot(h2, wu_ref[...], preferred_element_type=jnp.float32)
    act = (jax.nn.silu(gate) * up).astype(jnp.bfloat16)
    o_ref[...] = x1 + jnp.dot(act, wd_ref[...], preferred_element_type=jnp.float32)


def _mix(x2, at, u, gates, wpg, pscale, wba, wbp, wout, gffn, wg, wu, wd, *, batch, seq, tm):
    n = batch * seq
    tiles_per_seq = seq // tm
    row = lambda i: (i, 0)
    const2 = lambda i: (0, 0)
    resident = functools.partial(pl.BlockSpec, pipeline_mode=pl.Buffered(1))
    return pl.pallas_call(
        functools.partial(_mix_kernel, tm=tm, seq=seq),
        grid=(n // tm,),
        in_specs=[
            pl.BlockSpec((tm, D_MODEL), row),
            pl.BlockSpec((1, ATTN_WIDTH, tm), lambda i: (i // tiles_per_seq, 0, i % tiles_per_seq)),
            pl.BlockSpec((tm, POOL_WIDTH), row),
            pl.BlockSpec((HALO, POOL_WIDTH), lambda i: (jnp.maximum(i * (tm // HALO) - 1, 0), 0)),
            pl.BlockSpec((tm, 2 * D_MODEL), row),
            resident((len(POOL_WINDOWS), POOL_GROUP_WIDTH, POOL_GROUP_WIDTH), lambda i: (0, 0, 0)),
            resident((1, POOL_WIDTH), const2),
            resident((ATTN_WIDTH, D_MODEL), const2),
            resident((POOL_WIDTH, D_MODEL), const2),
            resident((D_MODEL, D_MODEL), const2),
            resident((1, D_MODEL), const2),
            resident((D_MODEL, D_FF), const2),
            resident((D_MODEL, D_FF), const2),
            resident((D_FF, D_MODEL), const2),
        ],
        out_specs=pl.BlockSpec((tm, D_MODEL), row),
        out_shape=jax.ShapeDtypeStruct((n, D_MODEL), jnp.float32),
        scratch_shapes=[pltpu.VMEM((tm + HALO, POOL_WIDTH), jnp.float32)],
        compiler_params=pltpu.CompilerParams(
            dimension_semantics=("arbitrary",), vmem_limit_bytes=VMEM_LIMIT),
        name="mix",
    )(x2, at, u, u, gates, wpg, pscale, wba, wbp, wout, gffn, wg, wu, wd)


def _alibi_rows():
    slopes = 2.0 ** (-8.0 * jnp.arange(1, N_HEADS + 1, dtype=jnp.float32) / N_HEADS) * LOG2E
    parts, rest = [], slopes
    for _ in range(SLOPE_PARTS):
        term = rest.astype(jnp.bfloat16)
        parts.append(term)
        rest = rest - term.astype(jnp.float32)
    block = jnp.stack(parts * 2, axis=0)
    block = jnp.repeat(block, Q_BLOCK, axis=1)
    pad = jnp.zeros((AUG - HEAD_DIM - 2 * SLOPE_PARTS, N_HEADS * Q_BLOCK), jnp.bfloat16)
    return jnp.concatenate([block, pad], axis=0)


def _layer(x2, norm_mix, w_in, q_norm, k_norm, w_pool_group, pool_scale, w_branch_attn,
           w_branch_pool, w_out, norm_ffn, w_ffn_gate, w_ffn_up, w_ffn_down, *, batch, seq):
    bf = jnp.bfloat16
    gk = jnp.concatenate([k_norm, jnp.zeros((LANES - HEAD_DIM,), k_norm.dtype)])[None, :]

    qt, qit, wit, vt, kaug, kiaug, u, gates = _proj(
        x2, norm_mix[None, :], w_in, q_norm[:, None], gk, batch=batch, seq=seq, tm=PROJ_ROWS)
    at = _attn(qt, qit, wit, vt, kaug, kiaug, _alibi_rows(), batch=batch, seq=seq)
    return _mix(x2, at, u, gates, w_pool_group.astype(bf), pool_scale[None, :],
                w_branch_attn.astype(bf), w_branch_pool.astype(bf), w_out.astype(bf),
                norm_ffn[None, :], w_ffn_gate.astype(bf), w_ffn_up.astype(bf),
                w_ffn_down.astype(bf), batch=batch, seq=seq, tm=MIX_ROWS)


def kernel(x, norm_mix, w_in, q_norm, k_norm, w_pool_group, pool_scale, w_branch_attn,
           w_branch_pool, w_out, norm_ffn, w_ffn_gate, w_ffn_up, w_ffn_down):
    batch, seq, d = x.shape
    x2 = x.reshape(batch * seq, d)
    w_in_t = jnp.swapaxes(w_in, 1, 2)
    for l in range(norm_mix.shape[0]):
        x2 = _layer(x2, norm_mix[l], (w_in_t, l), q_norm[l], k_norm[l], w_pool_group[l],
                    pool_scale[l], w_branch_attn[l], w_branch_pool[l], w_out[l], norm_ffn[l],
                    w_ffn_gate[l], w_ffn_up[l], w_ffn_down[l], batch=batch, seq=seq)
    return x2.reshape(batch, seq, d)
```

```python
import functools

import jax
import jax.numpy as jnp
from jax import lax
from jax.experimental import pallas as pl
from jax.experimental.pallas import tpu as pltpu

D_MODEL = 1024
N_HEADS = 8
HEAD_DIM = 64
ATTN_WIDTH = N_HEADS * HEAD_DIM
IDX_HEADS = 8
IDX_DIM = 64
IDX_SCALE = (IDX_HEADS ** -0.5) * (IDX_DIM ** -0.5)
TOPK = 256
POOL_WINDOWS = (2, 4, 8, 16)
POOL_WIDTH = 512
POOL_GROUP_WIDTH = 128
D_FF = 2816
RMS_EPS = 1e-6
NEG_BIG = -1e30

LANES = 128
PROJ_ROWS = 1024
MIX_ROWS = 512
Q_BLOCK = 256
GROUP = 4
KEY_CHUNK = 256
AUG = 128
VT_ROWS = 80
HALO = 16
LOG2E = 1.4426950408889634
SLOPE_PARTS = 3
PACK_ROWS = 16
BF16_STEP = 2.0 ** -7
BF16_EXACT_INT = 256
BISECT16_STEPS = 10
BISECT_STEPS = 6
TINY = 1e-30
MAX_FINISH_TRIPS = 512
VMEM_LIMIT = 56 * 1024 * 1024

_WIDTHS = (ATTN_WIDTH, HEAD_DIM, HEAD_DIM, IDX_HEADS * IDX_DIM, IDX_DIM, IDX_HEADS,
           POOL_WIDTH, 2 * D_MODEL)
W_Q0, W_K0, W_V0, W_QI0, W_KI0, W_WI0, W_U0, W_G0, W_END = (
    sum(_WIDTHS[:n]) for n in range(len(_WIDTHS) + 1))
T_Q0 = 0
T_V0 = T_Q0 + ATTN_WIDTH
T_QI0 = T_V0 + HEAD_DIM
T_WI0 = T_QI0 + IDX_HEADS * IDX_DIM
T_ROWS = -(-(T_WI0 + IDX_HEADS) // PACK_ROWS) * PACK_ROWS
N_KK0 = 0
N_U0 = N_KK0 + LANES
N_G0 = N_U0 + POOL_WIDTH
N_COLS = N_G0 + 2 * D_MODEL


def _rms(x, eps=RMS_EPS):
    return x * lax.rsqrt(jnp.mean(x * x, axis=-1, keepdims=True) + eps)


def _regroup_w_in(w_ref, wt_ref, wn_ref):
    bf = jnp.bfloat16
    wt_ref[T_Q0:T_V0, :] = w_ref[W_Q0:W_K0, :].astype(bf)
    wt_ref[T_V0:T_WI0, :] = w_ref[W_V0:W_KI0, :].astype(bf)
    tail = jnp.concatenate(
        [w_ref[W_WI0:W_U0, :], jnp.zeros((T_ROWS - T_WI0 - IDX_HEADS, D_MODEL), jnp.float32)],
        axis=0)
    wt_ref[T_WI0:T_ROWS, :] = tail.astype(bf)
    k_ki = jnp.concatenate([w_ref[W_K0:W_V0, :], w_ref[W_KI0:W_WI0, :]], axis=0)
    wn_ref[:, N_KK0:N_KK0 + LANES] = k_ki.T.astype(bf)
    wn_ref[:, N_U0:N_COLS] = w_ref[W_U0:W_END, :].T.astype(bf)


def _proj_kernel(x_ref, gmix_ref, w_ref, gq_ref, gk_ref,
                 qt_ref, qit_ref, wit_ref, vt_ref, kaug_ref, kiaug_ref, u_ref, g_ref,
                 wt_ref, wn_ref, *, tm, seq):
    i = pl.program_id(0)

    @pl.when(i == 0)
    def _():
        _regroup_w_in(w_ref, wt_ref, wn_ref)

    x = x_ref[...]
    h = (_rms(x) * gmix_ref[...]).astype(jnp.bfloat16)
    y_t = lax.dot_general(wt_ref[...], h, (((1,), (1,)), ((), ())),
                          preferred_element_type=jnp.float32)
    y_n = jnp.dot(h, wn_ref[...], preferred_element_type=jnp.float32)

    gq = gq_ref[...]
    for hd in range(N_HEADS):
        blk = y_t[T_Q0 + hd * HEAD_DIM:T_Q0 + (hd + 1) * HEAD_DIM, :]
        ss = jnp.mean(blk * blk, axis=0, keepdims=True)
        qn = blk * lax.rsqrt(ss + RMS_EPS) * gq * (HEAD_DIM ** -0.5 * LOG2E)
        qt_ref[0, hd * HEAD_DIM:(hd + 1) * HEAD_DIM, :] = qn.astype(qt_ref.dtype)

    qit_ref[0] = y_t[T_QI0:T_QI0 + IDX_HEADS * IDX_DIM, :].astype(qit_ref.dtype)
    wit_ref[0] = y_t[T_WI0:T_WI0 + IDX_HEADS, :] * IDX_SCALE

    v_t = y_t[T_V0:T_V0 + HEAD_DIM, :]
    row = lax.broadcasted_iota(jnp.int32, (VT_ROWS - HEAD_DIM, tm), 0)
    tail = jnp.where(row == 0, 1.0, 0.0)
    vt_full = jnp.concatenate([v_t, tail], axis=0).astype(vt_ref.dtype)
    for cc in range(tm // KEY_CHUNK):
        vt_ref[0, cc] = vt_full[:, cc * KEY_CHUNK:(cc + 1) * KEY_CHUNK]

    kk = y_n[:, N_KK0:N_KK0 + LANES]
    lane = lax.broadcasted_iota(jnp.int32, (tm, LANES), 1)
    is_k = lane < HEAD_DIM
    ss = jnp.sum(jnp.where(is_k, kk * kk, 0.0), axis=-1, keepdims=True) * (1.0 / HEAD_DIM)
    kn = kk * lax.rsqrt(ss + RMS_EPS) * gk_ref[...]
    pos = (i * tm) % seq + lax.broadcasted_iota(jnp.int32, (tm, LANES), 0)
    pos_lo = pos & (BF16_EXACT_INT - 1)
    pos_hi = (pos - pos_lo).astype(jnp.float32)
    pos_lo = pos_lo.astype(jnp.float32)
    feats = jnp.where(lane < HEAD_DIM + SLOPE_PARTS, pos_hi,
                      jnp.where(lane < HEAD_DIM + 2 * SLOPE_PARTS, pos_lo, 0.0))
    kaug_ref[0] = jnp.where(is_k, kn, feats).astype(kaug_ref.dtype)
    kiaug_ref[0] = jnp.where(is_k, 0.0, kk).astype(kiaug_ref.dtype)

    u_ref[...] = y_n[:, N_U0:N_U0 + POOL_WIDTH]
    g_ref[...] = y_n[:, N_G0:N_G0 + 2 * D_MODEL].astype(g_ref.dtype)


def _proj(x2, gmix, w_in_layers, gq, gk, *, batch, seq, tm):
    w_in_t, layer = w_in_layers
    assert seq % tm == 0 and tm % KEY_CHUNK == 0 and w_in_t.shape[1:] == (W_END, D_MODEL)
    n = batch * seq
    tiles_per_seq = seq // tm
    const = lambda i: (0, 0)
    bidx = lambda i: (i // tiles_per_seq, 0, i % tiles_per_seq)
    out_shape = (
        jax.ShapeDtypeStruct((batch, ATTN_WIDTH, seq), jnp.bfloat16),
        jax.ShapeDtypeStruct((batch, IDX_HEADS * IDX_DIM, seq), jnp.bfloat16),
        jax.ShapeDtypeStruct((batch, IDX_HEADS, seq), jnp.float32),
        jax.ShapeDtypeStruct((batch, seq // KEY_CHUNK, VT_ROWS, KEY_CHUNK), jnp.bfloat16),
        jax.ShapeDtypeStruct((batch, seq, AUG), jnp.bfloat16),
        jax.ShapeDtypeStruct((batch, seq, AUG), jnp.bfloat16),
        jax.ShapeDtypeStruct((n, POOL_WIDTH), jnp.float32),
        jax.ShapeDtypeStruct((n, 2 * D_MODEL), jnp.bfloat16),
    )
    out_specs = (
        pl.BlockSpec((1, ATTN_WIDTH, tm), bidx),
        pl.BlockSpec((1, IDX_HEADS * IDX_DIM, tm), bidx),
        pl.BlockSpec((1, IDX_HEADS, tm), bidx),
        pl.BlockSpec((1, tm // KEY_CHUNK, VT_ROWS, KEY_CHUNK),
                     lambda i: (i // tiles_per_seq, i % tiles_per_seq, 0, 0)),
        pl.BlockSpec((1, tm, AUG), lambda i: (i // tiles_per_seq, i % tiles_per_seq, 0)),
        pl.BlockSpec((1, tm, AUG), lambda i: (i // tiles_per_seq, i % tiles_per_seq, 0)),
        pl.BlockSpec((tm, POOL_WIDTH), lambda i: (i, 0)),
        pl.BlockSpec((tm, 2 * D_MODEL), lambda i: (i, 0)),
    )
    in_specs = [
        pl.BlockSpec((tm, D_MODEL), lambda i: (i, 0)),
        pl.BlockSpec((1, D_MODEL), const),
        pl.BlockSpec((None,) + w_in_t.shape[1:], lambda i: (layer, 0, 0),
                     pipeline_mode=pl.Buffered(1)),
        pl.BlockSpec((HEAD_DIM, 1), const),
        pl.BlockSpec((1, LANES), const),
    ]
    return pl.pallas_call(
        functools.partial(_proj_kernel, tm=tm, seq=seq),
        grid=(n // tm,),
        in_specs=in_specs,
        out_specs=out_specs,
        out_shape=out_shape,
        scratch_shapes=[
            pltpu.VMEM((T_ROWS, D_MODEL), jnp.bfloat16),
            pltpu.VMEM((D_MODEL, N_COLS), jnp.bfloat16),
        ],
        compiler_params=pltpu.CompilerParams(
            dimension_semantics=("arbitrary",), vmem_limit_bytes=VMEM_LIMIT),
        name="proj",
    )(x2, gmix, w_in_t, gq, gk)


def _attn_kernel(qt_ref, qit_ref, wit_ref, vt_ref, kaug_ref, kiaug_ref, qaug_ref,
                 at_ref, rhs_idx, rhs_att, sc_ref, sc16_ref, acc_t, buf_a, buf_b):
    j = pl.program_id(1)
    nq = Q_BLOCK
    kc = KEY_CHUNK
    nch = j + 1
    lanes = GROUP * nq
    tiles = GROUP * N_HEADS
    inf = jnp.float32(jnp.inf)

    def cols(t):
        return slice(t * nq, (t + 1) * nq)

    zeros64 = jnp.zeros((HEAD_DIM, nq), rhs_idx.dtype)
    for t in range(tiles):
        g, hd = divmod(t, N_HEADS)
        rows = slice(hd * HEAD_DIM, (hd + 1) * HEAD_DIM)
        rhs_idx[0:HEAD_DIM, cols(t)] = zeros64
        rhs_idx[HEAD_DIM:AUG, cols(t)] = qit_ref[g, rows, :]
        rhs_att[0:HEAD_DIM, cols(t)] = qt_ref[g, rows, :]
        rhs_att[HEAD_DIM:AUG, cols(t)] = qaug_ref[:, cols(hd)]

    t_row = j * nq + lax.broadcasted_iota(jnp.int32, (1, lanes), 1) % nq
    t_pos = j * nq + lax.broadcasted_iota(jnp.int32, (kc, nq), 1)
    s_loc = lax.broadcasted_iota(jnp.int32, (kc, nq), 0)

    def pairwise(step, n, carry):
        carry = lax.fori_loop(0, n // 2, lambda i, cr: step(2 * i + 1, step(2 * i, cr)), carry)
        return lax.cond(n % 2 == 1, lambda cr: step(n - 1, cr), lambda cr: cr, carry)

    def fold(x, op, rows=8):
        parts = [x[r * rows:(r + 1) * rows, :] for r in range(kc // rows)]
        while len(parts) > 1:
            parts = [op(parts[i], parts[i + 1]) for i in range(0, len(parts), 2)]
        return parts[0]

    bufs = (buf_a, buf_b)

    def pipelined(produce, consume, finish, carry, prepare=lambda c: None):
        def drain(c, par, carry, aux):
            part = None
            for t in range(tiles):
                part = consume(c, bufs[par], t, carry, aux, part)
            return finish(c, carry, part)

        def both(c, par, carry, aux):
            part, nxt, ctx = None, [], prepare(c + 1)
            for t in range(tiles):
                nxt.append(produce(c + 1, bufs[1 - par], t, ctx))
                part = consume(c, bufs[par], t, carry, aux, part)
            return finish(c, carry, part), tuple(nxt)

        def pair_body(i, st):
            return both(2 * i + 1, 1, *both(2 * i, 0, *st))

        ctx0 = prepare(0)
        first = tuple(produce(0, bufs[0], t, ctx0) for t in range(tiles))
        st = lax.fori_loop(0, (nch - 1) // 2, pair_body, (carry, first))

        def even_tail(st):
            return drain(nch - 1, 1, *both(nch - 2, 0, *st))

        return lax.cond(nch % 2 == 0, even_tail, lambda st: drain(nch - 1, 0, *st), st)

    def score_step(c, lo_hi):
        off = pl.multiple_of(c * kc, kc)
        causal = off + s_loc <= t_pos
        los, his = [], []
        for g in range(GROUP):
            tot = None
            for hd in range(IDX_HEADS):
                t = g * N_HEADS + hd
                x = jnp.dot(kiaug_ref[g, pl.ds(off, kc), :], rhs_idx[:, cols(t)],
                            preferred_element_type=jnp.float32)
                term = jnp.maximum(x, 0.0) * wit_ref[g, hd:hd + 1, :]
                tot = term if tot is None else tot + term
            masked = jnp.where(causal, tot, -inf)
            sc_ref[pl.ds(off, kc), cols(g)] = masked
            sc16_ref[pl.ds(off, kc), cols(g)] = masked.astype(jnp.bfloat16)
            los.append(fold(jnp.where(causal, tot, inf), jnp.minimum))
            his.append(fold(masked, jnp.maximum))
        lo, hi = lo_hi
        return (jnp.minimum(lo, jnp.concatenate(los, axis=1)),
                jnp.maximum(hi, jnp.concatenate(his, axis=1)))

    lo, hi = pairwise(score_step, nch,
                      (jnp.full((8, lanes), inf, jnp.float32),
                       jnp.full((8, lanes), -inf, jnp.float32)))
    lo = jnp.min(lo, axis=0, keepdims=True)
    hi = jnp.max(hi, axis=0, keepdims=True)

    def by_lane_tile(ref, c, fn):
        off = pl.multiple_of(c * kc, kc)
        return jnp.concatenate(
            [fn(ref[pl.ds(off, kc), l * LANES:(l + 1) * LANES], slice(l * LANES, (l + 1) * LANES))
             for l in range(lanes // LANES)], axis=1)

    def count_ge(v):
        def step(c, acc):
            return acc + by_lane_tile(
                sc_ref, c, lambda s, ln: fold(jnp.where(s >= v[:, ln], 1.0, 0.0), jnp.add))
        acc = pairwise(step, nch, jnp.zeros((8, lanes), jnp.float32))
        return jnp.sum(acc, axis=0, keepdims=True)

    def count_ge16(v):
        v16 = v.astype(jnp.bfloat16)
        one, zero = jnp.ones((), jnp.bfloat16), jnp.zeros((), jnp.bfloat16)
        def step(c, acc):
            return acc + by_lane_tile(
                sc16_ref, c,
                lambda s, ln: fold(jnp.where(s >= v16[:, ln], one, zero), jnp.add, PACK_ROWS))
        acc = pairwise(step, nch, jnp.zeros((PACK_ROWS, lanes), jnp.bfloat16))
        return jnp.sum(acc.astype(jnp.float32), axis=0, keepdims=True)

    def max_below(v):
        def step(c, acc):
            return jnp.maximum(acc, by_lane_tile(
                sc_ref, c, lambda s, ln: fold(jnp.where(s < v[:, ln], s, -inf), jnp.maximum)))
        acc = pairwise(step, nch, jnp.full((8, lanes), -inf, jnp.float32))
        return jnp.max(acc, axis=0, keepdims=True)

    def n_open(done):
        return jnp.sum(1 - done)

    def snap(state):
        lo, hi, thr, cge, done = state
        b = max_below(hi)
        c = count_ge(b)
        ok = c >= TOPK
        newly = ok & (done == 0)
        return (lo, jnp.where(ok, hi, b), jnp.where(newly, b, thr),
                jnp.where(newly, c, cge), jnp.where(ok, 1, done))

    def bisect(state):
        lo, hi, thr, cge, done = state
        mid = 0.5 * lo + 0.5 * hi
        ge = count_ge(mid) >= TOPK
        return (jnp.where(ge, mid, lo), jnp.where(ge, hi, mid), thr, cge, done)

    def bisect16(_, lo_hi):
        lo, hi = lo_hi
        mid = (0.5 * lo + 0.5 * hi).astype(jnp.bfloat16).astype(jnp.float32)
        ge = count_ge16(mid) >= TOPK
        return jnp.where(ge, mid, lo), jnp.where(ge, hi, mid)

    short = jnp.where((t_row + 1) <= TOPK, 1, 0)
    n0 = n_open(short)
    run = n0 > 0
    lo16 = lo.astype(jnp.bfloat16).astype(jnp.float32)
    hi16 = hi.astype(jnp.bfloat16).astype(jnp.float32)
    hi16 = (hi16 + jnp.abs(hi16) * (2 * BF16_STEP) + TINY).astype(jnp.bfloat16).astype(jnp.float32)
    lo16, hi16 = lax.fori_loop(0, jnp.where(run, BISECT16_STEPS, 0), bisect16, (lo16, hi16))
    lo32 = jnp.maximum(lo, lo16 - jnp.abs(lo16) * BF16_STEP - TINY)
    state = (lo32, hi16, lo, jnp.zeros((1, lanes), jnp.float32), short)
    state = lax.fori_loop(0, jnp.where(run, BISECT_STEPS, 0), lambda _, s: bisect(s), state)
    state = lax.cond(run, snap, lambda s: s, state)

    def finish_body(carry):
        s = snap(bisect(carry[0]))
        return s, n_open(s[4]), carry[2] + 1

    state, _, _ = lax.while_loop(lambda carry: (carry[1] > 0) & (carry[2] < MAX_FINISH_TRIPS),
                                 finish_body, (state, n_open(state[4]), jnp.int32(0)))
    thr, cge = state[2], state[3]

    @pl.when(jnp.max(cge) > TOPK)
    def _():
        def gt_body(c, acc):
            off = pl.multiple_of(c * kc, kc)
            return acc + fold(jnp.where(sc_ref[pl.ds(off, kc), :] > thr, 1.0, 0.0), jnp.add)
        cnt_gt = jnp.sum(lax.fori_loop(0, nch, gt_body, jnp.zeros((8, lanes), jnp.float32)),
                         axis=0, keepdims=True)
        need = TOPK - cnt_gt
        r_i = lax.broadcasted_iota(jnp.int32, (kc, kc), 0)
        c_i = lax.broadcasted_iota(jnp.int32, (kc, kc), 1)
        tri = jnp.where(c_i <= r_i, 1.0, 0.0).astype(jnp.bfloat16)

        def tie_body(c, seen):
            off = pl.multiple_of(c * kc, kc)
            s = sc_ref[pl.ds(off, kc), :]
            eq = s == thr
            eq_f = jnp.where(eq, 1.0, 0.0)
            incl = jnp.dot(tri, eq_f.astype(jnp.bfloat16), preferred_element_type=jnp.float32)
            rank = incl - eq_f + seen
            sc_ref[pl.ds(off, kc), :] = jnp.where(eq & (rank >= need), -inf, s)
            return seen + incl[kc - 1:kc, :]

        lax.fori_loop(0, nch, tie_body, jnp.zeros((1, lanes), jnp.float32))

    acc_t[...] = jnp.zeros_like(acc_t)

    def select_bias(c):
        off = pl.multiple_of(c * kc, kc)
        return jnp.where(sc_ref[pl.ds(off, kc), :] >= thr, 0.0, NEG_BIG)

    def masked_logits(c, buf, t, bias):
        g = t // N_HEADS
        off = pl.multiple_of(c * kc, kc)
        lm = jnp.dot(kaug_ref[g, pl.ds(off, kc), :], rhs_att[:, cols(t)],
                     preferred_element_type=jnp.float32) + bias[:, cols(g)]
        buf[:, cols(t)] = lm
        return jnp.max(fold(lm, jnp.maximum), axis=0, keepdims=True)

    def softmax_pv(c, buf, t, m_old, cmax, m_list):
        m_new = jnp.maximum(m_old[t], cmax[t])
        alpha = jnp.exp2(m_old[t] - m_new)
        p = jnp.exp2((buf[:, cols(t)] - m_new).astype(jnp.bfloat16))
        pv = jnp.dot(vt_ref[t // N_HEADS, c], p, preferred_element_type=jnp.float32)
        acc_t[:, cols(t)] = acc_t[:, cols(t)] * alpha + pv
        return (m_list or ()) + (m_new,)

    pipelined(masked_logits, softmax_pv, lambda c, m_old, m_list: m_list,
              tuple(jnp.full((1, nq), NEG_BIG, jnp.float32) for _ in range(tiles)),
              prepare=select_bias)

    for t in range(tiles):
        g, hd = divmod(t, N_HEADS)
        inv_l = 1.0 / acc_t[HEAD_DIM:HEAD_DIM + 1, cols(t)]
        at_ref[g, hd * HEAD_DIM:(hd + 1) * HEAD_DIM, :] = (
            acc_t[0:HEAD_DIM, cols(t)] * inv_l).astype(at_ref.dtype)


def _attn(qt, qit, wit, vt, kaug, kiaug, qaug, *, batch, seq):
    nq = Q_BLOCK
    assert KEY_CHUNK == Q_BLOCK and seq % nq == 0 and batch % GROUP == 0
    wide = GROUP * N_HEADS * nq
    qblk = lambda b, j: (b, 0, j)
    whole = lambda b, j: (b, 0, 0)
    return pl.pallas_call(
        _attn_kernel,
        grid=(batch // GROUP, seq // nq),
        in_specs=[
            pl.BlockSpec((GROUP, ATTN_WIDTH, nq), qblk),
            pl.BlockSpec((GROUP, IDX_HEADS * IDX_DIM, nq), qblk),
            pl.BlockSpec((GROUP, IDX_HEADS, nq), qblk),
            pl.BlockSpec((GROUP, seq // KEY_CHUNK, VT_ROWS, KEY_CHUNK),
                         lambda b, j: (b, 0, 0, 0)),
            pl.BlockSpec((GROUP, seq, AUG), whole),
            pl.BlockSpec((GROUP, seq, AUG), whole),
            pl.BlockSpec((AUG - HEAD_DIM, N_HEADS * nq), lambda b, j: (0, 0)),
        ],
        out_specs=pl.BlockSpec((GROUP, ATTN_WIDTH, nq), qblk),
        out_shape=jax.ShapeDtypeStruct((batch, ATTN_WIDTH, seq), jnp.bfloat16),
        scratch_shapes=[
            pltpu.VMEM((AUG, wide), jnp.bfloat16),
            pltpu.VMEM((AUG, wide), jnp.bfloat16),
            pltpu.VMEM((seq, GROUP * nq), jnp.float32),
            pltpu.VMEM((seq, GROUP * nq), jnp.bfloat16),
            pltpu.VMEM((VT_ROWS, wide), jnp.float32),
            pltpu.VMEM((KEY_CHUNK, wide), jnp.float32),
            pltpu.VMEM((KEY_CHUNK, wide), jnp.float32),
        ],
        compiler_params=pltpu.CompilerParams(
            dimension_semantics=("arbitrary", "arbitrary"), vmem_limit_bytes=VMEM_LIMIT),
        name="attn",
    )(qt, qit, wit, vt, kaug, kiaug, qaug)


def _mix_kernel(x_ref, at_ref, u_ref, halo_ref, g_ref, wpg_ref, pscale_ref, wba_ref, wbp_ref,
                wout_ref, gffn_ref, wg_ref, wu_ref, wd_ref, o_ref, ubuf, *, tm, seq):
    i = pl.program_id(0)
    tiles_per_seq = seq // tm
    first = (i % tiles_per_seq) == 0
    halo = halo_ref[...]
    ubuf[0:HALO, :] = jnp.where(first, jnp.zeros_like(halo), halo)
    ubuf[HALO:HALO + tm, :] = u_ref[...]

    t_pos = (i * tm) % seq + lax.broadcasted_iota(jnp.int32, (tm, POOL_GROUP_WIDTH), 0)
    pooled = []
    for g, w in enumerate(POOL_WINDOWS):
        lanes = slice(g * POOL_GROUP_WIDTH, (g + 1) * POOL_GROUP_WIDTH)
        levels = w.bit_length() - 1
        cur = ubuf[:, lanes]
        step = 1
        for _ in range(levels):
            cur = cur + pltpu.roll(cur, step, axis=0)
            step *= 2
        wsum = cur[HALO:, :]
        cnt = jnp.minimum(t_pos + 1, w).astype(jnp.float32)
        d = (wsum / cnt - ubuf[HALO:HALO + tm, lanes]).astype(jnp.bfloat16)
        pooled.append(jnp.dot(d, wpg_ref[g], preferred_element_type=jnp.float32))
    p = (jnp.concatenate(pooled, axis=-1) * pscale_ref[...]).astype(jnp.bfloat16)

    a_proj = lax.dot_general(at_ref[0], wba_ref[...], (((0,), (0,)), ((), ())),
                             preferred_element_type=jnp.float32)
    p_proj = jnp.dot(p, wbp_ref[...], preferred_element_type=jnp.float32)
    gates = g_ref[...].astype(jnp.float32)
    merged = (jax.nn.sigmoid(gates[:, :D_MODEL]) * a_proj
              + jax.nn.sigmoid(gates[:, D_MODEL:]) * p_proj)
    x1 = x_ref[...] + jnp.dot(merged.astype(jnp.bfloat16), wout_ref[...],
                              preferred_element_type=jnp.float32)

    h2 = (_rms(x1) * gffn_ref[...]).astype(jnp.bfloat16)
    gate = jnp.dot(h2, wg_ref[...], preferred_element_type=jnp.float32)
    up = jnp.dot(h2, wu_ref[...], preferred_element_type=jnp.float32)
    act = (jax.nn.silu(gate) * up).astype(jnp.bfloat16)
    o_ref[...] = x1 + jnp.dot(act, wd_ref[...], preferred_element_type=jnp.float32)


def _mix(x2, at, u, gates, wpg, pscale, wba, wbp, wout, gffn, wg, wu, wd, *, batch, seq, tm):
    assert seq % tm == 0 and tm % HALO == 0 and HALO >= max(POOL_WINDOWS)
    n = batch * seq
    tiles_per_seq = seq // tm
    row = lambda i: (i, 0)
    const2 = lambda i: (0, 0)
    resident = functools.partial(pl.BlockSpec, pipeline_mode=pl.Buffered(1))
    return pl.pallas_call(
        functools.partial(_mix_kernel, tm=tm, seq=seq),
        grid=(n // tm,),
        in_specs=[
            pl.BlockSpec((tm, D_MODEL), row),
            pl.BlockSpec((1, ATTN_WIDTH, tm), lambda i: (i // tiles_per_seq, 0, i % tiles_per_seq)),
            pl.BlockSpec((tm, POOL_WIDTH), row),
            pl.BlockSpec((HALO, POOL_WIDTH), lambda i: (jnp.maximum(i * (tm // HALO) - 1, 0), 0)),
            pl.BlockSpec((tm, 2 * D_MODEL), row),
            resident((len(POOL_WINDOWS), POOL_GROUP_WIDTH, POOL_GROUP_WIDTH), lambda i: (0, 0, 0)),
            resident((1, POOL_WIDTH), const2),
            resident((ATTN_WIDTH, D_MODEL), const2),
            resident((POOL_WIDTH, D_MODEL), const2),
            resident((D_MODEL, D_MODEL), const2),
            resident((1, D_MODEL), const2),
            resident((D_MODEL, D_FF), const2),
            resident((D_MODEL, D_FF), const2),
            resident((D_FF, D_MODEL), const2),
        ],
        out_specs=pl.BlockSpec((tm, D_MODEL), row),
        out_shape=jax.ShapeDtypeStruct((n, D_MODEL), jnp.float32),
        scratch_shapes=[pltpu.VMEM((tm + HALO, POOL_WIDTH), jnp.float32)],
        compiler_params=pltpu.CompilerParams(
            dimension_semantics=("arbitrary",), vmem_limit_bytes=VMEM_LIMIT),
        name="mix",
    )(x2, at, u, u, gates, wpg, pscale, wba, wbp, wout, gffn, wg, wu, wd)


def _alibi_rows():
    slopes = 2.0 ** (-8.0 * jnp.arange(1, N_HEADS + 1, dtype=jnp.float32) / N_HEADS) * LOG2E
    parts, rest = [], slopes
    for _ in range(SLOPE_PARTS):
        term = rest.astype(jnp.bfloat16)
        parts.append(term)
        rest = rest - term.astype(jnp.float32)
    block = jnp.stack(parts * 2, axis=0)
    block = jnp.repeat(block, Q_BLOCK, axis=1)
    pad = jnp.zeros((AUG - HEAD_DIM - 2 * SLOPE_PARTS, N_HEADS * Q_BLOCK), jnp.bfloat16)
    return jnp.concatenate([block, pad], axis=0)


def _layer(x2, norm_mix, w_in, q_norm, k_norm, w_pool_group, pool_scale, w_branch_attn,
           w_branch_pool, w_out, norm_ffn, w_ffn_gate, w_ffn_up, w_ffn_down, *, batch, seq):
    bf = jnp.bfloat16
    gk = jnp.concatenate([k_norm, jnp.zeros((LANES - HEAD_DIM,), k_norm.dtype)])[None, :]

    qt, qit, wit, vt, kaug, kiaug, u, gates = _proj(
        x2, norm_mix[None, :], w_in, q_norm[:, None], gk, batch=batch, seq=seq, tm=PROJ_ROWS)
    at = _attn(qt, qit, wit, vt, kaug, kiaug, _alibi_rows(), batch=batch, seq=seq)
    return _mix(x2, at, u, gates, w_pool_group.astype(bf), pool_scale[None, :],
                w_branch_attn.astype(bf), w_branch_pool.astype(bf), w_out.astype(bf),
                norm_ffn[None, :], w_ffn_gate.astype(bf), w_ffn_up.astype(bf),
                w_ffn_down.astype(bf), batch=batch, seq=seq, tm=MIX_ROWS)


def kernel(x, norm_mix, w_in, q_norm, k_norm, w_pool_group, pool_scale, w_branch_attn,
           w_branch_pool, w_out, norm_ffn, w_ffn_gate, w_ffn_up, w_ffn_down):
    batch, seq, d = x.shape
    x2 = x.reshape(batch * seq, d)
    w_in_t = jnp.swapaxes(w_in, 1, 2)
    for l in range(norm_mix.shape[0]):
        x2 = _layer(x2, norm_mix[l], (w_in_t, l), q_norm[l], k_norm[l], w_pool_group[l],
                    pool_scale[l], w_branch_attn[l], w_branch_pool[l], w_out[l], norm_ffn[l],
                    w_ffn_gate[l], w_ffn_up[l], w_ffn_down[l], batch=batch, seq=seq)
    return x2.reshape(batch, seq, d)
```

```python
import functools

import jax
import jax.numpy as jnp
from jax import lax
from jax.experimental import pallas as pl
from jax.experimental.pallas import tpu as pltpu

D_MODEL = 1024
N_HEADS = 8
HEAD_DIM = 64
ATTN_WIDTH = N_HEADS * HEAD_DIM
IDX_HEADS = 8
IDX_DIM = 64
IDX_SCALE = (IDX_HEADS ** -0.5) * (IDX_DIM ** -0.5)
TOPK = 256
POOL_WINDOWS = (2, 4, 8, 16)
POOL_WIDTH = 512
POOL_GROUP_WIDTH = 128
D_FF = 2816
RMS_EPS = 1e-6
NEG_BIG = -1e30

LANES = 128
PROJ_ROWS = 1024
MIX_ROWS = 512
Q_BLOCK = 256
GROUP = 4
KEY_CHUNK = 256
AUG = 128
VT_ROWS = 80
HALO = 16
LOG2E = 1.4426950408889634
SLOPE_PARTS = 3
PACK_ROWS = 16
BF16_STEP = 2.0 ** -7
BF16_EXACT_INT = 256
BISECT16_STEPS = 10
BISECT_STEPS = 6
TINY = 1e-30
MAX_FINISH_TRIPS = 512
VMEM_LIMIT = 56 * 1024 * 1024

_WIDTHS = (ATTN_WIDTH, HEAD_DIM, HEAD_DIM, IDX_HEADS * IDX_DIM, IDX_DIM, IDX_HEADS,
           POOL_WIDTH, 2 * D_MODEL)
W_Q0, W_K0, W_V0, W_QI0, W_KI0, W_WI0, W_U0, W_G0, W_END = (
    sum(_WIDTHS[:n]) for n in range(len(_WIDTHS) + 1))
T_Q0 = 0
T_V0 = T_Q0 + ATTN_WIDTH
T_QI0 = T_V0 + HEAD_DIM
T_WI0 = T_QI0 + IDX_HEADS * IDX_DIM
T_ROWS = -(-(T_WI0 + IDX_HEADS) // PACK_ROWS) * PACK_ROWS
N_KK0 = 0
N_U0 = N_KK0 + LANES
N_G0 = N_U0 + POOL_WIDTH
N_COLS = N_G0 + 2 * D_MODEL


def _rms(x, eps=RMS_EPS):
    return x * lax.rsqrt(jnp.mean(x * x, axis=-1, keepdims=True) + eps)


def _regroup_w_in(w_ref, wt_ref, wn_ref):
    bf = jnp.bfloat16
    wt_ref[T_Q0:T_V0, :] = w_ref[W_Q0:W_K0, :].astype(bf)
    wt_ref[T_V0:T_WI0, :] = w_ref[W_V0:W_KI0, :].astype(bf)
    tail = jnp.concatenate(
        [w_ref[W_WI0:W_U0, :], jnp.zeros((T_ROWS - T_WI0 - IDX_HEADS, D_MODEL), jnp.float32)],
        axis=0)
    wt_ref[T_WI0:T_ROWS, :] = tail.astype(bf)
    k_ki = jnp.concatenate([w_ref[W_K0:W_V0, :], w_ref[W_KI0:W_WI0, :]], axis=0)
    wn_ref[:, N_KK0:N_KK0 + LANES] = k_ki.T.astype(bf)
    wn_ref[:, N_U0:N_COLS] = w_ref[W_U0:W_END, :].T.astype(bf)


def _proj_kernel(x_ref, gmix_ref, w_ref, gq_ref, gk_ref,
                 qt_ref, qit_ref, wit_ref, vt_ref, kaug_ref, kiaug_ref, u_ref, g_ref,
                 wt_ref, wn_ref, *, tm, seq):
    i = pl.program_id(0)

    @pl.when(i == 0)
    def _():
        _regroup_w_in(w_ref, wt_ref, wn_ref)

    x = x_ref[...]
    h = (_rms(x) * gmix_ref[...]).astype(jnp.bfloat16)
    y_t = lax.dot_general(wt_ref[...], h, (((1,), (1,)), ((), ())),
                          preferred_element_type=jnp.float32)
    y_n = jnp.dot(h, wn_ref[...], preferred_element_type=jnp.float32)

    gq = gq_ref[...]
    for hd in range(N_HEADS):
        blk = y_t[T_Q0 + hd * HEAD_DIM:T_Q0 + (hd + 1) * HEAD_DIM, :]
        ss = jnp.mean(blk * blk, axis=0, keepdims=True)
        qn = blk * lax.rsqrt(ss + RMS_EPS) * gq * (HEAD_DIM ** -0.5 * LOG2E)
        qt_ref[0, hd * HEAD_DIM:(hd + 1) * HEAD_DIM, :] = qn.astype(qt_ref.dtype)

    qit_ref[0] = y_t[T_QI0:T_QI0 + IDX_HEADS * IDX_DIM, :].astype(qit_ref.dtype)
    wit_ref[0] = y_t[T_WI0:T_WI0 + IDX_HEADS, :] * IDX_SCALE

    v_t = y_t[T_V0:T_V0 + HEAD_DIM, :]
    row = lax.broadcasted_iota(jnp.int32, (VT_ROWS - HEAD_DIM, tm), 0)
    tail = jnp.where(row == 0, 1.0, 0.0)
    vt_full = jnp.concatenate([v_t, tail], axis=0).astype(vt_ref.dtype)
    for cc in range(tm // KEY_CHUNK):
        vt_ref[0, cc] = vt_full[:, cc * KEY_CHUNK:(cc + 1) * KEY_CHUNK]

    kk = y_n[:, N_KK0:N_KK0 + LANES]
    lane = lax.broadcasted_iota(jnp.int32, (tm, LANES), 1)
    is_k = lane < HEAD_DIM
    ss = jnp.sum(jnp.where(is_k, kk * kk, 0.0), axis=-1, keepdims=True) * (1.0 / HEAD_DIM)
    kn = kk * lax.rsqrt(ss + RMS_EPS) * gk_ref[...]
    pos = (i * tm) % seq + lax.broadcasted_iota(jnp.int32, (tm, LANES), 0)
    pos_lo = pos & (BF16_EXACT_INT - 1)
    pos_hi = (pos - pos_lo).astype(jnp.float32)
    pos_lo = pos_lo.astype(jnp.float32)
    feats = jnp.where(lane < HEAD_DIM + SLOPE_PARTS, pos_hi,
                      jnp.where(lane < HEAD_DIM + 2 * SLOPE_PARTS, pos_lo, 0.0))
    kaug_ref[0] = jnp.where(is_k, kn, feats).astype(kaug_ref.dtype)
    kiaug_ref[0] = jnp.where(is_k, 0.0, kk).astype(kiaug_ref.dtype)

    u_ref[...] = y_n[:, N_U0:N_U0 + POOL_WIDTH]
    g_ref[...] = y_n[:, N_G0:N_G0 + 2 * D_MODEL].astype(g_ref.dtype)


def _proj(x2, gmix, w_in_layers, gq, gk, *, batch, seq, tm):
    w_in_t, layer = w_in_layers
    assert seq % tm == 0 and tm % KEY_CHUNK == 0 and w_in_t.shape[1:] == (W_END, D_MODEL)
    n = batch * seq
    tiles_per_seq = seq // tm
    const = lambda i: (0, 0)
    bidx = lambda i: (i // tiles_per_seq, 0, i % tiles_per_seq)
    out_shape = (
        jax.ShapeDtypeStruct((batch, ATTN_WIDTH, seq), jnp.bfloat16),
        jax.ShapeDtypeStruct((batch, IDX_HEADS * IDX_DIM, seq), jnp.bfloat16),
        jax.ShapeDtypeStruct((batch, IDX_HEADS, seq), jnp.float32),
        jax.ShapeDtypeStruct((batch, seq // KEY_CHUNK, VT_ROWS, KEY_CHUNK), jnp.bfloat16),
        jax.ShapeDtypeStruct((batch, seq, AUG), jnp.bfloat16),
        jax.ShapeDtypeStruct((batch, seq, AUG), jnp.bfloat16),
        jax.ShapeDtypeStruct((n, POOL_WIDTH), jnp.float32),
        jax.ShapeDtypeStruct((n, 2 * D_MODEL), jnp.bfloat16),
    )
    out_specs = (
        pl.BlockSpec((1, ATTN_WIDTH, tm), bidx),
        pl.BlockSpec((1, IDX_HEADS * IDX_DIM, tm), bidx),
        pl.BlockSpec((1, IDX_HEADS, tm), bidx),
        pl.BlockSpec((1, tm // KEY_CHUNK, VT_ROWS, KEY_CHUNK),
                     lambda i: (i // tiles_per_seq, i % tiles_per_seq, 0, 0)),
        pl.BlockSpec((1, tm, AUG), lambda i: (i // tiles_per_seq, i % tiles_per_seq, 0)),
        pl.BlockSpec((1, tm, AUG), lambda i: (i // tiles_per_seq, i % tiles_per_seq, 0)),
        pl.BlockSpec((tm, POOL_WIDTH), lambda i: (i, 0)),
        pl.BlockSpec((tm, 2 * D_MODEL), lambda i: (i, 0)),
    )
    in_specs = [
        pl.BlockSpec((tm, D_MODEL), lambda i: (i, 0)),
        pl.BlockSpec((1, D_MODEL), const),
        pl.BlockSpec((None,) + w_in_t.shape[1:], lambda i: (layer, 0, 0),
                     pipeline_mode=pl.Buffered(1)),
        pl.BlockSpec((HEAD_DIM, 1), const),
        pl.BlockSpec((1, LANES), const),
    ]
    return pl.pallas_call(
        functools.partial(_proj_kernel, tm=tm, seq=seq),
        grid=(n // tm,),
        in_specs=in_specs,
        out_specs=out_specs,
        out_shape=out_shape,
        scratch_shapes=[
            pltpu.VMEM((T_ROWS, D_MODEL), jnp.bfloat16),
            pltpu.VMEM((D_MODEL, N_COLS), jnp.bfloat16),
        ],
        compiler_params=pltpu.CompilerParams(
            dimension_semantics=("arbitrary",), vmem_limit_bytes=VMEM_LIMIT),
        name="proj",
    )(x2, gmix, w_in_t, gq, gk)


def _attn_kernel(qt_ref, qit_ref, wit_ref, vt_ref, kaug_ref, kiaug_ref, qaug_ref,
                 at_ref, rhs_idx, rhs_att, sc_ref, sc16_ref, acc_t, buf_a, buf_b):
    j = pl.program_id(1)
    nq = Q_BLOCK
    kc = KEY_CHUNK
    nch = j + 1
    lanes = GROUP * nq
    tiles = GROUP * N_HEADS
    inf = jnp.float32(jnp.inf)

    def cols(t):
        return slice(t * nq, (t + 1) * nq)

    zeros64 = jnp.zeros((HEAD_DIM, nq), rhs_idx.dtype)
    for t in range(tiles):
        g, hd = divmod(t, N_HEADS)
        rows = slice(hd * HEAD_DIM, (hd + 1) * HEAD_DIM)
        rhs_idx[0:HEAD_DIM, cols(t)] = zeros64
        rhs_idx[HEAD_DIM:AUG, cols(t)] = qit_ref[g, rows, :]
        rhs_att[0:HEAD_DIM, cols(t)] = qt_ref[g, rows, :]
        rhs_att[HEAD_DIM:AUG, cols(t)] = qaug_ref[:, cols(hd)]

    t_row = j * nq + lax.broadcasted_iota(jnp.int32, (1, lanes), 1) % nq
    t_pos = j * nq + lax.broadcasted_iota(jnp.int32, (kc, nq), 1)
    s_loc = lax.broadcasted_iota(jnp.int32, (kc, nq), 0)

    def pairwise(step, n, carry):
        carry = lax.fori_loop(0, n // 2, lambda i, cr: step(2 * i + 1, step(2 * i, cr)), carry)
        return lax.cond(n % 2 == 1, lambda cr: step(n - 1, cr), lambda cr: cr, carry)

    def fold(x, op, rows=8):
        parts = [x[r * rows:(r + 1) * rows, :] for r in range(kc // rows)]
        while len(parts) > 1:
            parts = [op(parts[i], parts[i + 1]) for i in range(0, len(parts), 2)]
        return parts[0]

    bufs = (buf_a, buf_b)

    def pipelined(produce, consume, finish, carry, prepare=lambda c: None):
        def drain(c, par, carry, aux):
            part = None
            for t in range(tiles):
                part = consume(c, bufs[par], t, carry, aux, part)
            return finish(c, carry, part)

        def both(c, par, carry, aux):
            part, nxt, ctx = None, [], prepare(c + 1)
            for t in range(tiles):
                nxt.append(produce(c + 1, bufs[1 - par], t, ctx))
                part = consume(c, bufs[par], t, carry, aux, part)
            return finish(c, carry, part), tuple(nxt)

        def pair_body(i, st):
            return both(2 * i + 1, 1, *both(2 * i, 0, *st))

        ctx0 = prepare(0)
        first = tuple(produce(0, bufs[0], t, ctx0) for t in range(tiles))
        st = lax.fori_loop(0, (nch - 1) // 2, pair_body, (carry, first))

        def even_tail(st):
            return drain(nch - 1, 1, *both(nch - 2, 0, *st))

        return lax.cond(nch % 2 == 0, even_tail, lambda st: drain(nch - 1, 0, *st), st)

    def score_step(c, lo_hi):
        off = pl.multiple_of(c * kc, kc)
        causal = off + s_loc <= t_pos
        los, his = [], []
        for g in range(GROUP):
            tot = None
            for hd in range(IDX_HEADS):
                t = g * N_HEADS + hd
                x = jnp.dot(kiaug_ref[g, pl.ds(off, kc), :], rhs_idx[:, cols(t)],
                            preferred_element_type=jnp.float32)
                term = jnp.maximum(x, 0.0) * wit_ref[g, hd:hd + 1, :]
                tot = term if tot is None else tot + term
            masked = jnp.where(causal, tot, -inf)
            sc_ref[pl.ds(off, kc), cols(g)] = masked
            sc16_ref[pl.ds(off, kc), cols(g)] = masked.astype(jnp.bfloat16)
            los.append(fold(jnp.where(causal, tot, inf), jnp.minimum))
            his.append(fold(masked, jnp.maximum))
        lo, hi = lo_hi
        return (jnp.minimum(lo, jnp.concatenate(los, axis=1)),
                jnp.maximum(hi, jnp.concatenate(his, axis=1)))

    lo, hi = pairwise(score_step, nch,
                      (jnp.full((8, lanes), inf, jnp.float32),
                       jnp.full((8, lanes), -inf, jnp.float32)))
    lo = jnp.min(lo, axis=0, keepdims=True)
    hi = jnp.max(hi, axis=0, keepdims=True)

    def by_lane_tile(ref, c, fn):
        off = pl.multiple_of(c * kc, kc)
        return jnp.concatenate(
            [fn(ref[pl.ds(off, kc), l * LANES:(l + 1) * LANES], slice(l * LANES, (l + 1) * LANES))
             for l in range(lanes // LANES)], axis=1)

    def count_ge(v):
        def step(c, acc):
            return acc + by_lane_tile(
                sc_ref, c, lambda s, ln: fold(jnp.where(s >= v[:, ln], 1.0, 0.0), jnp.add))
        acc = pairwise(step, nch, jnp.zeros((8, lanes), jnp.float32))
        return jnp.sum(acc, axis=0, keepdims=True)

    def count_ge16(v):
        v16 = v.astype(jnp.bfloat16)
        one, zero = jnp.ones((), jnp.bfloat16), jnp.zeros((), jnp.bfloat16)
        def step(c, acc):
            return acc + by_lane_tile(
                sc16_ref, c,
                lambda s, ln: fold(jnp.where(s >= v16[:, ln], one, zero), jnp.add, PACK_ROWS))
        acc = pairwise(step, nch, jnp.zeros((PACK_ROWS, lanes), jnp.bfloat16))
        return jnp.sum(acc.astype(jnp.float32), axis=0, keepdims=True)

    def max_below(v):
        def step(c, acc):
            return jnp.maximum(acc, by_lane_tile(
                sc_ref, c, lambda s, ln: fold(jnp.where(s < v[:, ln], s, -inf), jnp.maximum)))
        acc = pairwise(step, nch, jnp.full((8, lanes), -inf, jnp.float32))
        return jnp.max(acc, axis=0, keepdims=True)

    def n_open(done):
        return jnp.sum(1 - done)

    def snap(state):
        lo, hi, thr, cge, done = state
        b = max_below(hi)
        c = count_ge(b)
        ok = c >= TOPK
        newly = ok & (done == 0)
        return (lo, jnp.where(ok, hi, b), jnp.where(newly, b, thr),
                jnp.where(newly, c, cge), jnp.where(ok, 1, done))

    def bisect(state):
        lo, hi, thr, cge, done = state
        mid = 0.5 * lo + 0.5 * hi
        ge = count_ge(mid) >= TOPK
        return (jnp.where(ge, mid, lo), jnp.where(ge, hi, mid), thr, cge, done)

    def bisect16(_, lo_hi):
        lo, hi = lo_hi
        mid = (0.5 * lo + 0.5 * hi).astype(jnp.bfloat16).astype(jnp.float32)
        ge = count_ge16(mid) >= TOPK
        return jnp.where(ge, mid, lo), jnp.where(ge, hi, mid)

    short = jnp.where((t_row + 1) <= TOPK, 1, 0)
    n0 = n_open(short)
    run = n0 > 0
    lo16 = lo.astype(jnp.bfloat16).astype(jnp.float32)
    hi16 = hi.astype(jnp.bfloat16).astype(jnp.float32)
    hi16 = (hi16 + jnp.abs(hi16) * (2 * BF16_STEP) + TINY).astype(jnp.bfloat16).astype(jnp.float32)
    lo16, hi16 = lax.fori_loop(0, jnp.where(run, BISECT16_STEPS, 0), bisect16, (lo16, hi16))
    lo32 = jnp.maximum(lo, lo16 - jnp.abs(lo16) * BF16_STEP - TINY)
    state = (lo32, hi16, lo, jnp.zeros((1, lanes), jnp.float32), short)
    state = lax.fori_loop(0, jnp.where(run, BISECT_STEPS, 0), lambda _, s: bisect(s), state)
    state = lax.cond(run, snap, lambda s: s, state)

    def finish_body(carry):
        s = snap(bisect(carry[0]))
        return s, n_open(s[4]), carry[2] + 1

    state, _, _ = lax.while_loop(lambda carry: (carry[1] > 0) & (carry[2] < MAX_FINISH_TRIPS),
                                 finish_body, (state, n_open(state[4]), jnp.int32(0)))
    thr, cge = state[2], state[3]

    @pl.when(jnp.max(cge) > TOPK)
    def _():
        def gt_body(c, acc):
            off = pl.multiple_of(c * kc, kc)
            return acc + fold(jnp.where(sc_ref[pl.ds(off, kc), :] > thr, 1.0, 0.0), jnp.add)
        cnt_gt = jnp.sum(lax.fori_loop(0, nch, gt_body, jnp.zeros((8, lanes), jnp.float32)),
                         axis=0, keepdims=True)
        need = TOPK - cnt_gt
        r_i = lax.broadcasted_iota(jnp.int32, (kc, kc), 0)
        c_i = lax.broadcasted_iota(jnp.int32, (kc, kc), 1)
        tri = jnp.where(c_i <= r_i, 1.0, 0.0).astype(jnp.bfloat16)

        def tie_body(c, seen):
            off = pl.multiple_of(c * kc, kc)
            s = sc_ref[pl.ds(off, kc), :]
            eq = s == thr
            eq_f = jnp.where(eq, 1.0, 0.0)
            incl = jnp.dot(tri, eq_f.astype(jnp.bfloat16), preferred_element_type=jnp.float32)
            rank = incl - eq_f + seen
            sc_ref[pl.ds(off, kc), :] = jnp.where(eq & (rank >= need), -inf, s)
            return seen + incl[kc - 1:kc, :]

        lax.fori_loop(0, nch, tie_body, jnp.zeros((1, lanes), jnp.float32))

    acc_t[...] = jnp.zeros_like(acc_t)

    def select_bias(c):
        off = pl.multiple_of(c * kc, kc)
        return jnp.where(sc_ref[pl.ds(off, kc), :] >= thr, 0.0, NEG_BIG)

    def masked_logits(c, buf, t, bias):
        g = t // N_HEADS
        off = pl.multiple_of(c * kc, kc)
        lm = jnp.dot(kaug_ref[g, pl.ds(off, kc), :], rhs_att[:, cols(t)],
                     preferred_element_type=jnp.float32) + bias[:, cols(g)]
        buf[:, cols(t)] = lm
        return jnp.max(fold(lm, jnp.maximum), axis=0, keepdims=True)

    def softmax_pv(c, buf, t, m_old, cmax, m_list):
        m_new = jnp.maximum(m_old[t], cmax[t])
        alpha = jnp.exp2(m_old[t] - m_new)
        p = jnp.exp2((buf[:, cols(t)] - m_new).astype(jnp.bfloat16))
        pv = jnp.dot(vt_ref[t // N_HEADS, c], p, preferred_element_type=jnp.float32)
        acc_t[:, cols(t)] = acc_t[:, cols(t)] * alpha + pv
        return (m_list or ()) + (m_new,)

    pipelined(masked_logits, softmax_pv, lambda c, m_old, m_list: m_list,
              tuple(jnp.full((1, nq), NEG_BIG, jnp.float32) for _ in range(tiles)),
              prepare=select_bias)

    for t in range(tiles):
        g, hd = divmod(t, N_HEADS)
        inv_l = 1.0 / acc_t[HEAD_DIM:HEAD_DIM + 1, cols(t)]
        at_ref[g, hd * HEAD_DIM:(hd + 1) * HEAD_DIM, :] = (
            acc_t[0:HEAD_DIM, cols(t)] * inv_l).astype(at_ref.dtype)


def _attn(qt, qit, wit, vt, kaug, kiaug, qaug, *, batch, seq):
    nq = Q_BLOCK
    assert KEY_CHUNK == Q_BLOCK and seq % nq == 0 and batch % GROUP == 0
    wide = GROUP * N_HEADS * nq
    qblk = lambda b, j: (b, 0, j)
    whole = lambda b, j: (b, 0, 0)
    return pl.pallas_call(
        _attn_kernel,
        grid=(batch // GROUP, seq // nq),
        in_specs=[
            pl.BlockSpec((GROUP, ATTN_WIDTH, nq), qblk),
            pl.BlockSpec((GROUP, IDX_HEADS * IDX_DIM, nq), qblk),
            pl.BlockSpec((GROUP, IDX_HEADS, nq), qblk),
            pl.BlockSpec((GROUP, seq // KEY_CHUNK, VT_ROWS, KEY_CHUNK),
                         lambda b, j: (b, 0, 0, 0)),
            pl.BlockSpec((GROUP, seq, AUG), whole),
            pl.BlockSpec((GROUP, seq, AUG), whole),
            pl.BlockSpec((AUG - HEAD_DIM, N_HEADS * nq), lambda b, j: (0, 0)),
        ],
        out_specs=pl.BlockSpec((GROUP, ATTN_WIDTH, nq), qblk),
        out_shape=jax.ShapeDtypeStruct((batch, ATTN_WIDTH, seq), jnp.bfloat16),
        scratch_shapes=[
            pltpu.VMEM((AUG, wide), jnp.bfloat16),
            pltpu.VMEM((AUG, wide), jnp.bfloat16),
            pltpu.VMEM((seq, GROUP * nq), jnp.float32),
            pltpu.VMEM((seq, GROUP * nq), jnp.bfloat16),
            pltpu.VMEM((VT_ROWS, wide), jnp.float32),
            pltpu.VMEM((KEY_CHUNK, wide), jnp.float32),
            pltpu.VMEM((KEY_CHUNK, wide), jnp.float32),
        ],
        compiler_params=pltpu.CompilerParams(
            dimension_semantics=("arbitrary", "arbitrary"), vmem_limit_bytes=VMEM_LIMIT),
        name="attn",
    )(qt, qit, wit, vt, kaug, kiaug, qaug)


def _mix_kernel(x_ref, at_ref, u_ref, halo_ref, g_ref, wpg_ref, pscale_ref, wba_ref, wbp_ref,
                wout_ref, gffn_ref, wg_ref, wu_ref, wd_ref, o_ref, ubuf, *, tm, seq):
    i = pl.program_id(0)
    tiles_per_seq = seq // tm
    first = (i % tiles_per_seq) == 0
    halo = halo_ref[...]
    ubuf[0:HALO, :] = jnp.where(first, jnp.zeros_like(halo), halo)
    ubuf[HALO:HALO + tm, :] = u_ref[...]

    t_pos = (i * tm) % seq + lax.broadcasted_iota(jnp.int32, (tm, POOL_GROUP_WIDTH), 0)
    pooled = []
    for g, w in enumerate(POOL_WINDOWS):
        lanes = slice(g * POOL_GROUP_WIDTH, (g + 1) * POOL_GROUP_WIDTH)
        levels = w.bit_length() - 1
        cur = ubuf[:, lanes]
        step = 1
        for _ in range(levels):
            cur = cur + pltpu.roll(cur, step, axis=0)
            step *= 2
        wsum = cur[HALO:, :]
        cnt = jnp.minimum(t_pos + 1, w).astype(jnp.float32)
        d = (wsum / cnt - ubuf[HALO:HALO + tm, lanes]).astype(jnp.bfloat16)
        pooled.append(jnp.dot(d, wpg_ref[g], preferred_element_type=jnp.float32))
    p = (jnp.concatenate(pooled, axis=-1) * pscale_ref[...]).astype(jnp.bfloat16)

    a_proj = lax.dot_general(at_ref[0], wba_ref[...], (((0,), (0,)), ((), ())),
                             preferred_element_type=jnp.float32)
    p_proj = jnp.dot(p, wbp_ref[...], preferred_element_type=jnp.float32)
    gates = g_ref[...].astype(jnp.float32)
    merged = (jax.nn.sigmoid(gates[:, :D_MODEL]) * a_proj
              + jax.nn.sigmoid(gates[:, D_MODEL:]) * p_proj)
    x1 = x_ref[...] + jnp.dot(merged.astype(jnp.bfloat16), wout_ref[...],
                              preferred_element_type=jnp.float32)

    h2 = (_rms(x1) * gffn_ref[...]).astype(jnp.bfloat16)
    gate = jnp.dot(h2, wg_ref[...], preferred_element_type=jnp.float32)
    up = jnp.dot(h2, wu_ref[...], preferred_element_type=jnp.float32)
    act = (jax.nn.silu(gate) * up).astype(jnp.bfloat16)
    o_ref[...] = x1 + jnp.dot(act, wd_ref[...], preferred_element_type=jnp.float32)


def _mix(x2, at, u, gates, wpg, pscale, wba, wbp, wout, gffn, wg, wu, wd, *, batch, seq, tm):
    assert seq % tm == 0 and tm % HALO == 0 and HALO >= max(POOL_WINDOWS)
    n = batch * seq
    tiles_per_seq = seq // tm
    row = lambda i: (i, 0)
    const2 = lambda i: (0, 0)
    resident = functools.partial(pl.BlockSpec, pipeline_mode=pl.Buffered(1))
    return pl.pallas_call(
        functools.partial(_mix_kernel, tm=tm, seq=seq),
        grid=(n // tm,),
        in_specs=[
            pl.BlockSpec((tm, D_MODEL), row),
            pl.BlockSpec((1, ATTN_WIDTH, tm), lambda i: (i // tiles_per_seq, 0, i % tiles_per_seq)),
            pl.BlockSpec((tm, POOL_WIDTH), row),
            pl.BlockSpec((HALO, POOL_WIDTH), lambda i: (jnp.maximum(i * (tm // HALO) - 1, 0), 0)),
            pl.BlockSpec((tm, 2 * D_MODEL), row),
            resident((len(POOL_WINDOWS), POOL_GROUP_WIDTH, POOL_GROUP_WIDTH), lambda i: (0, 0, 0)),
            resident((1, POOL_WIDTH), const2),
            resident((ATTN_WIDTH, D_MODEL), const2),
            resident((POOL_WIDTH, D_MODEL), const2),
            resident((D_MODEL, D_MODEL), const2),
            resident((1, D_MODEL), const2),
            resident((D_MODEL, D_FF), const2),
            resident((D_MODEL, D_FF), const2),
            resident((D_FF, D_MODEL), const2),
        ],
        out_specs=pl.BlockSpec((tm, D_MODEL), row),
        out_shape=jax.ShapeDtypeStruct((n, D_MODEL), jnp.float32),
        scratch_shapes=[pltpu.VMEM((tm + HALO, POOL_WIDTH), jnp.float32)],
        compiler_params=pltpu.CompilerParams(
            dimension_semantics=("arbitrary",), vmem_limit_bytes=VMEM_LIMIT,
            allow_input_fusion=[False] * 5 + [True, False, True, True, True, False,
                                              True, True, True]),
        name="mix",
    )(x2, at, u, u, gates, wpg, pscale, wba, wbp, wout, gffn, wg, wu, wd)


def _alibi_rows():
    slopes = 2.0 ** (-8.0 * jnp.arange(1, N_HEADS + 1, dtype=jnp.float32) / N_HEADS) * LOG2E
    parts, rest = [], slopes
    for _ in range(SLOPE_PARTS):
        term = rest.astype(jnp.bfloat16)
        parts.append(term)
        rest = rest - term.astype(jnp.float32)
    block = jnp.stack(parts * 2, axis=0)
    block = jnp.repeat(block, Q_BLOCK, axis=1)
    pad = jnp.zeros((AUG - HEAD_DIM - 2 * SLOPE_PARTS, N_HEADS * Q_BLOCK), jnp.bfloat16)
    return jnp.concatenate([block, pad], axis=0)


def _layer(x2, norm_mix, w_in, q_norm, k_norm, w_pool_group, pool_scale, w_branch_attn,
           w_branch_pool, w_out, norm_ffn, w_ffn_gate, w_ffn_up, w_ffn_down, *, batch, seq):
    bf = jnp.bfloat16
    gk = jnp.concatenate([k_norm, jnp.zeros((LANES - HEAD_DIM,), k_norm.dtype)])[None, :]

    qt, qit, wit, vt, kaug, kiaug, u, gates = _proj(
        x2, norm_mix[None, :], w_in, q_norm[:, None], gk, batch=batch, seq=seq, tm=PROJ_ROWS)
    at = _attn(qt, qit, wit, vt, kaug, kiaug, _alibi_rows(), batch=batch, seq=seq)
    return _mix(x2, at, u, gates, w_pool_group.astype(bf), pool_scale[None, :],
                w_branch_attn.astype(bf), w_branch_pool.astype(bf), w_out.astype(bf),
                norm_ffn[None, :], w_ffn_gate.astype(bf), w_ffn_up.astype(bf),
                w_ffn_down.astype(bf), batch=batch, seq=seq, tm=MIX_ROWS)


def kernel(x, norm_mix, w_in, q_norm, k_norm, w_pool_group, pool_scale, w_branch_attn,
           w_branch_pool, w_out, norm_ffn, w_ffn_gate, w_ffn_up, w_ffn_down):
    batch, seq, d = x.shape
    x2 = x.reshape(batch * seq, d)
    w_in_t = jnp.swapaxes(w_in, 1, 2)
    for l in range(norm_mix.shape[0]):
        x2 = _layer(x2, norm_mix[l], (w_in_t, l), q_norm[l], k_norm[l], w_pool_group[l],
                    pool_scale[l], w_branch_attn[l], w_branch_pool[l], w_out[l], norm_ffn[l],
                    w_ffn_gate[l], w_ffn_up[l], w_ffn_down[l], batch=batch, seq=seq)
    return x2.reshape(batch, seq, d)
```
